```python
import jax, jax.numpy as jnp
from jax import lax
import numpy as np

D_MODEL = 1024
BATCH = 16
SEQ = 2048
DEPTH = 1
DEC_BATCH = 128
DEC_SEQ = 4
PAST_LEN = 8192
PAGE_SIZE = 128

HEAD_DIM = 64
N_HEADS = 8
ATTN_WIDTH = N_HEADS * HEAD_DIM
POOL_WIDTH = D_MODEL // 2
POOL_WINDOWS = (2, 4, 8, 16)
N_POOL_GROUPS = len(POOL_WINDOWS)
POOL_GROUP = POOL_WIDTH // N_POOL_GROUPS
POOL_HIST = max(POOL_WINDOWS) - 1
D_MIX = ATTN_WIDTH + POOL_WIDTH
PROJ_WIDTH = 3 * ATTN_WIDTH + POOL_WIDTH
DIL_PATTERNS = ((128, 1), (512, 4), (2048, 16))
N_STRIDED = 128
MAX_WINDOW = 2048
Q_BLOCK = 128
D_FF = 2816
CONV_WIDTH = 3
EPS = 1e-6
NEG = -1e30

kernel_name = 'hybrid_dilated_attn_pool_convffn_step'


def _rmsnorm(x, g):
    x32 = x.astype(jnp.float32)
    y = x32 * lax.rsqrt(jnp.mean(x32 * x32, axis=-1, keepdims=True) + EPS)
    return (y * g.astype(jnp.float32)).astype(x.dtype)


def _alibi_slopes():
    return jnp.asarray(2.0 ** (-8.0 * np.arange(1, N_HEADS + 1) / N_HEADS), dtype=jnp.float32)


def _dilated_branch_prompt(q, k, v, dilation, slopes):
    b, s, h, dh = q.shape
    L = s // dilation
    nb = -(-L // Q_BLOCK)
    Lp = nb * Q_BLOCK
    qs = q.reshape(b, L, dilation, h, dh)
    ks = k.reshape(b, L, dilation, h, dh)
    vs = v.reshape(b, L, dilation, h, dh)
    qb = jnp.pad(qs, ((0, 0), (0, Lp - L), (0, 0), (0, 0), (0, 0))).reshape(b, nb, Q_BLOCK, dilation, h, dh)
    pad_kv = ((0, 0), (Q_BLOCK, Lp - L), (0, 0), (0, 0), (0, 0))
    kp = jnp.pad(ks, pad_kv).reshape(b, nb + 1, Q_BLOCK, dilation, h, dh)
    vp = jnp.pad(vs, pad_kv).reshape(b, nb + 1, Q_BLOCK, dilation, h, dh)
    kb = jnp.concatenate([kp[:, :-1], kp[:, 1:]], axis=2)
    vb = jnp.concatenate([vp[:, :-1], vp[:, 1:]], axis=2)
    scores = jnp.einsum('bnqchd,bnkchd->bnchqk', qb, kb).astype(jnp.float32) * (HEAD_DIM ** -0.5)
    qi = jnp.arange(Q_BLOCK)[:, None]
    ki = jnp.arange(2 * Q_BLOCK)[None, :]
    diff = qi + Q_BLOCK - ki
    key_j = jnp.arange(nb)[:, None, None] * Q_BLOCK + ki[None] - Q_BLOCK
    valid = ((diff >= 0) & (diff <= N_STRIDED))[None] & (key_j >= 0)
    bias = -slopes[:, None, None] * (dilation * diff).astype(jnp.float32)[None]
    scores = jnp.where(valid[None, :, None, None], scores + bias[None, None, None], NEG)
    m = jnp.max(scores, axis=-1, keepdims=True)
    p = jnp.exp(scores - m)
    l = jnp.sum(p, axis=-1, keepdims=True)
    out = jnp.einsum('bnchqk,bnkchd->bnqchd', (p / l).astype(v.dtype), vb)
    lse = (m + jnp.log(l))[..., 0]
    out = out.reshape(b, Lp, dilation, h, dh)[:, :L].reshape(b, s, h, dh)
    lse = jnp.transpose(lse, (0, 1, 4, 2, 3)).reshape(b, Lp, dilation, h)[:, :L].reshape(b, s, h)
    return out, lse


def _dilated_branch_sample(q, k_all, v_all, dilation, slopes, n_hist):
    t = q.shape[1]
    n = jnp.arange(t)[:, None]
    kk = jnp.arange(N_STRIDED + 1)[None, :]
    idx = n_hist + n - kk * dilation
    valid = idx >= 0
    idx_c = jnp.maximum(idx, 0)
    kg = k_all[:, idx_c]
    vg = v_all[:, idx_c]
    scores = jnp.einsum('bthd,btkhd->bhtk', q, kg).astype(jnp.float32) * (HEAD_DIM ** -0.5)
    bias = -slopes[:, None, None] * (dilation * kk).astype(jnp.float32)[None]
    scores = jnp.where(valid[None, None], scores + bias[None], NEG)
    m = jnp.max(scores, axis=-1, keepdims=True)
    p = jnp.exp(scores - m)
    l = jnp.sum(p, axis=-1, keepdims=True)
    out = jnp.einsum('bhtk,btkhd->bthd', (p / l).astype(v_all.dtype), vg)
    lse = jnp.transpose((m + jnp.log(l))[..., 0], (0, 2, 1))
    return out, lse


def _merge_branches(branches):
    outs = jnp.stack([o for o, _ in branches], 0).astype(jnp.float32)
    lses = jnp.stack([s for _, s in branches], 0)
    w = jax.nn.softmax(lses, axis=0)[..., None]
    return jnp.sum(w * outs, axis=0)


def _pool_mixer(u_ext, pos, w_pool, pool_scale):
    b = u_ext.shape[0]
    t = pos.shape[0]
    u32 = u_ext.astype(jnp.float32)
    c = jnp.pad(jnp.cumsum(u32, axis=1), ((0, 0), (1, 0), (0, 0)))
    end = POOL_HIST + 1 + jnp.arange(t)
    groups = []
    for g, w in enumerate(POOL_WINDOWS):
        sl = slice(g * POOL_GROUP, (g + 1) * POOL_GROUP)
        wsum = c[:, end, sl] - c[:, end - w, sl]
        cnt = jnp.minimum(w, pos + 1).astype(jnp.float32)[None, :, None]
        groups.append(wsum / cnt)
    pooled = jnp.concatenate(groups, axis=-1)
    d = (pooled - u32[:, POOL_HIST:]).astype(u_ext.dtype).reshape(b, t, N_POOL_GROUPS, POOL_GROUP)
    mixed = jnp.einsum('btgc,gcd->btgd', d, w_pool).reshape(b, t, POOL_WIDTH)
    return mixed * pool_scale


def _conv_ffn(xn, hist, w_up, conv_w, conv_b, w_down):
    up = xn @ w_up
    ext = jnp.concatenate([hist, up], axis=1)
    t = up.shape[1]
    conv = conv_b + sum(ext[:, j:j + t] * conv_w[j] for j in range(CONV_WIDTH))
    gate, val = jnp.split(conv, 2, axis=-1)
    out = (jax.nn.silu(gate) * val) @ w_down
    return out, ext[:, -(CONV_WIDTH - 1):]


def _layer(x, pos, kv_hist, pool_prefix, ffn_prefix, g_attn_norm, w_in, g_q, g_k, w_pool, pool_scale,
           w_out, g_ffn_norm, w_up, conv_w, conv_b, w_down):
    b, t, _ = x.shape
    xn = _rmsnorm(x, g_attn_norm)
    proj = xn @ w_in
    q, k, v, u = jnp.split(proj, [ATTN_WIDTH, 2 * ATTN_WIDTH, 3 * ATTN_WIDTH], axis=-1)
    q = _rmsnorm(q.reshape(b, t, N_HEADS, HEAD_DIM), g_q)
    k = _rmsnorm(k.reshape(b, t, N_HEADS, HEAD_DIM), g_k)
    v = v.reshape(b, t, N_HEADS, HEAD_DIM)
    slopes = _alibi_slopes()
    if kv_hist is None:
        branches = [_dilated_branch_prompt(q, k, v, r, slopes) for (_, r) in DIL_PATTERNS]
        n_keep = min(MAX_WINDOW, t)
        new_k, new_v = k[:, t - n_keep:], v[:, t - n_keep:]
    else:
        k_hist, v_hist = kv_hist
        n_hist = k_hist.shape[1]
        k_all = jnp.concatenate([k_hist, k], axis=1)
        v_all = jnp.concatenate([v_hist, v], axis=1)
        branches = [_dilated_branch_sample(q, k_all, v_all, r, slopes, n_hist) for (_, r) in DIL_PATTERNS]
        new_k, new_v = k, v
    attn = _merge_branches(branches).astype(x.dtype).reshape(b, t, ATTN_WIDTH)
    u_ext = jnp.concatenate([pool_prefix, u], axis=1)
    pool_out = _pool_mixer(u_ext, pos, w_pool, pool_scale)
    new_pool = u_ext[:, -POOL_HIST:]
    h = x + jnp.concatenate([attn, pool_out.astype(x.dtype)], axis=-1) @ w_out
    ffn_out, new_ffn = _conv_ffn(_rmsnorm(h, g_ffn_norm), ffn_prefix, w_up, conv_w, conv_b, w_down)
    return h + ffn_out, new_k, new_v, new_pool, new_ffn


def setup_inputs(seed: int = 0) -> dict:
    key = jax.random.key(seed)
    ks = jax.random.split(key, 20)
    f32 = jnp.float32
    wb = min(MAX_WINDOW, PAST_LEN)

    def nrm(k, shape, scale):
        return jax.random.normal(k, shape, f32) * scale

    return {
        'x_prompt': nrm(ks[0], (BATCH, SEQ, D_MODEL), 1.0),
        'x_sample': nrm(ks[1], (DEC_BATCH, DEC_SEQ, D_MODEL), 1.0),
        'cache_k': nrm(ks[2], (DEPTH, DEC_BATCH, wb, N_HEADS, HEAD_DIM), 1.0),
        'cache_v': nrm(ks[3], (DEPTH, DEC_BATCH, wb, N_HEADS, HEAD_DIM), 1.0),
        'state_pool': nrm(ks[4], (DEPTH, DEC_BATCH, POOL_HIST, POOL_WIDTH), 1.0),
        'state_ffn_conv': nrm(ks[5], (DEPTH, DEC_BATCH, CONV_WIDTH - 1, 2 * D_FF), 1.0),
        'g_attn_norm': 1.0 + nrm(ks[6], (DEPTH, D_MODEL), 0.02),
        'w_in': nrm(ks[7], (DEPTH, D_MODEL, PROJ_WIDTH), D_MODEL ** -0.5),
        'g_q': 1.0 + nrm(ks[8], (DEPTH, N_HEADS, HEAD_DIM), 0.02),
        'g_k': 1.0 + nrm(ks[9], (DEPTH, N_HEADS, HEAD_DIM), 0.02),
        'w_pool': nrm(ks[10], (DEPTH, N_POOL_GROUPS, POOL_GROUP, POOL_GROUP), POOL_GROUP ** -0.5),
        'pool_scale': 1.0 + nrm(ks[11], (DEPTH, POOL_WIDTH), 0.02),
        'w_out': nrm(ks[12], (DEPTH, D_MIX, D_MODEL), D_MIX ** -0.5),
        'g_ffn_norm': 1.0 + nrm(ks[13], (DEPTH, D_MODEL), 0.02),
        'w_up': nrm(ks[14], (DEPTH, D_MODEL, 2 * D_FF), D_MODEL ** -0.5),
        'conv_w': nrm(ks[15], (DEPTH, CONV_WIDTH, 2 * D_FF), CONV_WIDTH ** -0.5),
        'conv_b': nrm(ks[16], (DEPTH, 2 * D_FF), 0.02),
        'w_down': nrm(ks[17], (DEPTH, D_FF, D_MODEL), D_FF ** -0.5),
    }


def reference(x_prompt, x_sample, cache_k, cache_v, state_pool, state_ffn_conv, g_attn_norm, w_in, g_q, g_k,
              w_pool, pool_scale, w_out, g_ffn_norm, w_up, conv_w, conv_b, w_down):
    pos_p = jnp.arange(x_prompt.shape[1])
    pos_s = PAST_LEN + jnp.arange(x_sample.shape[1])
    yp, ys = x_prompt, x_sample
    kp_l, vp_l, pp_l, fp_l, ks_l, vs_l, ps_l, fs_l = [], [], [], [], [], [], [], []
    for l in range(DEPTH):
        wts = (g_attn_norm[l], w_in[l], g_q[l], g_k[l], w_pool[l], pool_scale[l], w_out[l], g_ffn_norm[l],
               w_up[l], conv_w[l], conv_b[l], w_down[l])
        pool0 = jnp.zeros((yp.shape[0], POOL_HIST, POOL_WIDTH), yp.dtype)
        ffn0 = jnp.zeros((yp.shape[0], CONV_WIDTH - 1, 2 * D_FF), yp.dtype)
        yp, kp, vp, pp, fp = _layer(yp, pos_p, None, pool0, ffn0, *wts)
        ys, kn, vn, pn, fn = _layer(ys, pos_s, (cache_k[l], cache_v[l]), state_pool[l], state_ffn_conv[l], *wts)
        kp_l.append(kp); vp_l.append(vp); pp_l.append(pp); fp_l.append(fp)
        ks_l.append(kn); vs_l.append(vn); ps_l.append(pn); fs_l.append(fn)
    new_k_prompt = jnp.stack(kp_l, 0)
    new_v_prompt = jnp.stack(vp_l, 0)
    new_pool_prompt = jnp.stack(pp_l, 0)
    new_ffn_prompt = jnp.stack(fp_l, 0)
    new_k_sample = jnp.stack(ks_l, 0)
    new_v_sample = jnp.stack(vs_l, 0)
    new_pool_sample = jnp.stack(ps_l, 0)
    new_ffn_sample = jnp.stack(fs_l, 0)
    return (yp, ys, new_k_prompt, new_v_prompt, new_pool_prompt, new_ffn_prompt, new_k_sample, new_v_sample, new_pool_sample, new_ffn_sample)
```

```python
import functools

import numpy as np
import jax
import jax.numpy as jnp
from jax import lax
from jax.experimental import pallas as pl
from jax.experimental.pallas import tpu as pltpu

D_MODEL = 1024
HEAD_DIM = 64
N_HEADS = 8
ATTN_WIDTH = N_HEADS * HEAD_DIM
POOL_WIDTH = 512
POOL_WINDOWS = (2, 4, 8, 16)
POOL_GROUP = 128
POOL_HIST = 15
PROJ_WIDTH = 3 * ATTN_WIDTH + POOL_WIDTH
DILATIONS = (1, 4, 16)
N_STRIDED = 128
BLK = 128
D_FF = 2816
CONV_WIDTH = 3
PAST_LEN = 8192
EPS = 1e-6
NEG = -1e30

F32 = jnp.float32
BF16 = jnp.bfloat16

TM = 512
FF_CHUNK = 256
N_FF_CHUNKS = D_FF // FF_CHUNK
POOL_HEAD = 24
CONV_HEAD = 8
VMEM_LIMIT = 56 * 1024 * 1024


def _params(sem, vmem=VMEM_LIMIT):
    return pltpu.CompilerParams(dimension_semantics=sem, vmem_limit_bytes=vmem)


def _const_spec(shape):
    nd = len(shape)
    return pl.BlockSpec(shape, lambda *_: (0,) * nd, pipeline_mode=pl.Buffered(1))


def _proj_kernel(x_ref, g_ref, w_ref, gq_ref, gk_ref, hm_ref, q_ref, k_ref, v_ref, u_ref):
    x = x_ref[...]
    xn = x * lax.rsqrt(jnp.mean(x * x, axis=-1, keepdims=True) + EPS) * g_ref[...]
    proj = jnp.dot(xn.astype(BF16), w_ref[...], preferred_element_type=F32)

    def head_norm(t, g):
        ms = jnp.dot((t * t).astype(BF16), hm_ref[...], preferred_element_type=F32)
        return t * lax.rsqrt(ms + EPS) * g

    q = head_norm(proj[:, :ATTN_WIDTH], gq_ref[...])
    q_ref[...] = q * (HEAD_DIM ** -0.5)
    k_ref[...] = head_norm(proj[:, ATTN_WIDTH:2 * ATTN_WIDTH], gk_ref[...])
    v_ref[...] = proj[:, 2 * ATTN_WIDTH:3 * ATTN_WIDTH]
    u_ref[...] = proj[:, 3 * ATTN_WIDTH:]


def _proj(x2d, rows, n_steps, x_map, o_map, out_rows, g_attn, w_in, gq, gk, head_mean):
    out = jax.ShapeDtypeStruct((out_rows[0], out_rows[1]), F32)
    return pl.pallas_call(
        _proj_kernel,
        grid=(n_steps,),
        in_specs=[
            pl.BlockSpec((rows, D_MODEL), x_map),
            _const_spec((1, D_MODEL)),
            _const_spec((D_MODEL, PROJ_WIDTH)),
            _const_spec((1, ATTN_WIDTH)),
            _const_spec((1, ATTN_WIDTH)),
            _const_spec((ATTN_WIDTH, ATTN_WIDTH)),
        ],
        out_specs=[pl.BlockSpec((rows, ATTN_WIDTH), o_map)] * 4,
        out_shape=[out] * 4,
        compiler_params=_params(("arbitrary",)),
        name="proj",
    )(x2d, g_attn, w_in, gq, gk, head_mean)


def _attn_prompt_kernel(slopes_ref, q_ref, k_ref, v_ref, o_ref, bias_s, acc_s, m_s, l_s):
    hp = pl.program_id(1)
    seq = q_ref.shape[0]
    lane = lax.broadcasted_iota(jnp.int32, (BLK, 2 * HEAD_DIM), 1)
    left = lane < HEAD_DIM

    row = lax.broadcasted_iota(jnp.int32, (2 * BLK, 2 * BLK), 0)
    col = lax.broadcasted_iota(jnp.int32, (2 * BLK, 2 * BLK), 1)
    diff = (row & (BLK - 1)) + BLK - col
    valid = (diff >= 0) & (diff <= N_STRIDED)
    slope = jnp.where(row < BLK, slopes_ref[2 * hp], slopes_ref[2 * hp + 1])
    for pi, r in enumerate(DILATIONS):
        bias_s[pi] = jnp.where(valid, -slope * (r * diff).astype(F32), NEG)

    def rows_of(start, r, n=BLK):
        return pl.ds(start, n) if r == 1 else pl.ds(start, n, stride=r)

    def unit(pi, r, start, with_prev):
        cur = rows_of(start, r)
        q2 = q_ref[cur, :]
        qq = jnp.concatenate([jnp.where(left, q2, 0.0), jnp.where(left, 0.0, q2)], axis=0).astype(BF16)
        if with_prev:
            if r == 1:
                both = pl.ds(start - BLK, 2 * BLK)
                kk = k_ref[both, :]
                vv = v_ref[both, :]
            else:
                prev = rows_of(start - BLK * r, r)
                kk = jnp.concatenate([k_ref[prev, :], k_ref[cur, :]], axis=0)
                vv = jnp.concatenate([v_ref[prev, :], v_ref[cur, :]], axis=0)
            bias = bias_s[pi]
        else:
            kk = k_ref[cur, :]
            vv = v_ref[cur, :]
            bias = bias_s[pi, :, BLK:]
        s = lax.dot_general(qq, kk.astype(BF16), (((1,), (1,)), ((), ())), preferred_element_type=F32) + bias
        m = jnp.max(s, axis=-1, keepdims=True)
        p = jnp.exp(s - m)
        l = jnp.sum(p, axis=-1, keepdims=True)
        pv = jnp.dot(p.astype(BF16), vv.astype(BF16), preferred_element_type=F32)
        acc_s[pi, cur, :] = jnp.where(left, pv[:BLK], pv[BLK:])
        m_s[pi, cur, :] = jnp.where(left, m[:BLK], m[BLK:])
        l_s[pi, cur, :] = jnp.where(left, l[:BLK], l[BLK:])

    for pi, r in enumerate(DILATIONS):
        n_blocks = seq // (r * BLK)

        def first(c, carry, pi=pi, r=r):
            unit(pi, r, c, False)
            return carry

        lax.fori_loop(0, r, first, 0)
        if n_blocks > 1:
            def rest(i, carry, pi=pi, r=r):
                c = i % r
                blk = i // r + 1
                unit(pi, r, c + blk * (BLK * r), True)
                return carry

            lax.fori_loop(0, r * (n_blocks - 1), rest, 0)

    def merge(i, carry):
        rows = pl.ds(pl.multiple_of(i * BLK, BLK), BLK)
        ms = [m_s[pi, rows, :] for pi in range(3)]
        mx = jnp.maximum(jnp.maximum(ms[0], ms[1]), ms[2])
        num = jnp.zeros((BLK, 2 * HEAD_DIM), F32)
        den = jnp.zeros((BLK, 2 * HEAD_DIM), F32)
        for pi in range(3):
            a = jnp.exp(ms[pi] - mx)
            num = num + a * acc_s[pi, rows, :]
            den = den + a * l_s[pi, rows, :]
        o_ref[rows, :] = (num / den).astype(o_ref.dtype)
        return carry

    lax.fori_loop(0, seq // BLK, merge, 0)


def _attn_prompt(slopes, q, k, v, batch, seq):
    blk = pl.BlockSpec((seq, 2 * HEAD_DIM), lambda b, h: (b, h))
    return pl.pallas_call(
        _attn_prompt_kernel,
        grid=(batch, N_HEADS // 2),
        in_specs=[pl.BlockSpec(memory_space=pltpu.SMEM), blk, blk, blk],
        out_specs=blk,
        out_shape=jax.ShapeDtypeStruct((batch * seq, ATTN_WIDTH), BF16),
        scratch_shapes=[
            pltpu.VMEM((3, 2 * BLK, 2 * BLK), F32),
            pltpu.VMEM((3, seq, 2 * HEAD_DIM), F32),
            pltpu.VMEM((3, seq, 2 * HEAD_DIM), F32),
            pltpu.VMEM((3, seq, 2 * HEAD_DIM), F32),
        ],
        compiler_params=_params(("arbitrary", "arbitrary")),
        name="attn_prompt",
    )(slopes, q, k, v)


N_NEW = 4
QROWS = N_NEW * N_HEADS


def _attn_sample_kernel(slope_ref, q_ref, kn_ref, vn_ref, kt_ref, vt_ref, o_ref, bias_s, mult_s):
    W = ATTN_WIDTH
    wb = kt_ref.shape[2]
    slope = slope_ref[...]
    row1 = lax.broadcasted_iota(jnp.int32, (QROWS, 1), 0)
    n_of_row = row1 // N_HEADS

    @pl.when(pl.program_id(0) == 0)
    def _():
        t = lax.broadcasted_iota(jnp.int32, (QROWS, wb), 1)
        dist = wb + n_of_row - t
        mult = jnp.zeros((QROWS, wb), F32)
        for r in DILATIONS:
            hit = ((dist & (r - 1)) == 0) & (dist <= r * N_STRIDED)
            mult = mult + jnp.where(hit, 1.0, 0.0)
        mult_s[...] = mult
        bias_s[...] = jnp.where(mult > 0.0, -slope * dist.astype(F32), NEG)

    q = q_ref[0]
    kn = kn_ref[0]
    vn = vn_ref[0]
    sub = lax.broadcasted_iota(jnp.int32, (N_HEADS, W), 0)
    own = sub == lax.broadcasted_iota(jnp.int32, (N_HEADS, W), 1) // HEAD_DIM
    own_rows = jnp.concatenate([own] * N_NEW, axis=0)
    qbd = jnp.concatenate(
        [jnp.where(own, jnp.broadcast_to(q[n:n + 1], (N_HEADS, W)), 0.0) for n in range(N_NEW)], axis=0)

    s = jnp.dot(qbd.astype(BF16), kt_ref[0].astype(BF16), preferred_element_type=F32) + bias_s[...]
    s_new, w_new = [], []
    for m in range(N_NEW):
        sm = jnp.sum(qbd * kn[m:m + 1], axis=-1, keepdims=True)
        gap = n_of_row - m
        s_new.append(jnp.where(gap >= 0, sm - slope * gap.astype(F32), NEG))
        w_new.append(jnp.where(gap == 0, float(len(DILATIONS)), jnp.where(gap > 0, 1.0, 0.0)))
    mx = jnp.max(s, axis=-1, keepdims=True)
    for sm in s_new:
        mx = jnp.maximum(mx, sm)
    p = jnp.exp(s - mx) * mult_s[...]
    den = jnp.sum(p, axis=-1, keepdims=True)
    pv = lax.dot_general(p.astype(BF16), vt_ref[0].astype(BF16), (((1,), (1,)), ((), ())),
                         preferred_element_type=F32)
    for m in range(N_NEW):
        pm = jnp.exp(s_new[m] - mx) * w_new[m]
        den = den + pm
        pv = pv + pm * vn[m:m + 1]
    full = jnp.where(own_rows, pv / den, 0.0)
    o_ref[0] = jnp.sum(full.reshape(N_NEW, N_HEADS, W), axis=1).astype(o_ref.dtype)


def _attn_sample(slope_rows, q, kn, vn, cache_kt, cache_vt):
    nseq, W, wb = cache_kt.shape
    new_spec = pl.BlockSpec((1, N_NEW, W), lambda s: (s, 0, 0))
    cache_spec = pl.BlockSpec((1, W, wb), lambda s: (s, 0, 0))
    return pl.pallas_call(
        _attn_sample_kernel,
        grid=(nseq,),
        in_specs=[_const_spec((QROWS, 1)), new_spec, new_spec, new_spec, cache_spec, cache_spec],
        out_specs=new_spec,
        out_shape=jax.ShapeDtypeStruct((nseq, N_NEW, W), BF16),
        scratch_shapes=[pltpu.VMEM((QROWS, wb), F32), pltpu.VMEM((QROWS, wb), F32)],
        compiler_params=_params(("arbitrary",)),
        name="attn_sample",
    )(slope_rows, q, kn, vn, cache_kt, cache_vt)


def _pool_mix(d_groups, wp_ref, ps_ref):
    mixed = [jnp.dot(d.astype(BF16), wp_ref[g], preferred_element_type=F32) for g, d in enumerate(d_groups)]
    return jnp.concatenate(mixed, axis=-1) * ps_ref[...]


def _out_proj(x, attn, pool_out, wo_ref):
    h = x + jnp.dot(attn, wo_ref[:ATTN_WIDTH, :], preferred_element_type=F32)
    return h + jnp.dot(pool_out.astype(BF16), wo_ref[ATTN_WIDTH:, :], preferred_element_type=F32)


def _mix_prompt_kernel(x_ref, a_ref, u_ref, wp_ref, ps_ref, wo_ref, h_ref, e_s, s2_s, s4_s, s8_s,
                       *, tiles_per_seq):
    i = pl.program_id(0)
    tm = u_ref.shape[0]
    H = POOL_HEAD
    G = POOL_GROUP

    @pl.when(i == 0)
    def _():
        e_s[0:8, :] = jnp.zeros((8, POOL_WIDTH), F32)
        s2_s[0:8, :] = jnp.zeros((8, POOL_WIDTH), F32)
        s4_s[0:8, :] = jnp.zeros((8, 3 * G), F32)
        s8_s[0:8, :] = jnp.zeros((8, 2 * G), F32)

    @pl.when(i % tiles_per_seq == 0)
    def _():
        e_s[8:H, :] = jnp.zeros((H - 8, POOL_WIDTH), F32)

    u = u_ref[...]
    e_s[H:, :] = u
    n = tm + H - 8
    s2_s[8:, :] = e_s[8:, :] + e_s[7:7 + n, :]
    s4_s[8:, :] = s2_s[8:, G:] + s2_s[6:6 + n, G:]
    s8_s[8:, :] = s4_s[8:, G:] + s4_s[4:4 + n, G:]
    sums = [
        s2_s[H:, :G],
        s4_s[H:, :G],
        s8_s[H:, :G],
        s8_s[H:, G:] + s8_s[H - 8:H - 8 + tm, G:],
    ]
    pos = (i % tiles_per_seq) * tm + lax.broadcasted_iota(jnp.int32, (tm, 1), 0)
    d = []
    for g, w in enumerate(POOL_WINDOWS):
        cnt = jnp.minimum(w, pos + 1).astype(F32)
        d.append(sums[g] / cnt - u[:, g * G:(g + 1) * G])
    pool_out = _pool_mix(d, wp_ref, ps_ref)
    h_ref[...] = _out_proj(x_ref[...], a_ref[...], pool_out, wo_ref)
    e_s[8:H, :] = e_s[tm + 8:tm + H, :]


def _mix_prompt(x2d, attn, u, w_pool, pool_scale, w_out, seq):
    n_tok = x2d.shape[0]
    G = POOL_GROUP
    rows = TM + POOL_HEAD
    return pl.pallas_call(
        functools.partial(_mix_prompt_kernel, tiles_per_seq=seq // TM),
        grid=(n_tok // TM,),
        in_specs=[
            pl.BlockSpec((TM, D_MODEL), lambda i: (i, 0)),
            pl.BlockSpec((TM, ATTN_WIDTH), lambda i: (i, 0)),
            pl.BlockSpec((TM, POOL_WIDTH), lambda i: (i, 0)),
            _const_spec((len(POOL_WINDOWS), G, G)),
            _const_spec((1, POOL_WIDTH)),
            _const_spec((D_MODEL, D_MODEL)),
        ],
        out_specs=pl.BlockSpec((TM, D_MODEL), lambda i: (i, 0)),
        out_shape=jax.ShapeDtypeStruct((n_tok, D_MODEL), F32),
        scratch_shapes=[
            pltpu.VMEM((rows, POOL_WIDTH), F32),
            pltpu.VMEM((rows, POOL_WIDTH), F32),
            pltpu.VMEM((rows, 3 * G), F32),
            pltpu.VMEM((rows, 2 * G), F32),
        ],
        compiler_params=_params(("arbitrary",)),
        name="mix_prompt",
    )(x2d, attn, u, w_pool, pool_scale, w_out)


def _mix_sample_kernel(x_ref, a_ref, u_ref, st_ref, wp_ref, ps_ref, wo_ref, h_ref):
    G = POOL_GROUP
    PW = POOL_WIDTH

    def row(t, lanes):
        if t < POOL_HIST:
            return st_ref[t, :, lanes]
        return u_ref[:, (t - POOL_HIST) * PW + lanes.start:(t - POOL_HIST) * PW + lanes.stop]

    for n in range(N_NEW):
        t = POOL_HIST + n
        d = []
        for g, w in enumerate(POOL_WINDOWS):
            lanes = slice(g * G, (g + 1) * G)
            cur = row(t, lanes)
            tot = cur
            for j in range(1, w):
                tot = tot + row(t - j, lanes)
            d.append(tot / float(min(w, PAST_LEN + n + 1)) - cur)
        pool_out = _pool_mix(d, wp_ref, ps_ref)
        cols = slice(n * D_MODEL, (n + 1) * D_MODEL)
        attn = a_ref[:, n * ATTN_WIDTH:(n + 1) * ATTN_WIDTH]
        h_ref[:, cols] = _out_proj(x_ref[:, cols], attn, pool_out, wo_ref)


def _mix_sample(x, attn, u, state_pool, w_pool, pool_scale, w_out):
    nseq = x.shape[0]
    full = lambda a: pl.BlockSpec(a.shape, lambda i: (0,) * a.ndim)
    args = (x, attn, u, state_pool, w_pool, pool_scale, w_out)
    return pl.pallas_call(
        _mix_sample_kernel,
        grid=(1,),
        in_specs=[full(a) for a in args],
        out_specs=pl.BlockSpec((nseq, N_NEW * D_MODEL), lambda i: (0, 0)),
        out_shape=jax.ShapeDtypeStruct((nseq, N_NEW * D_MODEL), F32),
        compiler_params=_params(("arbitrary",)),
        name="mix_sample",
    )(*args)


def _rms(h, g_ref):
    return (h * lax.rsqrt(jnp.mean(h * h, axis=-1, keepdims=True) + EPS) * g_ref[...]).astype(BF16)


def _silu_gate(gate, val):
    return (gate / (1.0 + jnp.exp(-gate)) * val).astype(BF16)


def _ffn_prompt_kernel(h_ref, g_ref, wu_ref, cw_ref, cb_ref, wd_ref, y_ref, hist_ref, ext_s, carry_s,
                       *, tiles_per_seq):
    i = pl.program_id(0)
    tm = h_ref.shape[0]
    C = FF_CHUNK
    HD = CONV_HEAD

    @pl.when(i % tiles_per_seq == 0)
    def _():
        carry_s[...] = jnp.zeros(carry_s.shape, F32)

    h = h_ref[...]
    hn = _rms(h, g_ref)
    y_ref[...] = h
    for c in range(N_FF_CHUNKS):
        halves = []
        for half in range(2):
            cols = slice(half * D_FF + c * C, half * D_FF + (c + 1) * C)
            up = jnp.dot(hn, wu_ref[:, cols], preferred_element_type=F32)
            ext = ext_s.at[half]
            ext[0:HD, :] = carry_s[:, cols]
            ext[HD:, :] = up
            conv = cb_ref[:, cols] + sum(
                ext[HD - (CONV_WIDTH - 1) + j:HD - (CONV_WIDTH - 1) + j + tm, :] * cw_ref[j:j + 1, cols]
                for j in range(CONV_WIDTH))
            carry_s[:, cols] = ext[tm:tm + HD, :]
            halves.append(conv)
        act = _silu_gate(halves[0], halves[1])
        y_ref[...] += jnp.dot(act, wd_ref[c * C:(c + 1) * C, :], preferred_element_type=F32)
    hist_ref[0] = carry_s[...]


def _ffn_prompt(h, g_ffn, w_up, conv_w, conv_b, w_down, seq):
    n_tok = h.shape[0]
    tps = seq // TM
    return pl.pallas_call(
        functools.partial(_ffn_prompt_kernel, tiles_per_seq=tps),
        grid=(n_tok // TM,),
        in_specs=[
            pl.BlockSpec((TM, D_MODEL), lambda i: (i, 0)),
            _const_spec((1, D_MODEL)),
            _const_spec((D_MODEL, 2 * D_FF)),
            _const_spec((CONV_WIDTH, 2 * D_FF)),
            _const_spec((1, 2 * D_FF)),
            _const_spec((D_FF, D_MODEL)),
        ],
        out_specs=[
            pl.BlockSpec((TM, D_MODEL), lambda i: (i, 0)),
            pl.BlockSpec((1, CONV_HEAD, 2 * D_FF), lambda i: (i // tps, 0, 0)),
        ],
        out_shape=[
            jax.ShapeDtypeStruct((n_tok, D_MODEL), F32),
            jax.ShapeDtypeStruct((n_tok // seq, CONV_HEAD, 2 * D_FF), F32),
        ],
        scratch_shapes=[
            pltpu.VMEM((2, TM + CONV_HEAD, FF_CHUNK), F32),
            pltpu.VMEM((CONV_HEAD, 2 * D_FF), F32),
        ],
        compiler_params=_params(("arbitrary",)),
        name="ffn_prompt",
    )(h, g_ffn, w_up, conv_w, conv_b, w_down)


def _ffn_sample_kernel(h_ref, st_ref, g_ref, wu_ref, cw_ref, cb_ref, wd_ref, y_ref, hist_ref):
    nseq = h_ref.shape[0]
    C = FF_CHUNK
    K = CONV_WIDTH
    hs = [h_ref[:, n * D_MODEL:(n + 1) * D_MODEL] for n in range(N_NEW)]
    hn = jnp.concatenate([_rms(h, g_ref) for h in hs], axis=0)
    for n in range(N_NEW):
        y_ref[:, n * D_MODEL:(n + 1) * D_MODEL] = hs[n]
    for c in range(N_FF_CHUNKS):
        halves = []
        for half in range(2):
            cols = slice(half * D_FF + c * C, half * D_FF + (c + 1) * C)
            up = jnp.dot(hn, wu_ref[:, cols], preferred_element_type=F32)
            rows = [st_ref[:, t * 2 * D_FF + cols.start:t * 2 * D_FF + cols.stop] for t in range(K - 1)]
            rows += [up[n * nseq:(n + 1) * nseq] for n in range(N_NEW)]
            conv = [cb_ref[:, cols] + sum(rows[n + j] * cw_ref[j:j + 1, cols] for j in range(K))
                    for n in range(N_NEW)]
            for t in range(K - 1):
                hist_ref[:, t * 2 * D_FF + cols.start:t * 2 * D_FF + cols.stop] = rows[N_NEW + t]
            halves.append(jnp.concatenate(conv, axis=0))
        act = _silu_gate(halves[0], halves[1])
        out = jnp.dot(act, wd_ref[c * C:(c + 1) * C, :], preferred_element_type=F32)
        for n in range(N_NEW):
            y_ref[:, n * D_MODEL:(n + 1) * D_MODEL] += out[n * nseq:(n + 1) * nseq]


def _ffn_sample(h, state_ffn, g_ffn, w_up, conv_w, conv_b, w_down):
    nseq = h.shape[0]
    full = lambda a: pl.BlockSpec(a.shape, lambda i: (0,) * a.ndim, pipeline_mode=pl.Buffered(1))
    args = (h, state_ffn, g_ffn, w_up, conv_w, conv_b, w_down)
    return pl.pallas_call(
        _ffn_sample_kernel,
        grid=(1,),
        in_specs=[full(a) for a in args],
        out_specs=[
            pl.BlockSpec((nseq, N_NEW * D_MODEL), lambda i: (0, 0)),
            pl.BlockSpec((nseq, (CONV_WIDTH - 1) * 2 * D_FF), lambda i: (0, 0)),
        ],
        out_shape=[
            jax.ShapeDtypeStruct((nseq, N_NEW * D_MODEL), F32),
            jax.ShapeDtypeStruct((nseq, (CONV_WIDTH - 1) * 2 * D_FF), F32),
        ],
        compiler_params=_params(("arbitrary",)),
        name="ffn_sample",
    )(*args)


def kernel(x_prompt, x_sample, cache_k, cache_v, state_pool, state_ffn_conv, g_attn_norm, w_in, g_q, g_k,
           w_pool, pool_scale, w_out, g_ffn_norm, w_up, conv_w, conv_b, w_down):
    depth = w_in.shape[0]
    assert depth == 1
    batch, seq, _ = x_prompt.shape
    nseq, n_new, _ = x_sample.shape
    assert n_new == N_NEW and seq % (16 * BLK) == 0 and seq % TM == 0
    assert cache_k.shape[2] == 16 * N_STRIDED

    slopes = jnp.asarray(2.0 ** (-8.0 * np.arange(1, N_HEADS + 1) / N_HEADS), dtype=F32)
    head_of_lane = np.arange(ATTN_WIDTH) // HEAD_DIM
    head_mean = jnp.asarray((head_of_lane[:, None] == head_of_lane[None, :]) / HEAD_DIM, dtype=BF16)

    l = 0
    w_in_b = w_in[l].astype(BF16)
    w_pool_b = w_pool[l].astype(BF16)
    w_out_b = w_out[l].astype(BF16)
    w_up_b = w_up[l].astype(BF16)
    w_down_b = w_down[l].astype(BF16)
    g_attn = g_attn_norm[l].reshape(1, D_MODEL)
    g_ffn = g_ffn_norm[l].reshape(1, D_MODEL)
    gq = g_q[l].reshape(1, ATTN_WIDTH)
    gk = g_k[l].reshape(1, ATTN_WIDTH)
    ps = pool_scale[l].reshape(1, POOL_WIDTH)
    cb = conv_b[l].reshape(1, 2 * D_FF)
    cw = conv_w[l]

    n_tok = batch * seq
    xp = x_prompt.reshape(n_tok, D_MODEL)
    qp, kp, vp, up = _proj(xp, TM, n_tok // TM, lambda i: (i, 0), lambda i: (i, 0), (n_tok, ATTN_WIDTH),
                           g_attn, w_in_b, gq, gk, head_mean)
    attn_p = _attn_prompt(slopes, qp, kp, vp, batch, seq)
    hp = _mix_prompt(xp, attn_p, up, w_pool_b, ps, w_out_b, seq)
    yp, hist_p = _ffn_prompt(hp, g_ffn, w_up_b, cw, cb, w_down_b, seq)

    xs = x_sample.reshape(nseq, N_NEW * D_MODEL)
    qs, ks, vs, us = _proj(xs, nseq, N_NEW, lambda n: (0, n), lambda n: (0, n), (nseq, N_NEW * ATTN_WIDTH),
                           g_attn, w_in_b, gq, gk, head_mean)
    per_seq = lambda a: a.reshape(nseq, N_NEW, ATTN_WIDTH)
    cache_kt = jnp.transpose(cache_k[l], (0, 2, 3, 1)).reshape(nseq, ATTN_WIDTH, -1)
    cache_vt = jnp.transpose(cache_v[l], (0, 2, 3, 1)).reshape(nseq, ATTN_WIDTH, -1)
    slope_rows = jnp.tile(slopes, N_NEW).reshape(QROWS, 1)
    attn_s = _attn_sample(slope_rows, per_seq(qs), per_seq(ks), per_seq(vs), cache_kt, cache_vt)
    attn_s = attn_s.reshape(nseq, N_NEW * ATTN_WIDTH)
    st_pool = jnp.transpose(state_pool[l], (1, 0, 2))
    hs = _mix_sample(xs, attn_s, us, st_pool, w_pool_b, ps, w_out_b)
    st_ffn = state_ffn_conv[l].reshape(nseq, (CONV_WIDTH - 1) * 2 * D_FF)
    ys, hist_s = _ffn_sample(hs, st_ffn, g_ffn, w_up_b, cw, cb, w_down_b)

    n_keep = min(16 * N_STRIDED, seq)
    kp5 = kp.reshape(batch, seq, N_HEADS, HEAD_DIM)[None, :, seq - n_keep:]
    vp5 = vp.reshape(batch, seq, N_HEADS, HEAD_DIM)[None, :, seq - n_keep:]
    new_pool_p = up.reshape(batch, seq, POOL_WIDTH)[None, :, seq - POOL_HIST:]
    new_ffn_p = hist_p[None, :, CONV_HEAD - (CONV_WIDTH - 1):]
    u_time = jnp.transpose(us.reshape(nseq, N_NEW, POOL_WIDTH), (1, 0, 2))
    new_pool_s = jnp.transpose(jnp.concatenate([st_pool[N_NEW:], u_time], axis=0), (1, 0, 2))[None]
    return (
        yp.reshape(batch, seq, D_MODEL),
        ys.reshape(nseq, N_NEW, D_MODEL),
        kp5, vp5, new_pool_p, new_ffn_p,
        ks.reshape(1, nseq, N_NEW, N_HEADS, HEAD_DIM),
        vs.reshape(1, nseq, N_NEW, N_HEADS, HEAD_DIM),
        new_pool_s,
        hist_s.reshape(1, nseq, CONV_WIDTH - 1, 2 * D_FF),
    )
```

```python
import functools

import numpy as np
import jax
import jax.numpy as jnp
from jax import lax
from jax.experimental import pallas as pl
from jax.experimental.pallas import tpu as pltpu

D_MODEL = 1024
HEAD_DIM = 64
N_HEADS = 8
ATTN_WIDTH = N_HEADS * HEAD_DIM
POOL_WIDTH = 512
POOL_WINDOWS = (2, 4, 8, 16)
POOL_GROUP = 128
POOL_HIST = 15
PROJ_WIDTH = 3 * ATTN_WIDTH + POOL_WIDTH
DILATIONS = (1, 4, 16)
N_STRIDED = 128
BLK = 128
D_FF = 2816
CONV_WIDTH = 3
PAST_LEN = 8192
EPS = 1e-6
NEG = -1e30
LOG2E = 1.4426950408889634

F32 = jnp.float32
BF16 = jnp.bfloat16

TM = 512
ATTN_UNITS = 4
FF_CHUNK = 256
N_FF_CHUNKS = D_FF // FF_CHUNK
POOL_HEAD = 24
CONV_HEAD = 8
VMEM_LIMIT = 56 * 1024 * 1024


def _params(sem, vmem=VMEM_LIMIT):
    return pltpu.CompilerParams(dimension_semantics=sem, vmem_limit_bytes=vmem)


def _const_spec(shape):
    nd = len(shape)
    return pl.BlockSpec(shape, lambda *_: (0,) * nd, pipeline_mode=pl.Buffered(1))


def _proj_kernel(x_ref, g_ref, w_ref, gq_ref, gk_ref, hm_ref, q_ref, k_ref, v_ref, u_ref, *kvt_refs):
    x = x_ref[...]
    xn = x * lax.rsqrt(jnp.mean(x * x, axis=-1, keepdims=True) + EPS) * g_ref[...]
    proj = jnp.dot(xn.astype(BF16), w_ref[...], preferred_element_type=F32)

    def head_norm(t, g):
        ms = jnp.dot((t * t).astype(BF16), hm_ref[...], preferred_element_type=F32)
        return t * lax.rsqrt(ms + EPS) * g

    q = head_norm(proj[:, :ATTN_WIDTH], gq_ref[...])
    q_ref[...] = q * (HEAD_DIM ** -0.5 * LOG2E)
    k = head_norm(proj[:, ATTN_WIDTH:2 * ATTN_WIDTH], gk_ref[...])
    v = proj[:, 2 * ATTN_WIDTH:3 * ATTN_WIDTH]
    k_ref[...] = k
    v_ref[...] = v
    u_ref[...] = proj[:, 3 * ATTN_WIDTH:]
    if kvt_refs:
        kvt_refs[0][0] = k.T
        kvt_refs[1][0] = v.T


def _proj(x2d, rows, n_steps, x_map, o_map, out_rows, g_attn, w_in, gq, gk, head_mean, seq=None):
    out = jax.ShapeDtypeStruct((out_rows[0], out_rows[1]), F32)
    out_specs = [pl.BlockSpec((rows, ATTN_WIDTH), o_map)] * 4
    out_shape = [out] * 4
    if seq is not None:
        tps = seq // rows
        out_specs += [pl.BlockSpec((1, ATTN_WIDTH, rows), lambda i: (i // tps, 0, i % tps))] * 2
        out_shape += [jax.ShapeDtypeStruct((out_rows[0] // seq, ATTN_WIDTH, seq), F32)] * 2
    return pl.pallas_call(
        _proj_kernel,
        grid=(n_steps,),
        in_specs=[
            pl.BlockSpec((rows, D_MODEL), x_map),
            _const_spec((1, D_MODEL)),
            _const_spec((D_MODEL, PROJ_WIDTH)),
            _const_spec((1, ATTN_WIDTH)),
            _const_spec((1, ATTN_WIDTH)),
            _const_spec((ATTN_WIDTH, ATTN_WIDTH)),
        ],
        out_specs=out_specs,
        out_shape=out_shape,
        compiler_params=_params(("arbitrary",)),
        name="proj",
    )(x2d, g_attn, w_in, gq, gk, head_mean)


def _attn_prompt_kernel(slopes_ref, q_ref, k_ref, v_ref, o_ref, bias_s, acc_s, m_s, l_s):
    hp = pl.program_id(1)
    seq = q_ref.shape[0]
    lane = lax.broadcasted_iota(jnp.int32, (BLK, 2 * HEAD_DIM), 1)
    left = lane < HEAD_DIM

    row = lax.broadcasted_iota(jnp.int32, (2 * BLK, 2 * BLK), 0)
    col = lax.broadcasted_iota(jnp.int32, (2 * BLK, 2 * BLK), 1)
    diff = (row & (BLK - 1)) + BLK - col
    valid = (diff >= 0) & (diff <= N_STRIDED)
    slope = jnp.where(row < BLK, slopes_ref[2 * hp], slopes_ref[2 * hp + 1])
    for pi, r in enumerate(DILATIONS):
        bias_s[pi] = jnp.where(valid, (-LOG2E) * slope * (r * diff).astype(F32), NEG)

    def rows_of(start, r, n=BLK):
        return pl.ds(start, n) if r == 1 else pl.ds(start, n, stride=r)

    def units(pi, r, starts, with_prev):
        curs = [rows_of(st, r) for st in starts]
        qqs, kks, vvs = [], [], []
        for st, cur in zip(starts, curs):
            q2 = q_ref[cur, :]
            qqs.append(
                jnp.concatenate([jnp.where(left, q2, 0.0), jnp.where(left, 0.0, q2)], axis=0).astype(BF16))
            if not with_prev:
                kk, vv = k_ref[cur, :], v_ref[cur, :]
            elif r == 1:
                both = pl.ds(st - BLK, 2 * BLK)
                kk, vv = k_ref[both, :], v_ref[both, :]
            else:
                prev = rows_of(st - BLK * r, r)
                kk = jnp.concatenate([k_ref[prev, :], k_ref[cur, :]], axis=0)
                vv = jnp.concatenate([v_ref[prev, :], v_ref[cur, :]], axis=0)
            kks.append(kk.astype(BF16))
            vvs.append(vv.astype(BF16))
        bias = bias_s[pi] if with_prev else bias_s[pi, :, BLK:]
        ss = [lax.dot_general(qq, kk, (((1,), (1,)), ((), ())), preferred_element_type=F32) + bias
              for qq, kk in zip(qqs, kks)]
        ms = [jnp.max(s, axis=-1, keepdims=True) for s in ss]
        ps = [jnp.exp2(s - m) for s, m in zip(ss, ms)]
        ls = [jnp.sum(p, axis=-1, keepdims=True) for p in ps]
        pvs = [jnp.dot(p.astype(BF16), vv, preferred_element_type=F32) for p, vv in zip(ps, vvs)]
        for cur, pv, m, l in zip(curs, pvs, ms, ls):
            acc_s[pi, cur, :] = jnp.where(left, pv[:BLK], pv[BLK:])
            m_s[pi, cur, :] = jnp.where(left, m[:BLK], m[BLK:])
            l_s[pi, cur, :] = jnp.where(left, l[:BLK], l[BLK:])

    U = ATTN_UNITS
    for pi, r in enumerate(DILATIONS):
        n_blocks = seq // (r * BLK)
        if r <= U:
            units(pi, r, list(range(r)), False)
        else:
            def first(i, carry, pi=pi, r=r):
                units(pi, r, [i * U + j for j in range(U)], False)
                return carry

            lax.fori_loop(0, r // U, first, 0)
        n_rest = r * (n_blocks - 1)
        if r == 1 and n_rest:
            g = 3 if n_rest % 3 == 0 else 1

            def rest1(i, carry, pi=pi, g=g):
                units(pi, 1, [pl.multiple_of((i * g + j + 1) * BLK, BLK) for j in range(g)], True)
                return carry

            lax.fori_loop(0, n_rest // g, rest1, 0)
        elif n_rest:
            g = min(r, U)

            def rest(i, carry, pi=pi, r=r, g=g):
                blk = i // (r // g) + 1
                c0 = (i % (r // g)) * g
                units(pi, r, [c0 + j + blk * (BLK * r) for j in range(g)], True)
                return carry

            lax.fori_loop(0, n_rest // g, rest, 0)

    def merge(i, carry):
        rows = pl.ds(pl.multiple_of(i * BLK, BLK), BLK)
        ms = [m_s[pi, rows, :] for pi in range(3)]
        mx = jnp.maximum(jnp.maximum(ms[0], ms[1]), ms[2])
        num = jnp.zeros((BLK, 2 * HEAD_DIM), F32)
        den = jnp.zeros((BLK, 2 * HEAD_DIM), F32)
        for pi in range(3):
            a = jnp.exp2(ms[pi] - mx)
            num = num + a * acc_s[pi, rows, :]
            den = den + a * l_s[pi, rows, :]
        o_ref[rows, :] = (num / den).astype(o_ref.dtype)
        return carry

    lax.fori_loop(0, seq // BLK, merge, 0)


def _attn_prompt(slopes, q, k, v, batch, seq):
    blk = pl.BlockSpec((seq, 2 * HEAD_DIM), lambda b, h: (b, h))
    return pl.pallas_call(
        _attn_prompt_kernel,
        grid=(batch, N_HEADS // 2),
        in_specs=[pl.BlockSpec(memory_space=pltpu.SMEM), blk, blk, blk],
        out_specs=blk,
        out_shape=jax.ShapeDtypeStruct((batch * seq, ATTN_WIDTH), BF16),
        scratch_shapes=[
            pltpu.VMEM((3, 2 * BLK, 2 * BLK), F32),
            pltpu.VMEM((3, seq, 2 * HEAD_DIM), F32),
            pltpu.VMEM((3, seq, 2 * HEAD_DIM), F32),
            pltpu.VMEM((3, seq, 2 * HEAD_DIM), F32),
        ],
        compiler_params=_params(("arbitrary", "arbitrary")),
        name="attn_prompt",
    )(slopes, q, k, v)


N_NEW = 4
QROWS = N_NEW * N_HEADS


def _attn_sample_kernel(slope_ref, q_ref, kn_ref, vn_ref, kt_ref, vt_ref, o_ref, bias_s, mult_s):
    W = ATTN_WIDTH
    wb = kt_ref.shape[2]
    slope = slope_ref[...]
    row1 = lax.broadcasted_iota(jnp.int32, (QROWS, 1), 0)
    n_of_row = row1 // N_HEADS

    @pl.when(pl.program_id(0) == 0)
    def _():
        t = lax.broadcasted_iota(jnp.int32, (QROWS, wb), 1)
        dist = wb + n_of_row - t
        mult = jnp.zeros((QROWS, wb), F32)
        for r in DILATIONS:
            hit = ((dist & (r - 1)) == 0) & (dist <= r * N_STRIDED)
            mult = mult + jnp.where(hit, 1.0, 0.0)
        mult_s[...] = mult
        bias_s[...] = jnp.where(mult > 0.0, (-LOG2E) * slope * dist.astype(F32), NEG)

    q = q_ref[0]
    kn = kn_ref[0]
    vn = vn_ref[0]
    sub = lax.broadcasted_iota(jnp.int32, (N_HEADS, W), 0)
    own = sub == lax.broadcasted_iota(jnp.int32, (N_HEADS, W), 1) // HEAD_DIM
    own_rows = jnp.concatenate([own] * N_NEW, axis=0)
    qbd = jnp.concatenate(
        [jnp.where(own, jnp.broadcast_to(q[n:n + 1], (N_HEADS, W)), 0.0) for n in range(N_NEW)], axis=0)

    s = jnp.dot(qbd.astype(BF16), kt_ref[0].astype(BF16), preferred_element_type=F32) + bias_s[...]
    s_new, w_new = [], []
    for m in range(N_NEW):
        sm = jnp.sum(qbd * kn[m:m + 1], axis=-1, keepdims=True)
        gap = n_of_row - m
        s_new.append(jnp.where(gap >= 0, sm - LOG2E * slope * gap.astype(F32), NEG))
        w_new.append(jnp.where(gap == 0, float(len(DILATIONS)), jnp.where(gap > 0, 1.0, 0.0)))
    mx = jnp.max(s, axis=-1, keepdims=True)
    for sm in s_new:
        mx = jnp.maximum(mx, sm)
    p = jnp.exp2(s - mx) * mult_s[...]
    den = jnp.sum(p, axis=-1, keepdims=True)
    pv = lax.dot_general(p.astype(BF16), vt_ref[0].astype(BF16), (((1,), (1,)), ((), ())),
                         preferred_element_type=F32)
    for m in range(N_NEW):
        pm = jnp.exp2(s_new[m] - mx) * w_new[m]
        den = den + pm
        pv = pv + pm * vn[m:m + 1]
    full = jnp.where(own_rows, pv / den, 0.0)
    o_ref[0] = jnp.sum(full.reshape(N_NEW, N_HEADS, W), axis=1).astype(o_ref.dtype)


def _attn_sample(slope_rows, q, kn, vn, cache_kt, cache_vt):
    nseq, W, wb = cache_kt.shape
    new_spec = pl.BlockSpec((1, N_NEW, W), lambda s: (s, 0, 0))
    cache_spec = pl.BlockSpec((1, W, wb), lambda s: (s, 0, 0))
    return pl.pallas_call(
        _attn_sample_kernel,
        grid=(nseq,),
        in_specs=[_const_spec((QROWS, 1)), new_spec, new_spec, new_spec, cache_spec, cache_spec],
        out_specs=new_spec,
        out_shape=jax.ShapeDtypeStruct((nseq, N_NEW, W), BF16),
        scratch_shapes=[pltpu.VMEM((QROWS, wb), F32), pltpu.VMEM((QROWS, wb), F32)],
        compiler_params=_params(("arbitrary",)),
        name="attn_sample",
    )(slope_rows, q, kn, vn, cache_kt, cache_vt)


def _pool_mix(d_groups, wp_ref, ps_ref):
    mixed = [jnp.dot(d.astype(BF16), wp_ref[g], preferred_element_type=F32) for g, d in enumerate(d_groups)]
    return jnp.concatenate(mixed, axis=-1) * ps_ref[...]


def _out_proj(x, attn, pool_out, wo_ref):
    h = x + jnp.dot(attn, wo_ref[:ATTN_WIDTH, :], preferred_element_type=F32)
    return h + jnp.dot(pool_out.astype(BF16), wo_ref[ATTN_WIDTH:, :], preferred_element_type=F32)


def _mix_prompt_kernel(x_ref, a_ref, u_ref, wp_ref, ps_ref, wo_ref, h_ref, e_s, s2_s, s4_s, s8_s,
                       *, tiles_per_seq):
    i = pl.program_id(0)
    tm = u_ref.shape[0]
    H = POOL_HEAD
    G = POOL_GROUP

    @pl.when(i == 0)
    def _():
        e_s[0:8, :] = jnp.zeros((8, POOL_WIDTH), F32)
        s2_s[0:8, :] = jnp.zeros((8, POOL_WIDTH), F32)
        s4_s[0:8, :] = jnp.zeros((8, 3 * G), F32)
        s8_s[0:8, :] = jnp.zeros((8, 2 * G), F32)

    @pl.when(i % tiles_per_seq == 0)
    def _():
        e_s[8:H, :] = jnp.zeros((H - 8, POOL_WIDTH), F32)

    u = u_ref[...]
    e_s[H:, :] = u
    n = tm + H - 8
    s2_s[8:, :] = e_s[8:, :] + e_s[7:7 + n, :]
    s4_s[8:, :] = s2_s[8:, G:] + s2_s[6:6 + n, G:]
    s8_s[8:, :] = s4_s[8:, G:] + s4_s[4:4 + n, G:]
    sums = [
        s2_s[H:, :G],
        s4_s[H:, :G],
        s8_s[H:, :G],
        s8_s[H:, G:] + s8_s[H - 8:H - 8 + tm, G:],
    ]
    pos = (i % tiles_per_seq) * tm + lax.broadcasted_iota(jnp.int32, (tm, 1), 0)
    d = []
    for g, w in enumerate(POOL_WINDOWS):
        cnt = jnp.minimum(w, pos + 1).astype(F32)
        d.append(sums[g] / cnt - u[:, g * G:(g + 1) * G])
    pool_out = _pool_mix(d, wp_ref, ps_ref)
    h_ref[...] = _out_proj(x_ref[...], a_ref[...], pool_out, wo_ref)
    e_s[8:H, :] = e_s[tm + 8:tm + H, :]


def _mix_prompt(x2d, attn, u, w_pool, pool_scale, w_out, seq):
    n_tok = x2d.shape[0]
    G = POOL_GROUP
    rows = TM + POOL_HEAD
    return pl.pallas_call(
        functools.partial(_mix_prompt_kernel, tiles_per_seq=seq // TM),
        grid=(n_tok // TM,),
        in_specs=[
            pl.BlockSpec((TM, D_MODEL), lambda i: (i, 0)),
            pl.BlockSpec((TM, ATTN_WIDTH), lambda i: (i, 0)),
            pl.BlockSpec((TM, POOL_WIDTH), lambda i: (i, 0)),
            _const_spec((len(POOL_WINDOWS), G, G)),
            _const_spec((1, POOL_WIDTH)),
            _const_spec((D_MODEL, D_MODEL)),
        ],
        out_specs=pl.BlockSpec((TM, D_MODEL), lambda i: (i, 0)),
        out_shape=jax.ShapeDtypeStruct((n_tok, D_MODEL), F32),
        scratch_shapes=[
            pltpu.VMEM((rows, POOL_WIDTH), F32),
            pltpu.VMEM((rows, POOL_WIDTH), F32),
            pltpu.VMEM((rows, 3 * G), F32),
            pltpu.VMEM((rows, 2 * G), F32),
        ],
        compiler_params=_params(("arbitrary",)),
        name="mix_prompt",
    )(x2d, attn, u, w_pool, pool_scale, w_out)


def _mix_sample_kernel(x_ref, a_ref, u_ref, st_ref, wp_ref, ps_ref, wo_ref, h_ref):
    G = POOL_GROUP
    PW = POOL_WIDTH

    def row(t, lanes):
        if t < POOL_HIST:
            return st_ref[t, :, lanes]
        return u_ref[:, (t - POOL_HIST) * PW + lanes.start:(t - POOL_HIST) * PW + lanes.stop]

    for n in range(N_NEW):
        t = POOL_HIST + n
        d = []
        for g, w in enumerate(POOL_WINDOWS):
            lanes = slice(g * G, (g + 1) * G)
            cur = row(t, lanes)
            tot = cur
            for j in range(1, w):
                tot = tot + row(t - j, lanes)
            d.append(tot / float(min(w, PAST_LEN + n + 1)) - cur)
        pool_out = _pool_mix(d, wp_ref, ps_ref)
        cols = slice(n * D_MODEL, (n + 1) * D_MODEL)
        attn = a_ref[:, n * ATTN_WIDTH:(n + 1) * ATTN_WIDTH]
        h_ref[:, cols] = _out_proj(x_ref[:, cols], attn, pool_out, wo_ref)


def _mix_sample(x, attn, u, state_pool, w_pool, pool_scale, w_out):
    nseq = x.shape[0]
    full = lambda a: pl.BlockSpec(a.shape, lambda i: (0,) * a.ndim)
    args = (x, attn, u, state_pool, w_pool, pool_scale, w_out)
    return pl.pallas_call(
        _mix_sample_kernel,
        grid=(1,),
        in_specs=[full(a) for a in args],
        out_specs=pl.BlockSpec((nseq, N_NEW * D_MODEL), lambda i: (0, 0)),
        out_shape=jax.ShapeDtypeStruct((nseq, N_NEW * D_MODEL), F32),
        compiler_params=_params(("arbitrary",)),
        name="mix_sample",
    )(*args)


def _rms(h, g_ref):
    return (h * lax.rsqrt(jnp.mean(h * h, axis=-1, keepdims=True) + EPS) * g_ref[...]).astype(BF16)


def _silu_gate(gate, val):
    return (gate / (1.0 + jnp.exp(-gate)) * val).astype(BF16)


def _ffn_prompt_kernel(h_ref, g_ref, wu_ref, cw_ref, cb_ref, wd_ref, y_ref, hist_ref, ext_s, carry_s, act_s,
                       *, tiles_per_seq):
    i = pl.program_id(0)
    tm = h_ref.shape[0]
    C = FF_CHUNK
    HD = CONV_HEAD

    @pl.when(i % tiles_per_seq == 0)
    def _():
        carry_s[...] = jnp.zeros(carry_s.shape, F32)

    hn = _rms(h_ref[...], g_ref)
    for c in range(N_FF_CHUNKS):
        halves = []
        for half in range(2):
            cols = slice(half * D_FF + c * C, half * D_FF + (c + 1) * C)
            up = jnp.dot(hn, wu_ref[:, cols], preferred_element_type=F32)
            ext = ext_s.at[half]
            ext[0:HD, :] = carry_s[:, cols]
            ext[HD:, :] = up
            conv = cb_ref[:, cols] + sum(
                ext[HD - (CONV_WIDTH - 1) + j:HD - (CONV_WIDTH - 1) + j + tm, :] * cw_ref[j:j + 1, cols]
                for j in range(CONV_WIDTH))
            carry_s[:, cols] = ext[tm:tm + HD, :]
            halves.append(conv)
        act_s[:, c * C:(c + 1) * C] = _silu_gate(halves[0], halves[1])
    y_ref[...] = h_ref[...] + jnp.dot(act_s[...], wd_ref[...], preferred_element_type=F32)
    hist_ref[0] = carry_s[...]


def _ffn_prompt(h, g_ffn, w_up, conv_w, conv_b, w_down, seq):
    n_tok = h.shape[0]
    tps = seq // TM
    return pl.pallas_call(
        functools.partial(_ffn_prompt_kernel, tiles_per_seq=tps),
        grid=(n_tok // TM,),
        in_specs=[
            pl.BlockSpec((TM, D_MODEL), lambda i: (i, 0)),
            _const_spec((1, D_MODEL)),
            _const_spec((D_MODEL, 2 * D_FF)),
            _const_spec((CONV_WIDTH, 2 * D_FF)),
            _const_spec((1, 2 * D_FF)),
            _const_spec((D_FF, D_MODEL)),
        ],
        out_specs=[
            pl.BlockSpec((TM, D_MODEL), lambda i: (i, 0)),
            pl.BlockSpec((1, CONV_HEAD, 2 * D_FF), lambda i: (i // tps, 0, 0)),
        ],
        out_shape=[
            jax.ShapeDtypeStruct((n_tok, D_MODEL), F32),
            jax.ShapeDtypeStruct((n_tok // seq, CONV_HEAD, 2 * D_FF), F32),
        ],
        scratch_shapes=[
            pltpu.VMEM((2, TM + CONV_HEAD, FF_CHUNK), F32),
            pltpu.VMEM((CONV_HEAD, 2 * D_FF), F32),
            pltpu.VMEM((TM, D_FF), BF16),
        ],
        compiler_params=_params(("arbitrary",)),
        name="ffn_prompt",
    )(h, g_ffn, w_up, conv_w, conv_b, w_down)


def _ffn_sample_kernel(h_ref, st_ref, g_ref, wu_ref, cw_ref, cb_ref, wd_ref, y_ref, hist_ref):
    nseq = h_ref.shape[0]
    C = FF_CHUNK
    K = CONV_WIDTH
    hs = [h_ref[:, n * D_MODEL:(n + 1) * D_MODEL] for n in range(N_NEW)]
    hn = jnp.concatenate([_rms(h, g_ref) for h in hs], axis=0)
    for n in range(N_NEW):
        y_ref[:, n * D_MODEL:(n + 1) * D_MODEL] = hs[n]
    for c in range(N_FF_CHUNKS):
        halves = []
        for half in range(2):
            cols = slice(half * D_FF + c * C, half * D_FF + (c + 1) * C)
            up = jnp.dot(hn, wu_ref[:, cols], preferred_element_type=F32)
            rows = [st_ref[:, t * 2 * D_FF + cols.start:t * 2 * D_FF + cols.stop] for t in range(K - 1)]
            rows += [up[n * nseq:(n + 1) * nseq] for n in range(N_NEW)]
            conv = [cb_ref[:, cols] + sum(rows[n + j] * cw_ref[j:j + 1, cols] for j in range(K))
                    for n in range(N_NEW)]
            for t in range(K - 1):
                hist_ref[:, t * 2 * D_FF + cols.start:t * 2 * D_FF + cols.stop] = rows[N_NEW + t]
            halves.append(jnp.concatenate(conv, axis=0))
        act = _silu_gate(halves[0], halves[1])
        out = jnp.dot(act, wd_ref[c * C:(c + 1) * C, :], preferred_element_type=F32)
        for n in range(N_NEW):
            y_ref[:, n * D_MODEL:(n + 1) * D_MODEL] += out[n * nseq:(n + 1) * nseq]


def _ffn_sample(h, state_ffn, g_ffn, w_up, conv_w, conv_b, w_down):
    nseq = h.shape[0]
    full = lambda a: pl.BlockSpec(a.shape, lambda i: (0,) * a.ndim, pipeline_mode=pl.Buffered(1))
    args = (h, state_ffn, g_ffn, w_up, conv_w, conv_b, w_down)
    return pl.pallas_call(
        _ffn_sample_kernel,
        grid=(1,),
        in_specs=[full(a) for a in args],
        out_specs=[
            pl.BlockSpec((nseq, N_NEW * D_MODEL), lambda i: (0, 0)),
            pl.BlockSpec((nseq, (CONV_WIDTH - 1) * 2 * D_FF), lambda i: (0, 0)),
        ],
        out_shape=[
            jax.ShapeDtypeStruct((nseq, N_NEW * D_MODEL), F32),
            jax.ShapeDtypeStruct((nseq, (CONV_WIDTH - 1) * 2 * D_FF), F32),
        ],
        compiler_params=_params(("arbitrary",)),
        name="ffn_sample",
    )(*args)


def kernel(x_prompt, x_sample, cache_k, cache_v, state_pool, state_ffn_conv, g_attn_norm, w_in, g_q, g_k,
           w_pool, pool_scale, w_out, g_ffn_norm, w_up, conv_w, conv_b, w_down):
    depth = w_in.shape[0]
    assert depth == 1
    batch, seq, _ = x_prompt.shape
    nseq, n_new, _ = x_sample.shape
    assert n_new == N_NEW and seq % (16 * BLK) == 0 and seq % TM == 0
    assert cache_k.shape[2] == 16 * N_STRIDED

    slopes = jnp.asarray(2.0 ** (-8.0 * np.arange(1, N_HEADS + 1) / N_HEADS), dtype=F32)
    head_of_lane = np.arange(ATTN_WIDTH) // HEAD_DIM
    head_mean = jnp.asarray((head_of_lane[:, None] == head_of_lane[None, :]) / HEAD_DIM, dtype=BF16)

    l = 0
    w_in_b = w_in[l].astype(BF16)
    w_pool_b = w_pool[l].astype(BF16)
    w_out_b = w_out[l].astype(BF16)
    w_up_b = w_up[l].astype(BF16)
    w_down_b = w_down[l].astype(BF16)
    g_attn = g_attn_norm[l].reshape(1, D_MODEL)
    g_ffn = g_ffn_norm[l].reshape(1, D_MODEL)
    gq = g_q[l].reshape(1, ATTN_WIDTH)
    gk = g_k[l].reshape(1, ATTN_WIDTH)
    ps = pool_scale[l].reshape(1, POOL_WIDTH)
    cb = conv_b[l].reshape(1, 2 * D_FF)
    cw = conv_w[l]

    n_tok = batch * seq
    xp = x_prompt.reshape(n_tok, D_MODEL)
    qp, kp, vp, up, kpt, vpt = _proj(xp, TM, n_tok // TM, lambda i: (i, 0), lambda i: (i, 0),
                                     (n_tok, ATTN_WIDTH), g_attn, w_in_b, gq, gk, head_mean, seq=seq)
    attn_p = _attn_prompt(slopes, qp, kp, vp, batch, seq)
    hp = _mix_prompt(xp, attn_p, up, w_pool_b, ps, w_out_b, seq)
    yp, hist_p = _ffn_prompt(hp, g_ffn, w_up_b, cw, cb, w_down_b, seq)

    xs = x_sample.reshape(nseq, N_NEW * D_MODEL)
    qs, ks, vs, us = _proj(xs, nseq, N_NEW, lambda n: (0, n), lambda n: (0, n), (nseq, N_NEW * ATTN_WIDTH),
                           g_attn, w_in_b, gq, gk, head_mean)
    per_seq = lambda a: a.reshape(nseq, N_NEW, ATTN_WIDTH)
    cache_kt = jnp.transpose(cache_k[l], (0, 2, 3, 1)).reshape(nseq, ATTN_WIDTH, -1)
    cache_vt = jnp.transpose(cache_v[l], (0, 2, 3, 1)).reshape(nseq, ATTN_WIDTH, -1)
    slope_rows = jnp.tile(slopes, N_NEW).reshape(QROWS, 1)
    attn_s = _attn_sample(slope_rows, per_seq(qs), per_seq(ks), per_seq(vs), cache_kt, cache_vt)
    attn_s = attn_s.reshape(nseq, N_NEW * ATTN_WIDTH)
    st_pool = jnp.transpose(state_pool[l], (1, 0, 2))
    hs = _mix_sample(xs, attn_s, us, st_pool, w_pool_b, ps, w_out_b)
    st_ffn = state_ffn_conv[l].reshape(nseq, (CONV_WIDTH - 1) * 2 * D_FF)
    ys, hist_s = _ffn_sample(hs, st_ffn, g_ffn, w_up_b, cw, cb, w_down_b)

    n_keep = min(16 * N_STRIDED, seq)
    window = lambda t: jnp.transpose(
        t.reshape(batch, N_HEADS, HEAD_DIM, seq), (0, 3, 1, 2))[None, :, seq - n_keep:]
    kp5, vp5 = window(kpt), window(vpt)
    new_pool_p = up.reshape(batch, seq, POOL_WIDTH)[None, :, seq - POOL_HIST:]
    new_ffn_p = hist_p[None, :, CONV_HEAD - (CONV_WIDTH - 1):]
    u_time = jnp.transpose(us.reshape(nseq, N_NEW, POOL_WIDTH), (1, 0, 2))
    new_pool_s = jnp.transpose(jnp.concatenate([st_pool[N_NEW:], u_time], axis=0), (1, 0, 2))[None]
    return (
        yp.reshape(batch, seq, D_MODEL),
        ys.reshape(nseq, N_NEW, D_MODEL),
        kp5, vp5, new_pool_p, new_ffn_p,
        ks.reshape(1, nseq, N_NEW, N_HEADS, HEAD_DIM),
        vs.reshape(1, nseq, N_NEW, N_HEADS, HEAD_DIM),
        new_pool_s,
        hist_s.reshape(1, nseq, CONV_WIDTH - 1, 2 * D_FF),
    )
```

```python
import functools

import numpy as np
import jax
import jax.numpy as jnp
from jax import lax
from jax.experimental import pallas as pl
from jax.experimental.pallas import tpu as pltpu

D_MODEL = 1024
HEAD_DIM = 64
N_HEADS = 8
ATTN_WIDTH = N_HEADS * HEAD_DIM
POOL_WIDTH = 512
POOL_WINDOWS = (2, 4, 8, 16)
POOL_GROUP = 128
POOL_HIST = 15
PROJ_WIDTH = 3 * ATTN_WIDTH + POOL_WIDTH
DILATIONS = (1, 4, 16)
N_STRIDED = 128
BLK = 128
D_FF = 2816
CONV_WIDTH = 3
PAST_LEN = 8192
EPS = 1e-6
NEG = -1e30
LOG2E = 1.4426950408889634

F32 = jnp.float32
BF16 = jnp.bfloat16

TM = 512
ATTN_UNITS = 8
FF_CHUNK = 256
N_FF_CHUNKS = D_FF // FF_CHUNK
POOL_HEAD = 24
CONV_HEAD = 8
VMEM_LIMIT = 56 * 1024 * 1024


def _params(sem, vmem=VMEM_LIMIT):
    return pltpu.CompilerParams(dimension_semantics=sem, vmem_limit_bytes=vmem)


def _const_spec(shape):
    nd = len(shape)
    return pl.BlockSpec(shape, lambda *_: (0,) * nd, pipeline_mode=pl.Buffered(1))


def _proj_kernel(x_ref, g_ref, w_ref, gq_ref, gk_ref, hm_ref, q_ref, k_ref, v_ref, u_ref, *kvt_refs):
    x = x_ref[...]
    xn = x * lax.rsqrt(jnp.mean(x * x, axis=-1, keepdims=True) + EPS) * g_ref[...]
    proj = jnp.dot(xn.astype(BF16), w_ref[...], preferred_element_type=F32)

    def head_norm(t, g):
        ms = jnp.dot((t * t).astype(BF16), hm_ref[...], preferred_element_type=F32)
        return t * lax.rsqrt(ms + EPS) * g

    q = head_norm(proj[:, :ATTN_WIDTH], gq_ref[...])
    q_ref[...] = q * (HEAD_DIM ** -0.5 * LOG2E)
    k = head_norm(proj[:, ATTN_WIDTH:2 * ATTN_WIDTH], gk_ref[...])
    v = proj[:, 2 * ATTN_WIDTH:3 * ATTN_WIDTH]
    k_ref[...] = k
    v_ref[...] = v
    u_ref[...] = proj[:, 3 * ATTN_WIDTH:]
    if kvt_refs:
        kvt_refs[0][0] = k.T
        kvt_refs[1][0] = v.T


def _proj(x2d, rows, n_steps, x_map, o_map, out_rows, g_attn, w_in, gq, gk, head_mean, seq=None):
    out = jax.ShapeDtypeStruct((out_rows[0], out_rows[1]), F32)
    out_specs = [pl.BlockSpec((rows, ATTN_WIDTH), o_map)] * 4
    out_shape = [out] * 4
    if seq is not None:
        tps = seq // rows
        out_specs += [pl.BlockSpec((1, ATTN_WIDTH, rows), lambda i: (i // tps, 0, i % tps))] * 2
        out_shape += [jax.ShapeDtypeStruct((out_rows[0] // seq, ATTN_WIDTH, seq), F32)] * 2
    return pl.pallas_call(
        _proj_kernel,
        grid=(n_steps,),
        in_specs=[
            pl.BlockSpec((rows, D_MODEL), x_map),
            _const_spec((1, D_MODEL)),
            _const_spec((D_MODEL, PROJ_WIDTH)),
            _const_spec((1, ATTN_WIDTH)),
            _const_spec((1, ATTN_WIDTH)),
            _const_spec((ATTN_WIDTH, ATTN_WIDTH)),
        ],
        out_specs=out_specs,
        out_shape=out_shape,
        compiler_params=_params(("arbitrary",)),
        name="proj",
    )(x2d, g_attn, w_in, gq, gk, head_mean)


def _attn_prompt_kernel(slopes_ref, q_ref, k_ref, v_ref, o_ref, bias_s, acc_s, m_s, l_s):
    hp = pl.program_id(1)
    seq = q_ref.shape[0]
    lane = lax.broadcasted_iota(jnp.int32, (BLK, 2 * HEAD_DIM), 1)
    left = lane < HEAD_DIM

    row = lax.broadcasted_iota(jnp.int32, (2 * BLK, 2 * BLK), 0)
    col = lax.broadcasted_iota(jnp.int32, (2 * BLK, 2 * BLK), 1)
    diff = (row & (BLK - 1)) + BLK - col
    valid = (diff >= 0) & (diff <= N_STRIDED)
    slope = jnp.where(row < BLK, slopes_ref[2 * hp], slopes_ref[2 * hp + 1])
    for pi, r in enumerate(DILATIONS):
        bias_s[pi] = jnp.where(valid, (-LOG2E) * slope * (r * diff).astype(F32), NEG)

    def rows_of(start, r, n=BLK):
        return pl.ds(start, n) if r == 1 else pl.ds(start, n, stride=r)

    ones_cols = jnp.ones((2 * BLK, 2 * HEAD_DIM), BF16)

    def units(specs, with_prev):
        curs = [rows_of(st, r) for _, r, st in specs]
        qqs, kks, vvs = [], [], []
        for (_, r, st), cur in zip(specs, curs):
            q2 = q_ref[cur, :]
            qqs.append(
                jnp.concatenate([jnp.where(left, q2, 0.0), jnp.where(left, 0.0, q2)], axis=0).astype(BF16))
            if not with_prev:
                kk, vv = k_ref[cur, :], v_ref[cur, :]
            elif r == 1:
                both = pl.ds(st - BLK, 2 * BLK)
                kk, vv = k_ref[both, :], v_ref[both, :]
            else:
                prev = rows_of(st - BLK * r, r)
                kk = jnp.concatenate([k_ref[prev, :], k_ref[cur, :]], axis=0)
                vv = jnp.concatenate([v_ref[prev, :], v_ref[cur, :]], axis=0)
            kks.append(kk.astype(BF16))
            vvs.append(jnp.concatenate([vv.astype(BF16), ones_cols[:vv.shape[0]]], axis=1))
        ss = [lax.dot_general(qq, kk, (((1,), (1,)), ((), ())), preferred_element_type=F32)
              + (bias_s[pi] if with_prev else bias_s[pi, :, BLK:])
              for (pi, _, _), qq, kk in zip(specs, qqs, kks)]
        ms = [jnp.max(s, axis=-1, keepdims=True) for s in ss]
        ps = [jnp.exp2(s - m).astype(BF16) for s, m in zip(ss, ms)]
        pvs = [jnp.dot(p, vv, preferred_element_type=F32) for p, vv in zip(ps, vvs)]
        W2 = 2 * HEAD_DIM
        for (pi, _, _), cur, pv, m in zip(specs, curs, pvs, ms):
            acc_s[pi, cur, :] = jnp.where(left, pv[:BLK, :W2], pv[BLK:, :W2])
            m_s[pi, cur, :] = jnp.where(left, m[:BLK], m[BLK:])
            l_s[pi, cur, :] = jnp.where(left, pv[:BLK, W2:], pv[BLK:, W2:])

    U = ATTN_UNITS
    small = [(pi, r, c) for pi, r in enumerate(DILATIONS) if r <= U for c in range(r)]
    for i0 in range(0, len(small), U):
        units(small[i0:i0 + U], False)
    for pi, r in enumerate(DILATIONS):
        if r > U:
            def first(i, carry, pi=pi, r=r):
                units([(pi, r, i * U + j) for j in range(U)], False)
                return carry

            lax.fori_loop(0, r // U, first, 0)
    for pi, r in enumerate(DILATIONS):
        n_rest = r * (seq // (r * BLK) - 1)
        if r == 1 and n_rest:
            g = 5 if n_rest % 5 == 0 else 1

            def rest1(i, carry, pi=pi, g=g):
                units([(pi, 1, pl.multiple_of((i * g + j + 1) * BLK, BLK)) for j in range(g)], True)
                return carry

            lax.fori_loop(0, n_rest // g, rest1, 0)
        elif n_rest:
            g = min(r, U)

            def rest(i, carry, pi=pi, r=r, g=g):
                blk = i // (r // g) + 1
                c0 = (i % (r // g)) * g
                units([(pi, r, c0 + j + blk * (BLK * r)) for j in range(g)], True)
                return carry

            lax.fori_loop(0, n_rest // g, rest, 0)

    def merge(i, carry):
        rows = pl.ds(pl.multiple_of(i * BLK, BLK), BLK)
        ms = [m_s[pi, rows, :] for pi in range(3)]
        mx = jnp.maximum(jnp.maximum(ms[0], ms[1]), ms[2])
        num = jnp.zeros((BLK, 2 * HEAD_DIM), F32)
        den = jnp.zeros((BLK, 2 * HEAD_DIM), F32)
        for pi in range(3):
            a = jnp.exp2(ms[pi] - mx)
            num = num + a * acc_s[pi, rows, :]
            den = den + a * l_s[pi, rows, :]
        o_ref[rows, :] = (num / den).astype(o_ref.dtype)
        return carry

    lax.fori_loop(0, seq // BLK, merge, 0)


def _attn_prompt(slopes, q, k, v, batch, seq):
    blk = pl.BlockSpec((seq, 2 * HEAD_DIM), lambda b, h: (b, h))
    return pl.pallas_call(
        _attn_prompt_kernel,
        grid=(batch, N_HEADS // 2),
        in_specs=[pl.BlockSpec(memory_space=pltpu.SMEM), blk, blk, blk],
        out_specs=blk,
        out_shape=jax.ShapeDtypeStruct((batch * seq, ATTN_WIDTH), BF16),
        scratch_shapes=[
            pltpu.VMEM((3, 2 * BLK, 2 * BLK), F32),
            pltpu.VMEM((3, seq, 2 * HEAD_DIM), F32),
            pltpu.VMEM((3, seq, 2 * HEAD_DIM), F32),
            pltpu.VMEM((3, seq, 2 * HEAD_DIM), F32),
        ],
        compiler_params=_params(("arbitrary", "arbitrary")),
        name="attn_prompt",
    )(slopes, q, k, v)


N_NEW = 4
QROWS = N_NEW * N_HEADS


def _attn_sample_kernel(slope_ref, q_ref, kn_ref, vn_ref, kt_ref, vt_ref, o_ref, bias_s, mult_s):
    W = ATTN_WIDTH
    wb = kt_ref.shape[2]
    slope = slope_ref[...]
    row1 = lax.broadcasted_iota(jnp.int32, (QROWS, 1), 0)
    n_of_row = row1 // N_HEADS

    @pl.when(pl.program_id(0) == 0)
    def _():
        t = lax.broadcasted_iota(jnp.int32, (QROWS, wb), 1)
        dist = wb + n_of_row - t
        mult = jnp.zeros((QROWS, wb), F32)
        for r in DILATIONS:
            hit = ((dist & (r - 1)) == 0) & (dist <= r * N_STRIDED)
            mult = mult + jnp.where(hit, 1.0, 0.0)
        mult_s[...] = mult
        bias_s[...] = jnp.where(mult > 0.0, (-LOG2E) * slope * dist.astype(F32), NEG)

    q = q_ref[0]
    kn = kn_ref[0]
    vn = vn_ref[0]
    sub = lax.broadcasted_iota(jnp.int32, (N_HEADS, W), 0)
    own = sub == lax.broadcasted_iota(jnp.int32, (N_HEADS, W), 1) // HEAD_DIM
    own_rows = jnp.concatenate([own] * N_NEW, axis=0)
    qbd = jnp.concatenate(
        [jnp.where(own, jnp.broadcast_to(q[n:n + 1], (N_HEADS, W)), 0.0) for n in range(N_NEW)], axis=0)

    s = jnp.dot(qbd.astype(BF16), kt_ref[0].astype(BF16), preferred_element_type=F32) + bias_s[...]
    s_new, w_new = [], []
    for m in range(N_NEW):
        sm = jnp.sum(qbd * kn[m:m + 1], axis=-1, keepdims=True)
        gap = n_of_row - m
        s_new.append(jnp.where(gap >= 0, sm - LOG2E * slope * gap.astype(F32), NEG))
        w_new.append(jnp.where(gap == 0, float(len(DILATIONS)), jnp.where(gap > 0, 1.0, 0.0)))
    mx = jnp.max(s, axis=-1, keepdims=True)
    for sm in s_new:
        mx = jnp.maximum(mx, sm)
    p = jnp.exp2(s - mx) * mult_s[...]
    den = jnp.sum(p, axis=-1, keepdims=True)
    pv = lax.dot_general(p.astype(BF16), vt_ref[0].astype(BF16), (((1,), (1,)), ((), ())),
                         preferred_element_type=F32)
    for m in range(N_NEW):
        pm = jnp.exp2(s_new[m] - mx) * w_new[m]
        den = den + pm
        pv = pv + pm * vn[m:m + 1]
    full = jnp.where(own_rows, pv / den, 0.0)
    o_ref[0] = jnp.sum(full.reshape(N_NEW, N_HEADS, W), axis=1).astype(o_ref.dtype)


def _attn_sample(slope_rows, q, kn, vn, cache_kt, cache_vt):
    nseq, W, wb = cache_kt.shape
    new_spec = pl.BlockSpec((1, N_NEW, W), lambda s: (s, 0, 0))
    cache_spec = pl.BlockSpec((1, W, wb), lambda s: (s, 0, 0))
    return pl.pallas_call(
        _attn_sample_kernel,
        grid=(nseq,),
        in_specs=[_const_spec((QROWS, 1)), new_spec, new_spec, new_spec, cache_spec, cache_spec],
        out_specs=new_spec,
        out_shape=jax.ShapeDtypeStruct((nseq, N_NEW, W), BF16),
        scratch_shapes=[pltpu.VMEM((QROWS, wb), F32), pltpu.VMEM((QROWS, wb), F32)],
        compiler_params=_params(("arbitrary",)),
        name="attn_sample",
    )(slope_rows, q, kn, vn, cache_kt, cache_vt)


def _pool_mix(d_groups, wp_ref, ps_ref):
    mixed = [jnp.dot(d.astype(BF16), wp_ref[g], preferred_element_type=F32) for g, d in enumerate(d_groups)]
    return jnp.concatenate(mixed, axis=-1) * ps_ref[...]


def _out_proj(x, attn, pool_out, wo_ref):
    h = x + jnp.dot(attn, wo_ref[:ATTN_WIDTH, :], preferred_element_type=F32)
    return h + jnp.dot(pool_out.astype(BF16), wo_ref[ATTN_WIDTH:, :], preferred_element_type=F32)


def _mix_prompt_tile(x_ref, a_ref, u_ref, wp_ref, ps_ref, wo_ref, e_s, s2_s, s4_s, s8_s, tiles_per_seq):
    i = pl.program_id(0)
    tm = u_ref.shape[0]
    H = POOL_HEAD
    G = POOL_GROUP

    @pl.when(i == 0)
    def _():
        e_s[0:8, :] = jnp.zeros((8, POOL_WIDTH), F32)
        s2_s[0:8, :] = jnp.zeros((8, POOL_WIDTH), F32)
        s4_s[0:8, :] = jnp.zeros((8, 3 * G), F32)
        s8_s[0:8, :] = jnp.zeros((8, 2 * G), F32)

    @pl.when(i % tiles_per_seq == 0)
    def _():
        e_s[8:H, :] = jnp.zeros((H - 8, POOL_WIDTH), F32)

    u = u_ref[...]
    e_s[H:, :] = u
    n = tm + H - 8
    s2_s[8:, :] = e_s[8:, :] + e_s[7:7 + n, :]
    s4_s[8:, :] = s2_s[8:, G:] + s2_s[6:6 + n, G:]
    s8_s[8:, :] = s4_s[8:, G:] + s4_s[4:4 + n, G:]
    sums = [
        s2_s[H:, :G],
        s4_s[H:, :G],
        s8_s[H:, :G],
        s8_s[H:, G:] + s8_s[H - 8:H - 8 + tm, G:],
    ]
    pos = (i % tiles_per_seq) * tm + lax.broadcasted_iota(jnp.int32, (tm, 1), 0)
    d = []
    for g, w in enumerate(POOL_WINDOWS):
        cnt = jnp.minimum(w, pos + 1).astype(F32)
        d.append(sums[g] / cnt - u[:, g * G:(g + 1) * G])
    pool_out = _pool_mix(d, wp_ref, ps_ref)
    h = _out_proj(x_ref[...], a_ref[...], pool_out, wo_ref)
    e_s[8:H, :] = e_s[tm + 8:tm + H, :]
    return h


def _mix_sample_kernel(x_ref, a_ref, u_ref, st_ref, wp_ref, ps_ref, wo_ref, h_ref):
    G = POOL_GROUP
    PW = POOL_WIDTH

    def row(t, lanes):
        if t < POOL_HIST:
            return st_ref[t, :, lanes]
        return u_ref[:, (t - POOL_HIST) * PW + lanes.start:(t - POOL_HIST) * PW + lanes.stop]

    for n in range(N_NEW):
        t = POOL_HIST + n
        d = []
        for g, w in enumerate(POOL_WINDOWS):
            lanes = slice(g * G, (g + 1) * G)
            cur = row(t, lanes)
            tot = cur
            for j in range(1, w):
                tot = tot + row(t - j, lanes)
            d.append(tot / float(min(w, PAST_LEN + n + 1)) - cur)
        pool_out = _pool_mix(d, wp_ref, ps_ref)
        cols = slice(n * D_MODEL, (n + 1) * D_MODEL)
        attn = a_ref[:, n * ATTN_WIDTH:(n + 1) * ATTN_WIDTH]
        h_ref[:, cols] = _out_proj(x_ref[:, cols], attn, pool_out, wo_ref)


def _mix_sample(x, attn, u, state_pool, w_pool, pool_scale, w_out):
    nseq = x.shape[0]
    full = lambda a: pl.BlockSpec(a.shape, lambda i: (0,) * a.ndim)
    args = (x, attn, u, state_pool, w_pool, pool_scale, w_out)
    return pl.pallas_call(
        _mix_sample_kernel,
        grid=(1,),
        in_specs=[full(a) for a in args],
        out_specs=pl.BlockSpec((nseq, N_NEW * D_MODEL), lambda i: (0, 0)),
        out_shape=jax.ShapeDtypeStruct((nseq, N_NEW * D_MODEL), F32),
        compiler_params=_params(("arbitrary",)),
        name="mix_sample",
    )(*args)


def _rms(h, g_ref):
    return (h * lax.rsqrt(jnp.mean(h * h, axis=-1, keepdims=True) + EPS) * g_ref[...]).astype(BF16)


def _silu_gate(gate, val):
    return (gate / (1.0 + jnp.exp(-gate)) * val).astype(BF16)


def _tail_prompt_kernel(x_ref, a_ref, u_ref, wp_ref, ps_ref, wo_ref, g_ref, wu_ref, cw_ref, cb_ref, wd_ref,
                        y_ref, hist_ref, e_s, s2_s, s4_s, s8_s, h_s, ext_s, carry_s, act_s, *, tiles_per_seq):
    i = pl.program_id(0)
    tm = x_ref.shape[0]
    C = FF_CHUNK
    HD = CONV_HEAD

    @pl.when(i % tiles_per_seq == 0)
    def _():
        carry_s[...] = jnp.zeros(carry_s.shape, F32)

    h_s[...] = _mix_prompt_tile(x_ref, a_ref, u_ref, wp_ref, ps_ref, wo_ref, e_s, s2_s, s4_s, s8_s,
                                tiles_per_seq)
    hn = _rms(h_s[...], g_ref)
    for c in range(N_FF_CHUNKS):
        halves = []
        for half in range(2):
            cols = slice(half * D_FF + c * C, half * D_FF + (c + 1) * C)
            up = jnp.dot(hn, wu_ref[:, cols], preferred_element_type=F32)
            ext = ext_s.at[half]
            ext[0:HD, :] = carry_s[:, cols]
            ext[HD:, :] = up
            conv = cb_ref[:, cols] + sum(
                ext[HD - (CONV_WIDTH - 1) + j:HD - (CONV_WIDTH - 1) + j + tm, :] * cw_ref[j:j + 1, cols]
                for j in range(CONV_WIDTH))
            carry_s[:, cols] = ext[tm:tm + HD, :]
            halves.append(conv)
        act_s[:, c * C:(c + 1) * C] = _silu_gate(halves[0], halves[1])
    y_ref[...] = h_s[...] + jnp.dot(act_s[...], wd_ref[...], preferred_element_type=F32)
    hist_ref[0] = carry_s[...]


def _tail_prompt(x2d, attn, u, w_pool, pool_scale, w_out, g_ffn, w_up, conv_w, conv_b, w_down, seq):
    n_tok = x2d.shape[0]
    tps = seq // TM
    G = POOL_GROUP
    pool_rows = TM + POOL_HEAD
    return pl.pallas_call(
        functools.partial(_tail_prompt_kernel, tiles_per_seq=tps),
        grid=(n_tok // TM,),
        in_specs=[
            pl.BlockSpec((TM, D_MODEL), lambda i: (i, 0)),
            pl.BlockSpec((TM, ATTN_WIDTH), lambda i: (i, 0)),
            pl.BlockSpec((TM, POOL_WIDTH), lambda i: (i, 0)),
            _const_spec((len(POOL_WINDOWS), G, G)),
            _const_spec((1, POOL_WIDTH)),
            _const_spec((D_MODEL, D_MODEL)),
            _const_spec((1, D_MODEL)),
            _const_spec((D_MODEL, 2 * D_FF)),
            _const_spec((CONV_WIDTH, 2 * D_FF)),
            _const_spec((1, 2 * D_FF)),
            _const_spec((D_FF, D_MODEL)),
        ],
        out_specs=[
            pl.BlockSpec((TM, D_MODEL), lambda i: (i, 0)),
            pl.BlockSpec((1, CONV_HEAD, 2 * D_FF), lambda i: (i // tps, 0, 0)),
        ],
        out_shape=[
            jax.ShapeDtypeStruct((n_tok, D_MODEL), F32),
            jax.ShapeDtypeStruct((n_tok // seq, CONV_HEAD, 2 * D_FF), F32),
        ],
        scratch_shapes=[
            pltpu.VMEM((pool_rows, POOL_WIDTH), F32),
            pltpu.VMEM((pool_rows, POOL_WIDTH), F32),
            pltpu.VMEM((pool_rows, 3 * G), F32),
            pltpu.VMEM((pool_rows, 2 * G), F32),
            pltpu.VMEM((TM, D_MODEL), F32),
            pltpu.VMEM((2, TM + CONV_HEAD, FF_CHUNK), F32),
            pltpu.VMEM((CONV_HEAD, 2 * D_FF), F32),
            pltpu.VMEM((TM, D_FF), BF16),
        ],
        compiler_params=_params(("arbitrary",)),
        name="tail_prompt",
    )(x2d, attn, u, w_pool, pool_scale, w_out, g_ffn, w_up, conv_w, conv_b, w_down)


def _ffn_sample_kernel(h_ref, st_ref, g_ref, wu_ref, cw_ref, cb_ref, wd_ref, y_ref, hist_ref):
    nseq = h_ref.shape[0]
    C = FF_CHUNK
    K = CONV_WIDTH
    hs = [h_ref[:, n * D_MODEL:(n + 1) * D_MODEL] for n in range(N_NEW)]
    hn = jnp.concatenate([_rms(h, g_ref) for h in hs], axis=0)
    for n in range(N_NEW):
        y_ref[:, n * D_MODEL:(n + 1) * D_MODEL] = hs[n]
    for c in range(N_FF_CHUNKS):
        halves = []
        for half in range(2):
            cols = slice(half * D_FF + c * C, half * D_FF + (c + 1) * C)
            up = jnp.dot(hn, wu_ref[:, cols], preferred_element_type=F32)
            rows = [st_ref[:, t * 2 * D_FF + cols.start:t * 2 * D_FF + cols.stop] for t in range(K - 1)]
            rows += [up[n * nseq:(n + 1) * nseq] for n in range(N_NEW)]
            conv = [cb_ref[:, cols] + sum(rows[n + j] * cw_ref[j:j + 1, cols] for j in range(K))
                    for n in range(N_NEW)]
            for t in range(K - 1):
                hist_ref[:, t * 2 * D_FF + cols.start:t * 2 * D_FF + cols.stop] = rows[N_NEW + t]
            halves.append(jnp.concatenate(conv, axis=0))
        act = _silu_gate(halves[0], halves[1])
        out = jnp.dot(act, wd_ref[c * C:(c + 1) * C, :], preferred_element_type=F32)
        for n in range(N_NEW):
            y_ref[:, n * D_MODEL:(n + 1) * D_MODEL] += out[n * nseq:(n + 1) * nseq]


def _ffn_sample(h, state_ffn, g_ffn, w_up, conv_w, conv_b, w_down):
    nseq = h.shape[0]
    full = lambda a: pl.BlockSpec(a.shape, lambda i: (0,) * a.ndim, pipeline_mode=pl.Buffered(1))
    args = (h, state_ffn, g_ffn, w_up, conv_w, conv_b, w_down)
    return pl.pallas_call(
        _ffn_sample_kernel,
        grid=(1,),
        in_specs=[full(a) for a in args],
        out_specs=[
            pl.BlockSpec((nseq, N_NEW * D_MODEL), lambda i: (0, 0)),
            pl.BlockSpec((nseq, (CONV_WIDTH - 1) * 2 * D_FF), lambda i: (0, 0)),
        ],
        out_shape=[
            jax.ShapeDtypeStruct((nseq, N_NEW * D_MODEL), F32),
            jax.ShapeDtypeStruct((nseq, (CONV_WIDTH - 1) * 2 * D_FF), F32),
        ],
        compiler_params=_params(("arbitrary",)),
        name="ffn_sample",
    )(*args)


def kernel(x_prompt, x_sample, cache_k, cache_v, state_pool, state_ffn_conv, g_attn_norm, w_in, g_q, g_k,
           w_pool, pool_scale, w_out, g_ffn_norm, w_up, conv_w, conv_b, w_down):
    depth = w_in.shape[0]
    assert depth == 1
    batch, seq, _ = x_prompt.shape
    nseq, n_new, _ = x_sample.shape
    assert n_new == N_NEW and seq % (16 * BLK) == 0 and seq % TM == 0
    assert cache_k.shape[2] == 16 * N_STRIDED

    slopes = jnp.asarray(2.0 ** (-8.0 * np.arange(1, N_HEADS + 1) / N_HEADS), dtype=F32)
    head_of_lane = np.arange(ATTN_WIDTH) // HEAD_DIM
    head_mean = jnp.asarray((head_of_lane[:, None] == head_of_lane[None, :]) / HEAD_DIM, dtype=BF16)

    l = 0
    w_in_b = w_in[l].astype(BF16)
    w_pool_b = w_pool[l].astype(BF16)
    w_out_b = w_out[l].astype(BF16)
    w_up_b = w_up[l].astype(BF16)
    w_down_b = w_down[l].astype(BF16)
    g_attn = g_attn_norm[l].reshape(1, D_MODEL)
    g_ffn = g_ffn_norm[l].reshape(1, D_MODEL)
    gq = g_q[l].reshape(1, ATTN_WIDTH)
    gk = g_k[l].reshape(1, ATTN_WIDTH)
    ps = pool_scale[l].reshape(1, POOL_WIDTH)
    cb = conv_b[l].reshape(1, 2 * D_FF)
    cw = conv_w[l]

    n_tok = batch * seq
    xp = x_prompt.reshape(n_tok, D_MODEL)
    qp, kp, vp, up, kpt, vpt = _proj(xp, TM, n_tok // TM, lambda i: (i, 0), lambda i: (i, 0),
                                     (n_tok, ATTN_WIDTH), g_attn, w_in_b, gq, gk, head_mean, seq=seq)
    attn_p = _attn_prompt(slopes, qp, kp, vp, batch, seq)
    yp, hist_p = _tail_prompt(xp, attn_p, up, w_pool_b, ps, w_out_b, g_ffn, w_up_b, cw, cb, w_down_b, seq)

    xs = x_sample.reshape(nseq, N_NEW * D_MODEL)
    qs, ks, vs, us = _proj(xs, nseq, N_NEW, lambda n: (0, n), lambda n: (0, n), (nseq, N_NEW * ATTN_WIDTH),
                           g_attn, w_in_b, gq, gk, head_mean)
    per_seq = lambda a: a.reshape(nseq, N_NEW, ATTN_WIDTH)
    cache_kt = jnp.transpose(cache_k[l], (0, 2, 3, 1)).reshape(nseq, ATTN_WIDTH, -1)
    cache_vt = jnp.transpose(cache_v[l], (0, 2, 3, 1)).reshape(nseq, ATTN_WIDTH, -1)
    slope_rows = jnp.tile(slopes, N_NEW).reshape(QROWS, 1)
    attn_s = _attn_sample(slope_rows, per_seq(qs), per_seq(ks), per_seq(vs), cache_kt, cache_vt)
    attn_s = attn_s.reshape(nseq, N_NEW * ATTN_WIDTH)
    st_pool = jnp.transpose(state_pool[l], (1, 0, 2))
    hs = _mix_sample(xs, attn_s, us, st_pool, w_pool_b, ps, w_out_b)
    st_ffn = state_ffn_conv[l].reshape(nseq, (CONV_WIDTH - 1) * 2 * D_FF)
    ys, hist_s = _ffn_sample(hs, st_ffn, g_ffn, w_up_b, cw, cb, w_down_b)

    n_keep = min(16 * N_STRIDED, seq)
    window = lambda t: jnp.transpose(
        t.reshape(batch, N_HEADS, HEAD_DIM, seq), (0, 3, 1, 2))[None, :, seq - n_keep:]
    kp5, vp5 = window(kpt), window(vpt)
    new_pool_p = up.reshape(batch, seq, POOL_WIDTH)[None, :, seq - POOL_HIST:]
    new_ffn_p = hist_p[None, :, CONV_HEAD - (CONV_WIDTH - 1):]
    u_time = jnp.transpose(us.reshape(nseq, N_NEW, POOL_WIDTH), (1, 0, 2))
    new_pool_s = jnp.transpose(jnp.concatenate([st_pool[N_NEW:], u_time], axis=0), (1, 0, 2))[None]
    return (
        yp.reshape(batch, seq, D_MODEL),
        ys.reshape(nseq, N_NEW, D_MODEL),
        kp5, vp5, new_pool_p, new_ffn_p,
        ks.reshape(1, nseq, N_NEW, N_HEADS, HEAD_DIM),
        vs.reshape(1, nseq, N_NEW, N_HEADS, HEAD_DIM),
        new_pool_s,
        hist_s.reshape(1, nseq, CONV_WIDTH - 1, 2 * D_FF),
    )
```

```python
import functools

import numpy as np
import jax
import jax.numpy as jnp
from jax import lax
from jax.experimental import pallas as pl
from jax.experimental.pallas import tpu as pltpu

D_MODEL = 1024
HEAD_DIM = 64
N_HEADS = 8
ATTN_WIDTH = N_HEADS * HEAD_DIM
POOL_WIDTH = 512
POOL_WINDOWS = (2, 4, 8, 16)
POOL_GROUP = 128
POOL_HIST = 15
PROJ_WIDTH = 3 * ATTN_WIDTH + POOL_WIDTH
DILATIONS = (1, 4, 16)
N_STRIDED = 128
BLK = 128
D_FF = 2816
CONV_WIDTH = 3
PAST_LEN = 8192
EPS = 1e-6
NEG = -1e30
LOG2E = 1.4426950408889634

F32 = jnp.float32
BF16 = jnp.bfloat16

TM = 512
ATTN_UNITS = 8
FF_CHUNK = 256
N_FF_CHUNKS = D_FF // FF_CHUNK
POOL_HEAD = 24
CONV_HEAD = 8
VMEM_LIMIT = 56 * 1024 * 1024


def _params(sem, vmem=VMEM_LIMIT):
    return pltpu.CompilerParams(dimension_semantics=sem, vmem_limit_bytes=vmem)


def _const_spec(shape):
    nd = len(shape)
    return pl.BlockSpec(shape, lambda *_: (0,) * nd, pipeline_mode=pl.Buffered(1))


def _proj_kernel(x_ref, g_ref, w_ref, gq_ref, gk_ref, hm_ref, q_ref, k_ref, v_ref, u_ref, *kvt_refs):
    x = x_ref[...]
    xn = x * lax.rsqrt(jnp.mean(x * x, axis=-1, keepdims=True) + EPS) * g_ref[...]
    proj = jnp.dot(xn.astype(BF16), w_ref[...], preferred_element_type=F32)

    def head_norm(t, g):
        ms = jnp.dot((t * t).astype(BF16), hm_ref[...], preferred_element_type=F32)
        return t * lax.rsqrt(ms + EPS) * g

    q = head_norm(proj[:, :ATTN_WIDTH], gq_ref[...])
    q_ref[...] = q * (HEAD_DIM ** -0.5 * LOG2E)
    k = head_norm(proj[:, ATTN_WIDTH:2 * ATTN_WIDTH], gk_ref[...])
    v = proj[:, 2 * ATTN_WIDTH:3 * ATTN_WIDTH]
    k_ref[...] = k
    v_ref[...] = v
    u_ref[...] = proj[:, 3 * ATTN_WIDTH:]
    if kvt_refs:
        kvt_refs[0][0] = k.T
        kvt_refs[1][0] = v.T


def _proj(x2d, rows, n_steps, x_map, o_map, out_rows, g_attn, w_in, gq, gk, head_mean, seq=None):
    out = jax.ShapeDtypeStruct((out_rows[0], out_rows[1]), F32)
    out_specs = [pl.BlockSpec((rows, ATTN_WIDTH), o_map)] * 4
    out_shape = [out] * 4
    if seq is not None:
        tps = seq // rows
        out_specs += [pl.BlockSpec((1, ATTN_WIDTH, rows), lambda i: (i // tps, 0, i % tps))] * 2
        out_shape += [jax.ShapeDtypeStruct((out_rows[0] // seq, ATTN_WIDTH, seq), F32)] * 2
    return pl.pallas_call(
        _proj_kernel,
        grid=(n_steps,),
        in_specs=[
            pl.BlockSpec((rows, D_MODEL), x_map),
            _const_spec((1, D_MODEL)),
            _const_spec((D_MODEL, PROJ_WIDTH)),
            _const_spec((1, ATTN_WIDTH)),
            _const_spec((1, ATTN_WIDTH)),
            _const_spec((ATTN_WIDTH, ATTN_WIDTH)),
        ],
        out_specs=out_specs,
        out_shape=out_shape,
        compiler_params=_params(("arbitrary",)),
        name="proj",
    )(x2d, g_attn, w_in, gq, gk, head_mean)


def _attn_prompt_kernel(slopes_ref, q_ref, k_ref, v_ref, o_ref, bias_s, acc_s, m_s, l_s):
    hp = pl.program_id(1)
    seq = q_ref.shape[0]
    lane = lax.broadcasted_iota(jnp.int32, (BLK, 2 * HEAD_DIM), 1)
    left = lane < HEAD_DIM

    row = lax.broadcasted_iota(jnp.int32, (2 * BLK, 2 * BLK), 0)
    col = lax.broadcasted_iota(jnp.int32, (2 * BLK, 2 * BLK), 1)
    diff = (row & (BLK - 1)) + BLK - col
    valid = (diff >= 0) & (diff <= N_STRIDED)
    slope = jnp.where(row < BLK, slopes_ref[2 * hp], slopes_ref[2 * hp + 1])
    for pi, r in enumerate(DILATIONS):
        bias_s[pi] = jnp.where(valid, (-LOG2E) * slope * (r * diff).astype(F32), NEG)

    def rows_of(start, r, n=BLK):
        return pl.ds(start, n) if r == 1 else pl.ds(start, n, stride=r)

    ones_cols = jnp.ones((2 * BLK, 2 * HEAD_DIM), BF16)

    def units(specs, with_prev):
        curs = [rows_of(st, r) for _, r, st in specs]
        qqs, kks, vvs = [], [], []
        for (_, r, st), cur in zip(specs, curs):
            q2 = q_ref[cur, :]
            qqs.append(
                jnp.concatenate([jnp.where(left, q2, 0.0), jnp.where(left, 0.0, q2)], axis=0).astype(BF16))
            if not with_prev:
                kk, vv = k_ref[cur, :], v_ref[cur, :]
            elif r == 1:
                both = pl.ds(st - BLK, 2 * BLK)
                kk, vv = k_ref[both, :], v_ref[both, :]
            else:
                prev = rows_of(st - BLK * r, r)
                kk = jnp.concatenate([k_ref[prev, :], k_ref[cur, :]], axis=0)
                vv = jnp.concatenate([v_ref[prev, :], v_ref[cur, :]], axis=0)
            kks.append(kk.astype(BF16))
            vvs.append(jnp.concatenate([vv.astype(BF16), ones_cols[:vv.shape[0]]], axis=1))
        ss = [lax.dot_general(qq, kk, (((1,), (1,)), ((), ())), preferred_element_type=F32)
              + (bias_s[pi] if with_prev else bias_s[pi, :, BLK:])
              for (pi, _, _), qq, kk in zip(specs, qqs, kks)]
        ms = [jnp.max(s, axis=-1, keepdims=True) for s in ss]
        ps = [jnp.exp2(s - m).astype(BF16) for s, m in zip(ss, ms)]
        pvs = [jnp.dot(p, vv, preferred_element_type=F32) for p, vv in zip(ps, vvs)]
        W2 = 2 * HEAD_DIM
        for (pi, _, _), cur, pv, m in zip(specs, curs, pvs, ms):
            acc_s[pi, cur, :] = jnp.where(left, pv[:BLK, :W2], pv[BLK:, :W2])
            m_s[pi, cur, :] = jnp.where(left, m[:BLK], m[BLK:])
            l_s[pi, cur, :] = jnp.where(left, pv[:BLK, W2:], pv[BLK:, W2:])

    U = ATTN_UNITS
    small = [(pi, r, c) for pi, r in enumerate(DILATIONS) if r <= U for c in range(r)]
    for i0 in range(0, len(small), U):
        units(small[i0:i0 + U], False)
    for pi, r in enumerate(DILATIONS):
        if r > U:
            def first(i, carry, pi=pi, r=r):
                units([(pi, r, i * U + j) for j in range(U)], False)
                return carry

            lax.fori_loop(0, r // U, first, 0)
    for pi, r in enumerate(DILATIONS):
        n_rest = r * (seq // (r * BLK) - 1)
        if r == 1 and n_rest:
            g = 5 if n_rest % 5 == 0 else 1

            def rest1(i, carry, pi=pi, g=g):
                units([(pi, 1, pl.multiple_of((i * g + j + 1) * BLK, BLK)) for j in range(g)], True)
                return carry

            lax.fori_loop(0, n_rest // g, rest1, 0)
        elif n_rest:
            g = min(r, U)

            def rest(i, carry, pi=pi, r=r, g=g):
                blk = i // (r // g) + 1
                c0 = (i % (r // g)) * g
                units([(pi, r, c0 + j + blk * (BLK * r)) for j in range(g)], True)
                return carry

            lax.fori_loop(0, n_rest // g, rest, 0)

    def merge(i, carry):
        rows = pl.ds(pl.multiple_of(i * BLK, BLK), BLK)
        ms = [m_s[pi, rows, :] for pi in range(3)]
        mx = jnp.maximum(jnp.maximum(ms[0], ms[1]), ms[2])
        num = jnp.zeros((BLK, 2 * HEAD_DIM), F32)
        den = jnp.zeros((BLK, 2 * HEAD_DIM), F32)
        for pi in range(3):
            a = jnp.exp2(ms[pi] - mx)
            num = num + a * acc_s[pi, rows, :]
            den = den + a * l_s[pi, rows, :]
        o_ref[rows, :] = (num / den).astype(o_ref.dtype)
        return carry

    lax.fori_loop(0, seq // BLK, merge, 0)


def _attn_prompt(slopes, q, k, v, batch, seq):
    blk = pl.BlockSpec((seq, 2 * HEAD_DIM), lambda b, h: (b, h))
    return pl.pallas_call(
        _attn_prompt_kernel,
        grid=(batch, N_HEADS // 2),
        in_specs=[pl.BlockSpec(memory_space=pltpu.SMEM), blk, blk, blk],
        out_specs=blk,
        out_shape=jax.ShapeDtypeStruct((batch * seq, ATTN_WIDTH), BF16),
        scratch_shapes=[
            pltpu.VMEM((3, 2 * BLK, 2 * BLK), F32),
            pltpu.VMEM((3, seq, 2 * HEAD_DIM), F32),
            pltpu.VMEM((3, seq, 2 * HEAD_DIM), F32),
            pltpu.VMEM((3, seq, 2 * HEAD_DIM), F32),
        ],
        compiler_params=_params(("arbitrary", "arbitrary")),
        name="attn_prompt",
    )(slopes, q, k, v)


N_NEW = 4
QROWS = N_NEW * N_HEADS


SAMPLE_TCHUNK = 512


def _attn_sample_tables(slope_ref, bias_s, mult_s):
    wb = bias_s.shape[1]
    n_of_row = lax.broadcasted_iota(jnp.int32, (QROWS, 1), 0) // N_HEADS
    t = lax.broadcasted_iota(jnp.int32, (QROWS, wb), 1)
    dist = wb + n_of_row - t
    mult = jnp.zeros((QROWS, wb), F32)
    for r in DILATIONS:
        hit = ((dist & (r - 1)) == 0) & (dist <= r * N_STRIDED)
        mult = mult + jnp.where(hit, 1.0, 0.0)
    mult_s[...] = mult
    bias_s[...] = jnp.where(mult > 0.0, (-LOG2E) * slope_ref[...] * dist.astype(F32), NEG)


def _attn_sample_step(slope_ref, q_ref, kn_ref, vn_ref, kt_ref, vt_ref, o_ref, bias_s, mult_s):
    W = ATTN_WIDTH
    wb = kt_ref.shape[2]
    chunks = [slice(c, c + SAMPLE_TCHUNK) for c in range(0, wb, SAMPLE_TCHUNK)]
    slope = slope_ref[...]
    n_of_row = lax.broadcasted_iota(jnp.int32, (QROWS, 1), 0) // N_HEADS
    q = q_ref[0]
    kn = kn_ref[0]
    vn = vn_ref[0]
    sub = lax.broadcasted_iota(jnp.int32, (N_HEADS, W), 0)
    own = sub == lax.broadcasted_iota(jnp.int32, (N_HEADS, W), 1) // HEAD_DIM
    own_rows = jnp.concatenate([own] * N_NEW, axis=0)
    qbd = jnp.concatenate(
        [jnp.where(own, jnp.broadcast_to(q[n:n + 1], (N_HEADS, W)), 0.0) for n in range(N_NEW)], axis=0)
    qbd16 = qbd.astype(BF16)

    ss = [jnp.dot(qbd16, kt_ref[0, :, c].astype(BF16), preferred_element_type=F32) + bias_s[:, c]
          for c in chunks]
    s_new, w_new = [], []
    for m in range(N_NEW):
        sm = jnp.sum(qbd * kn[m:m + 1], axis=-1, keepdims=True)
        gap = n_of_row - m
        s_new.append(jnp.where(gap >= 0, sm - LOG2E * slope * gap.astype(F32), NEG))
        w_new.append(jnp.where(gap == 0, float(len(DILATIONS)), jnp.where(gap > 0, 1.0, 0.0)))
    mx = s_new[0]
    for sm in s_new[1:] + [jnp.max(s, axis=-1, keepdims=True) for s in ss]:
        mx = jnp.maximum(mx, sm)
    den = jnp.zeros((QROWS, 1), F32)
    pv = jnp.zeros((QROWS, W), F32)
    for s, c in zip(ss, chunks):
        p = jnp.exp2(s - mx) * mult_s[:, c]
        den = den + jnp.sum(p, axis=-1, keepdims=True)
        pv = pv + lax.dot_general(p.astype(BF16), vt_ref[0, :, c].astype(BF16), (((1,), (1,)), ((), ())),
                                  preferred_element_type=F32)
    for m in range(N_NEW):
        pm = jnp.exp2(s_new[m] - mx) * w_new[m]
        den = den + pm
        pv = pv + pm * vn[m:m + 1]
    full = jnp.where(own_rows, pv / den, 0.0)
    o_ref[0] = jnp.sum(full.reshape(N_NEW, N_HEADS, W), axis=1).astype(o_ref.dtype)


def _pool_mix(d_groups, wp_ref, ps_ref):
    mixed = [jnp.dot(d.astype(BF16), wp_ref[g], preferred_element_type=F32) for g, d in enumerate(d_groups)]
    return jnp.concatenate(mixed, axis=-1) * ps_ref[...]


def _out_proj(x, attn, pool_out, wo_ref):
    h = x + jnp.dot(attn, wo_ref[:ATTN_WIDTH, :], preferred_element_type=F32)
    return h + jnp.dot(pool_out.astype(BF16), wo_ref[ATTN_WIDTH:, :], preferred_element_type=F32)


def _mix_prompt_tile(x_ref, a_ref, u_ref, wp_ref, ps_ref, wo_ref, e_s, s2_s, s4_s, s8_s, tiles_per_seq):
    i = pl.program_id(0)
    tm = u_ref.shape[0]
    H = POOL_HEAD
    G = POOL_GROUP

    @pl.when(i == 0)
    def _():
        e_s[0:8, :] = jnp.zeros((8, POOL_WIDTH), F32)
        s2_s[0:8, :] = jnp.zeros((8, POOL_WIDTH), F32)
        s4_s[0:8, :] = jnp.zeros((8, 3 * G), F32)
        s8_s[0:8, :] = jnp.zeros((8, 2 * G), F32)

    @pl.when(i % tiles_per_seq == 0)
    def _():
        e_s[8:H, :] = jnp.zeros((H - 8, POOL_WIDTH), F32)

    u = u_ref[...]
    e_s[H:, :] = u
    n = tm + H - 8
    s2_s[8:, :] = e_s[8:, :] + e_s[7:7 + n, :]
    s4_s[8:, :] = s2_s[8:, G:] + s2_s[6:6 + n, G:]
    s8_s[8:, :] = s4_s[8:, G:] + s4_s[4:4 + n, G:]
    sums = [
        s2_s[H:, :G],
        s4_s[H:, :G],
        s8_s[H:, :G],
        s8_s[H:, G:] + s8_s[H - 8:H - 8 + tm, G:],
    ]
    pos = (i % tiles_per_seq) * tm + lax.broadcasted_iota(jnp.int32, (tm, 1), 0)
    d = []
    for g, w in enumerate(POOL_WINDOWS):
        cnt = jnp.minimum(w, pos + 1).astype(F32)
        d.append(sums[g] / cnt - u[:, g * G:(g + 1) * G])
    pool_out = _pool_mix(d, wp_ref, ps_ref)
    h = _out_proj(x_ref[...], a_ref[...], pool_out, wo_ref)
    e_s[8:H, :] = e_s[tm + 8:tm + H, :]
    return h


def _mix_sample_kernel(x_ref, a_ref, u_ref, st_ref, wp_ref, ps_ref, wo_ref, h_ref):
    G = POOL_GROUP
    PW = POOL_WIDTH

    def row(t, lanes):
        if t < POOL_HIST:
            return st_ref[t, :, lanes]
        return u_ref[:, (t - POOL_HIST) * PW + lanes.start:(t - POOL_HIST) * PW + lanes.stop]

    for n in range(N_NEW):
        t = POOL_HIST + n
        d = []
        for g, w in enumerate(POOL_WINDOWS):
            lanes = slice(g * G, (g + 1) * G)
            cur = row(t, lanes)
            tot = cur
            for j in range(1, w):
                tot = tot + row(t - j, lanes)
            d.append(tot / float(min(w, PAST_LEN + n + 1)) - cur)
        pool_out = _pool_mix(d, wp_ref, ps_ref)
        cols = slice(n * D_MODEL, (n + 1) * D_MODEL)
        attn = a_ref[:, n * ATTN_WIDTH:(n + 1) * ATTN_WIDTH]
        h_ref[:, cols] = _out_proj(x_ref[:, cols], attn, pool_out, wo_ref)


def _mix_sample(x, attn, u, state_pool, w_pool, pool_scale, w_out):
    nseq = x.shape[0]
    full = lambda a: pl.BlockSpec(a.shape, lambda i: (0,) * a.ndim)
    args = (x, attn, u, state_pool, w_pool, pool_scale, w_out)
    return pl.pallas_call(
        _mix_sample_kernel,
        grid=(1,),
        in_specs=[full(a) for a in args],
        out_specs=pl.BlockSpec((nseq, N_NEW * D_MODEL), lambda i: (0, 0)),
        out_shape=jax.ShapeDtypeStruct((nseq, N_NEW * D_MODEL), F32),
        compiler_params=_params(("arbitrary",)),
        name="mix_sample",
    )(*args)


def _rms(h, g_ref):
    return (h * lax.rsqrt(jnp.mean(h * h, axis=-1, keepdims=True) + EPS) * g_ref[...]).astype(BF16)


def _silu_gate(gate, val):
    return (gate / (1.0 + jnp.exp(-gate)) * val).astype(BF16)


def _tail_prompt_kernel(x_ref, a_ref, u_ref, wp_ref, ps_ref, wo_ref, g_ref, wu_ref, cw_ref, cb_ref, wd_ref,
                        slope_ref, qs_ref, kn_ref, vn_ref, kt_ref, vt_ref,
                        y_ref, hist_ref, os_ref,
                        e_s, s2_s, s4_s, s8_s, h_s, ext_s, carry_s, act_s, bias_s, mult_s, *, tiles_per_seq):
    i = pl.program_id(0)
    tm = x_ref.shape[0]
    C = FF_CHUNK
    HD = CONV_HEAD

    @pl.when(i == 0)
    def _():
        _attn_sample_tables(slope_ref, bias_s, mult_s)

    @pl.when(i % tiles_per_seq == 0)
    def _():
        carry_s[...] = jnp.zeros(carry_s.shape, F32)

    h_s[...] = _mix_prompt_tile(x_ref, a_ref, u_ref, wp_ref, ps_ref, wo_ref, e_s, s2_s, s4_s, s8_s,
                                tiles_per_seq)
    _attn_sample_step(slope_ref, qs_ref, kn_ref, vn_ref, kt_ref, vt_ref, os_ref, bias_s, mult_s)
    hn = _rms(h_s[...], g_ref)
    for c in range(N_FF_CHUNKS):
        halves = []
        for half in range(2):
            cols = slice(half * D_FF + c * C, half * D_FF + (c + 1) * C)
            up = jnp.dot(hn, wu_ref[:, cols], preferred_element_type=F32)
            ext = ext_s.at[half]
            ext[0:HD, :] = carry_s[:, cols]
            ext[HD:, :] = up
            conv = cb_ref[:, cols] + sum(
                ext[HD - (CONV_WIDTH - 1) + j:HD - (CONV_WIDTH - 1) + j + tm, :] * cw_ref[j:j + 1, cols]
                for j in range(CONV_WIDTH))
            carry_s[:, cols] = ext[tm:tm + HD, :]
            halves.append(conv)
        act_s[:, c * C:(c + 1) * C] = _silu_gate(halves[0], halves[1])
    y_ref[...] = h_s[...] + jnp.dot(act_s[...], wd_ref[...], preferred_element_type=F32)
    hist_ref[0] = carry_s[...]


def _tail_prompt(x2d, attn, u, w_pool, pool_scale, w_out, g_ffn, w_up, conv_w, conv_b, w_down, seq,
                 slope_rows, q_s, kn_s, vn_s, cache_kt, cache_vt):
    n_tok = x2d.shape[0]
    nseq, W, wb = cache_kt.shape
    tm = n_tok // nseq
    assert tm * nseq == n_tok and tm % 8 == 0 and seq % tm == 0
    tps = seq // tm
    G = POOL_GROUP
    pool_rows = tm + POOL_HEAD
    new_spec = pl.BlockSpec((1, N_NEW, W), lambda i: (i, 0, 0))
    cache_spec = pl.BlockSpec((1, W, wb), lambda i: (i, 0, 0))
    return pl.pallas_call(
        functools.partial(_tail_prompt_kernel, tiles_per_seq=tps),
        grid=(nseq,),
        in_specs=[
            pl.BlockSpec((tm, D_MODEL), lambda i: (i, 0)),
            pl.BlockSpec((tm, ATTN_WIDTH), lambda i: (i, 0)),
            pl.BlockSpec((tm, POOL_WIDTH), lambda i: (i, 0)),
            _const_spec((len(POOL_WINDOWS), G, G)),
            _const_spec((1, POOL_WIDTH)),
            _const_spec((D_MODEL, D_MODEL)),
            _const_spec((1, D_MODEL)),
            _const_spec((D_MODEL, 2 * D_FF)),
            _const_spec((CONV_WIDTH, 2 * D_FF)),
            _const_spec((1, 2 * D_FF)),
            _const_spec((D_FF, D_MODEL)),
            _const_spec((QROWS, 1)),
            new_spec, new_spec, new_spec, cache_spec, cache_spec,
        ],
        out_specs=[
            pl.BlockSpec((tm, D_MODEL), lambda i: (i, 0)),
            pl.BlockSpec((1, CONV_HEAD, 2 * D_FF), lambda i: (i // tps, 0, 0)),
            new_spec,
        ],
        out_shape=[
            jax.ShapeDtypeStruct((n_tok, D_MODEL), F32),
            jax.ShapeDtypeStruct((n_tok // seq, CONV_HEAD, 2 * D_FF), F32),
            jax.ShapeDtypeStruct((nseq, N_NEW, W), BF16),
        ],
        scratch_shapes=[
            pltpu.VMEM((pool_rows, POOL_WIDTH), F32),
            pltpu.VMEM((pool_rows, POOL_WIDTH), F32),
            pltpu.VMEM((pool_rows, 3 * G), F32),
            pltpu.VMEM((pool_rows, 2 * G), F32),
            pltpu.VMEM((tm, D_MODEL), F32),
            pltpu.VMEM((2, tm + CONV_HEAD, FF_CHUNK), F32),
            pltpu.VMEM((CONV_HEAD, 2 * D_FF), F32),
            pltpu.VMEM((tm, D_FF), BF16),
            pltpu.VMEM((QROWS, wb), F32),
            pltpu.VMEM((QROWS, wb), F32),
        ],
        compiler_params=_params(("arbitrary",)),
        name="tail_prompt",
    )(x2d, attn, u, w_pool, pool_scale, w_out, g_ffn, w_up, conv_w, conv_b, w_down,
      slope_rows, q_s, kn_s, vn_s, cache_kt, cache_vt)


def _ffn_sample_kernel(h_ref, st_ref, g_ref, wu_ref, cw_ref, cb_ref, wd_ref, y_ref, hist_ref):
    nseq = h_ref.shape[0]
    C = FF_CHUNK
    K = CONV_WIDTH
    hs = [h_ref[:, n * D_MODEL:(n + 1) * D_MODEL] for n in range(N_NEW)]
    hn = jnp.concatenate([_rms(h, g_ref) for h in hs], axis=0)
    for n in range(N_NEW):
        y_ref[:, n * D_MODEL:(n + 1) * D_MODEL] = hs[n]
    for c in range(N_FF_CHUNKS):
        halves = []
        for half in range(2):
            cols = slice(half * D_FF + c * C, half * D_FF + (c + 1) * C)
            up = jnp.dot(hn, wu_ref[:, cols], preferred_element_type=F32)
            rows = [st_ref[:, t * 2 * D_FF + cols.start:t * 2 * D_FF + cols.stop] for t in range(K - 1)]
            rows += [up[n * nseq:(n + 1) * nseq] for n in range(N_NEW)]
            conv = [cb_ref[:, cols] + sum(rows[n + j] * cw_ref[j:j + 1, cols] for j in range(K))
                    for n in range(N_NEW)]
            for t in range(K - 1):
                hist_ref[:, t * 2 * D_FF + cols.start:t * 2 * D_FF + cols.stop] = rows[N_NEW + t]
            halves.append(jnp.concatenate(conv, axis=0))
        act = _silu_gate(halves[0], halves[1])
        out = jnp.dot(act, wd_ref[c * C:(c + 1) * C, :], preferred_element_type=F32)
        for n in range(N_NEW):
            y_ref[:, n * D_MODEL:(n + 1) * D_MODEL] += out[n * nseq:(n + 1) * nseq]


def _ffn_sample(h, state_ffn, g_ffn, w_up, conv_w, conv_b, w_down):
    nseq = h.shape[0]
    full = lambda a: pl.BlockSpec(a.shape, lambda i: (0,) * a.ndim, pipeline_mode=pl.Buffered(1))
    args = (h, state_ffn, g_ffn, w_up, conv_w, conv_b, w_down)
    return pl.pallas_call(
        _ffn_sample_kernel,
        grid=(1,),
        in_specs=[full(a) for a in args],
        out_specs=[
            pl.BlockSpec((nseq, N_NEW * D_MODEL), lambda i: (0, 0)),
            pl.BlockSpec((nseq, (CONV_WIDTH - 1) * 2 * D_FF), lambda i: (0, 0)),
        ],
        out_shape=[
            jax.ShapeDtypeStruct((nseq, N_NEW * D_MODEL), F32),
            jax.ShapeDtypeStruct((nseq, (CONV_WIDTH - 1) * 2 * D_FF), F32),
        ],
        compiler_params=_params(("arbitrary",)),
        name="ffn_sample",
    )(*args)


def kernel(x_prompt, x_sample, cache_k, cache_v, state_pool, state_ffn_conv, g_attn_norm, w_in, g_q, g_k,
           w_pool, pool_scale, w_out, g_ffn_norm, w_up, conv_w, conv_b, w_down):
    depth = w_in.shape[0]
    assert depth == 1
    batch, seq, _ = x_prompt.shape
    nseq, n_new, _ = x_sample.shape
    assert n_new == N_NEW and seq % (16 * BLK) == 0 and seq % TM == 0
    assert cache_k.shape[2] == 16 * N_STRIDED

    slopes = jnp.asarray(2.0 ** (-8.0 * np.arange(1, N_HEADS + 1) / N_HEADS), dtype=F32)
    head_of_lane = np.arange(ATTN_WIDTH) // HEAD_DIM
    head_mean = jnp.asarray((head_of_lane[:, None] == head_of_lane[None, :]) / HEAD_DIM, dtype=BF16)

    l = 0
    w_in_b = w_in[l].astype(BF16)
    w_pool_b = w_pool[l].astype(BF16)
    w_out_b = w_out[l].astype(BF16)
    w_up_b = w_up[l].astype(BF16)
    w_down_b = w_down[l].astype(BF16)
    g_attn = g_attn_norm[l].reshape(1, D_MODEL)
    g_ffn = g_ffn_norm[l].reshape(1, D_MODEL)
    gq = g_q[l].reshape(1, ATTN_WIDTH)
    gk = g_k[l].reshape(1, ATTN_WIDTH)
    ps = pool_scale[l].reshape(1, POOL_WIDTH)
    cb = conv_b[l].reshape(1, 2 * D_FF)
    cw = conv_w[l]

    xs = x_sample.reshape(nseq, N_NEW * D_MODEL)
    qs, ks, vs, us = _proj(xs, nseq, N_NEW, lambda n: (0, n), lambda n: (0, n), (nseq, N_NEW * ATTN_WIDTH),
                           g_attn, w_in_b, gq, gk, head_mean)
    per_seq = lambda a: a.reshape(nseq, N_NEW, ATTN_WIDTH)
    cache_kt = jnp.transpose(cache_k[l], (0, 2, 3, 1)).reshape(nseq, ATTN_WIDTH, -1)
    cache_vt = jnp.transpose(cache_v[l], (0, 2, 3, 1)).reshape(nseq, ATTN_WIDTH, -1)
    slope_rows = jnp.tile(slopes, N_NEW).reshape(QROWS, 1)

    n_tok = batch * seq
    xp = x_prompt.reshape(n_tok, D_MODEL)
    qp, kp, vp, up, kpt, vpt = _proj(xp, TM, n_tok // TM, lambda i: (i, 0), lambda i: (i, 0),
                                     (n_tok, ATTN_WIDTH), g_attn, w_in_b, gq, gk, head_mean, seq=seq)
    attn_p = _attn_prompt(slopes, qp, kp, vp, batch, seq)
    yp, hist_p, attn_s = _tail_prompt(xp, attn_p, up, w_pool_b, ps, w_out_b, g_ffn, w_up_b, cw, cb, w_down_b,
                                      seq, slope_rows, per_seq(qs), per_seq(ks), per_seq(vs), cache_kt, cache_vt)

    attn_s = attn_s.reshape(nseq, N_NEW * ATTN_WIDTH)
    st_pool = jnp.transpose(state_pool[l], (1, 0, 2))
    hs = _mix_sample(xs, attn_s, us, st_pool, w_pool_b, ps, w_out_b)
    st_ffn = state_ffn_conv[l].reshape(nseq, (CONV_WIDTH - 1) * 2 * D_FF)
    ys, hist_s = _ffn_sample(hs, st_ffn, g_ffn, w_up_b, cw, cb, w_down_b)

    n_keep = min(16 * N_STRIDED, seq)
    window = lambda t: jnp.transpose(
        t.reshape(batch, N_HEADS, HEAD_DIM, seq), (0, 3, 1, 2))[None, :, seq - n_keep:]
    kp5, vp5 = window(kpt), window(vpt)
    new_pool_p = up.reshape(batch, seq, POOL_WIDTH)[None, :, seq - POOL_HIST:]
    new_ffn_p = hist_p[None, :, CONV_HEAD - (CONV_WIDTH - 1):]
    u_time = jnp.transpose(us.reshape(nseq, N_NEW, POOL_WIDTH), (1, 0, 2))
    new_pool_s = jnp.transpose(jnp.concatenate([st_pool[N_NEW:], u_time], axis=0), (1, 0, 2))[None]
    return (
        yp.reshape(batch, seq, D_MODEL),
        ys.reshape(nseq, N_NEW, D_MODEL),
        kp5, vp5, new_pool_p, new_ffn_p,
        ks.reshape(1, nseq, N_NEW, N_HEADS, HEAD_DIM),
        vs.reshape(1, nseq, N_NEW, N_HEADS, HEAD_DIM),
        new_pool_s,
        hist_s.reshape(1, nseq, CONV_WIDTH - 1, 2 * D_FF),
    )
```

```python
import functools

import numpy as np
import jax
import jax.numpy as jnp
from jax import lax
from jax.experimental import pallas as pl
from jax.experimental.pallas import tpu as pltpu

D_MODEL = 1024
HEAD_DIM = 64
N_HEADS = 8
ATTN_WIDTH = N_HEADS * HEAD_DIM
POOL_WIDTH = 512
POOL_WINDOWS = (2, 4, 8, 16)
POOL_GROUP = 128
POOL_HIST = 15
PROJ_WIDTH = 3 * ATTN_WIDTH + POOL_WIDTH
DILATIONS = (1, 4, 16)
N_STRIDED = 128
BLK = 128
D_FF = 2816
CONV_WIDTH = 3
PAST_LEN = 8192
EPS = 1e-6
NEG = -1e30
LOG2E = 1.4426950408889634

F32 = jnp.float32
BF16 = jnp.bfloat16

TM = 512
ATTN_MERGE_UNITS = 8
ATTN_UNITS = 8
FF_CHUNK = 256
N_FF_CHUNKS = D_FF // FF_CHUNK
POOL_HEAD = 24
SUBLANES = 8
LANES = 128
CONV_HEAD = (CONV_WIDTH - 1) * SUBLANES
VMEM_LIMIT = 56 * 1024 * 1024


def _params(sem, vmem=VMEM_LIMIT):
    return pltpu.CompilerParams(dimension_semantics=sem, vmem_limit_bytes=vmem)


def _const_spec(shape):
    nd = len(shape)
    return pl.BlockSpec(shape, lambda *_: (0,) * nd, pipeline_mode=pl.Buffered(1))


def _proj_kernel(x_ref, g_ref, w_ref, gq_ref, gk_ref, hm_ref, q_ref, k_ref, v_ref, u_ref, *kvt_refs):
    x = x_ref[...]
    xn = x * lax.rsqrt(jnp.mean(x * x, axis=-1, keepdims=True) + EPS) * g_ref[...]
    proj = jnp.dot(xn.astype(BF16), w_ref[...], preferred_element_type=F32)

    def head_norm(t, g):
        ms = jnp.dot((t * t).astype(BF16), hm_ref[...], preferred_element_type=F32)
        return t * lax.rsqrt(ms + EPS) * g

    q = head_norm(proj[:, :ATTN_WIDTH], gq_ref[...])
    q_ref[...] = q * (HEAD_DIM ** -0.5 * LOG2E)
    k = head_norm(proj[:, ATTN_WIDTH:2 * ATTN_WIDTH], gk_ref[...])
    v = proj[:, 2 * ATTN_WIDTH:3 * ATTN_WIDTH]
    k_ref[...] = k
    v_ref[...] = v
    u_ref[...] = proj[:, 3 * ATTN_WIDTH:]
    if kvt_refs:
        kvt_refs[0][0] = k.T
        kvt_refs[1][0] = v.T


def _proj(x2d, rows, n_steps, x_map, o_map, out_rows, g_attn, w_in, gq, gk, head_mean, seq=None):
    out = jax.ShapeDtypeStruct((out_rows[0], out_rows[1]), F32)
    out_specs = [pl.BlockSpec((rows, ATTN_WIDTH), o_map)] * 4
    out_shape = [out] * 4
    if seq is not None:
        tps = seq // rows
        out_specs += [pl.BlockSpec((1, ATTN_WIDTH, rows), lambda i: (i // tps, 0, i % tps))] * 2
        out_shape += [jax.ShapeDtypeStruct((out_rows[0] // seq, ATTN_WIDTH, seq), F32)] * 2
    return pl.pallas_call(
        _proj_kernel,
        grid=(n_steps,),
        in_specs=[
            pl.BlockSpec((rows, D_MODEL), x_map),
            _const_spec((1, D_MODEL)),
            _const_spec((D_MODEL, PROJ_WIDTH)),
            _const_spec((1, ATTN_WIDTH)),
            _const_spec((1, ATTN_WIDTH)),
            _const_spec((ATTN_WIDTH, ATTN_WIDTH)),
        ],
        out_specs=out_specs,
        out_shape=out_shape,
        compiler_params=_params(("arbitrary",)),
        name="proj",
    )(x2d, g_attn, w_in, gq, gk, head_mean)


def _attn_prompt_kernel(slopes_ref, q_ref, k_ref, v_ref, o_ref, bias_s, acc_s, m_s, l_s):
    hp = pl.program_id(1)
    seq = q_ref.shape[0]
    lane = lax.broadcasted_iota(jnp.int32, (BLK, 2 * HEAD_DIM), 1)
    left = lane < HEAD_DIM

    row = lax.broadcasted_iota(jnp.int32, (2 * BLK, 2 * BLK), 0)
    col = lax.broadcasted_iota(jnp.int32, (2 * BLK, 2 * BLK), 1)
    diff = (row & (BLK - 1)) + BLK - col
    valid = (diff >= 0) & (diff <= N_STRIDED)
    slope = jnp.where(row < BLK, slopes_ref[2 * hp], slopes_ref[2 * hp + 1])
    for pi, r in enumerate(DILATIONS):
        bias_s[pi] = jnp.where(valid, (-LOG2E) * slope * (r * diff).astype(F32), NEG)

    def rows_of(start, r, n=BLK):
        return pl.ds(start, n) if r == 1 else pl.ds(start, n, stride=r)

    ones_cols = jnp.ones((2 * BLK, 2 * HEAD_DIM), BF16)

    def units(specs, with_prev):
        curs = [rows_of(st, r) for _, r, st in specs]
        qqs, kks, vvs = [], [], []
        for (_, r, st), cur in zip(specs, curs):
            q2 = q_ref[cur, :]
            qqs.append(
                jnp.concatenate([jnp.where(left, q2, 0.0), jnp.where(left, 0.0, q2)], axis=0).astype(BF16))
            if not with_prev:
                kk, vv = k_ref[cur, :], v_ref[cur, :]
            elif r == 1:
                both = pl.ds(st - BLK, 2 * BLK)
                kk, vv = k_ref[both, :], v_ref[both, :]
            else:
                prev = rows_of(st - BLK * r, r)
                kk = jnp.concatenate([k_ref[prev, :], k_ref[cur, :]], axis=0)
                vv = jnp.concatenate([v_ref[prev, :], v_ref[cur, :]], axis=0)
            kks.append(kk.astype(BF16))
            vvs.append(jnp.concatenate([vv.astype(BF16), ones_cols[:vv.shape[0]]], axis=1))
        ss = [lax.dot_general(qq, kk, (((1,), (1,)), ((), ())), preferred_element_type=F32)
              + (bias_s[pi] if with_prev else bias_s[pi, :, BLK:])
              for (pi, _, _), qq, kk in zip(specs, qqs, kks)]
        ms = [jnp.max(s, axis=-1, keepdims=True) for s in ss]
        ps = [jnp.exp2(s - m).astype(BF16) for s, m in zip(ss, ms)]
        pvs = [jnp.dot(p, vv, preferred_element_type=F32) for p, vv in zip(ps, vvs)]
        W2 = 2 * HEAD_DIM
        for (pi, _, _), cur, pv, m in zip(specs, curs, pvs, ms):
            acc = jnp.where(left, pv[:BLK, :W2], pv[BLK:, :W2])
            mt = jnp.where(left, m[:BLK], m[BLK:])
            lt = jnp.where(left, pv[:BLK, W2:], pv[BLK:, W2:])
            if pi > 0:
                acc_s[pi - 1, cur, :] = acc
                m_s[pi - 1, cur, :] = mt
                l_s[pi - 1, cur, :] = lt
                continue
            parts = [(acc, mt, lt)] + [(acc_s[j, cur, :], m_s[j, cur, :], l_s[j, cur, :])
                                       for j in range(len(DILATIONS) - 1)]
            mx = parts[0][1]
            for _, mj, _ in parts[1:]:
                mx = jnp.maximum(mx, mj)
            num = jnp.zeros((BLK, W2), F32)
            den = jnp.zeros((BLK, W2), F32)
            for aj, mj, lj in parts:
                w = jnp.exp2(mj - mx)
                num = num + w * aj
                den = den + w * lj
            o_ref[cur, :] = (num / den).astype(o_ref.dtype)

    U = ATTN_UNITS
    assert DILATIONS[0] == 1
    for pats, per_group in ((range(1, len(DILATIONS)), U), ((0,), ATTN_MERGE_UNITS)):
        first = [(pi, DILATIONS[pi], c) for pi in pats for c in range(DILATIONS[pi])]
        rest = [(pi, DILATIONS[pi], c + blk * (BLK * DILATIONS[pi])) for pi in pats
                for blk in range(1, seq // (DILATIONS[pi] * BLK)) for c in range(DILATIONS[pi])]
        for group, with_prev in ((first, False), (rest, True)):
            for i0 in range(0, len(group), per_group):
                units(group[i0:i0 + per_group], with_prev)


def _attn_prompt(slopes, q, k, v, batch, seq):
    blk = pl.BlockSpec((seq, 2 * HEAD_DIM), lambda b, h: (b, h))
    return pl.pallas_call(
        _attn_prompt_kernel,
        grid=(batch, N_HEADS // 2),
        in_specs=[pl.BlockSpec(memory_space=pltpu.SMEM), blk, blk, blk],
        out_specs=blk,
        out_shape=jax.ShapeDtypeStruct((batch * seq, ATTN_WIDTH), BF16),
        scratch_shapes=[
            pltpu.VMEM((len(DILATIONS), 2 * BLK, 2 * BLK), F32),
            pltpu.VMEM((len(DILATIONS) - 1, seq, 2 * HEAD_DIM), F32),
            pltpu.VMEM((len(DILATIONS) - 1, seq, 2 * HEAD_DIM), F32),
            pltpu.VMEM((len(DILATIONS) - 1, seq, 2 * HEAD_DIM), F32),
        ],
        compiler_params=_params(("arbitrary", "arbitrary")),
        name="attn_prompt",
    )(slopes, q, k, v)


N_NEW = 4
QROWS = N_NEW * N_HEADS


SAMPLE_TCHUNK = 512


def _attn_sample_tables(slope_ref, bias_s, mult_s):
    wb = bias_s.shape[1]
    n_of_row = lax.broadcasted_iota(jnp.int32, (QROWS, 1), 0) // N_HEADS
    t = lax.broadcasted_iota(jnp.int32, (QROWS, wb), 1)
    dist = wb + n_of_row - t
    mult = jnp.zeros((QROWS, wb), F32)
    for r in DILATIONS:
        hit = ((dist & (r - 1)) == 0) & (dist <= r * N_STRIDED)
        mult = mult + jnp.where(hit, 1.0, 0.0)
    mult_s[...] = mult
    bias_s[...] = jnp.where(mult > 0.0, (-LOG2E) * slope_ref[...] * dist.astype(F32), NEG)


def _attn_sample_step(slope_ref, q_ref, kn_ref, vn_ref, kt_ref, vt_ref, o_ref, bias_s, mult_s):
    W = ATTN_WIDTH
    wb = kt_ref.shape[2]
    chunks = [slice(c, c + SAMPLE_TCHUNK) for c in range(0, wb, SAMPLE_TCHUNK)]
    slope = slope_ref[...]
    n_of_row = lax.broadcasted_iota(jnp.int32, (QROWS, 1), 0) // N_HEADS
    q = q_ref[0]
    kn = kn_ref[0]
    vn = vn_ref[0]
    sub = lax.broadcasted_iota(jnp.int32, (N_HEADS, W), 0)
    own = sub == lax.broadcasted_iota(jnp.int32, (N_HEADS, W), 1) // HEAD_DIM
    own_rows = jnp.concatenate([own] * N_NEW, axis=0)
    qbd = jnp.concatenate(
        [jnp.where(own, jnp.broadcast_to(q[n:n + 1], (N_HEADS, W)), 0.0) for n in range(N_NEW)], axis=0)
    qbd16 = qbd.astype(BF16)

    ss = [jnp.dot(qbd16, kt_ref[0, :, c].astype(BF16), preferred_element_type=F32) + bias_s[:, c]
          for c in chunks]
    s_new, w_new = [], []
    for m in range(N_NEW):
        sm = jnp.sum(qbd * kn[m:m + 1], axis=-1, keepdims=True)
        gap = n_of_row - m
        s_new.append(jnp.where(gap >= 0, sm - LOG2E * slope * gap.astype(F32), NEG))
        w_new.append(jnp.where(gap == 0, float(len(DILATIONS)), jnp.where(gap > 0, 1.0, 0.0)))
    mx = s_new[0]
    for sm in s_new[1:] + [jnp.max(s, axis=-1, keepdims=True) for s in ss]:
        mx = jnp.maximum(mx, sm)
    den = jnp.zeros((QROWS, 1), F32)
    pv = jnp.zeros((QROWS, W), F32)
    for s, c in zip(ss, chunks):
        p = jnp.exp2(s - mx) * mult_s[:, c]
        den = den + jnp.sum(p, axis=-1, keepdims=True)
        pv = pv + lax.dot_general(p.astype(BF16), vt_ref[0, :, c].astype(BF16), (((1,), (1,)), ((), ())),
                                  preferred_element_type=F32)
    for m in range(N_NEW):
        pm = jnp.exp2(s_new[m] - mx) * w_new[m]
        den = den + pm
        pv = pv + pm * vn[m:m + 1]
    full = jnp.where(own_rows, pv / den, 0.0)
    o_ref[0] = jnp.sum(full.reshape(N_NEW, N_HEADS, W), axis=1).astype(o_ref.dtype)


def _pool_mix(d_groups, wp_ref, ps_ref):
    mixed = [jnp.dot(d.astype(BF16), wp_ref[g], preferred_element_type=F32) for g, d in enumerate(d_groups)]
    return jnp.concatenate(mixed, axis=-1) * ps_ref[...]


def _out_proj(x, attn, pool_out, wo_ref):
    h = x + jnp.dot(attn, wo_ref[:ATTN_WIDTH, :], preferred_element_type=F32)
    return h + jnp.dot(pool_out.astype(BF16), wo_ref[ATTN_WIDTH:, :], preferred_element_type=F32)


def _mix_prompt_tile(x_ref, a_ref, u_ref, wp_ref, ps_ref, wo_ref, e_s, s2_s, s4_s, s8_s, tiles_per_seq):
    i = pl.program_id(0)
    tm = u_ref.shape[0]
    H = POOL_HEAD
    G = POOL_GROUP

    @pl.when(i == 0)
    def _():
        e_s[0:8, :] = jnp.zeros((8, POOL_WIDTH), F32)
        s2_s[0:8, :] = jnp.zeros((8, POOL_WIDTH), F32)
        s4_s[0:8, :] = jnp.zeros((8, 3 * G), F32)
        s8_s[0:8, :] = jnp.zeros((8, 2 * G), F32)

    @pl.when(i % tiles_per_seq == 0)
    def _():
        e_s[8:H, :] = jnp.zeros((H - 8, POOL_WIDTH), F32)

    u = u_ref[...]
    e_s[H:, :] = u
    n = tm + H - 8
    s2_s[8:, :] = e_s[8:, :] + e_s[7:7 + n, :]
    s4_s[8:, :] = s2_s[8:, G:] + s2_s[6:6 + n, G:]
    s8_s[8:, :] = s4_s[8:, G:] + s4_s[4:4 + n, G:]
    sums = [
        s2_s[H:, :G],
        s4_s[H:, :G],
        s8_s[H:, :G],
        s8_s[H:, G:] + s8_s[H - 8:H - 8 + tm, G:],
    ]
    pos = (i % tiles_per_seq) * tm + lax.broadcasted_iota(jnp.int32, (tm, 1), 0)
    d = []
    for g, w in enumerate(POOL_WINDOWS):
        cnt = jnp.minimum(w, pos + 1).astype(F32)
        d.append(sums[g] / cnt - u[:, g * G:(g + 1) * G])
    pool_out = _pool_mix(d, wp_ref, ps_ref)
    h = _out_proj(x_ref[...], a_ref[...], pool_out, wo_ref)
    e_s[8:H, :] = e_s[tm + 8:tm + H, :]
    return h


def _mix_sample_kernel(x_ref, a_ref, u_ref, st_ref, wp_ref, ps_ref, wo_ref, h_ref):
    G = POOL_GROUP
    PW = POOL_WIDTH

    def row(t, lanes):
        if t < POOL_HIST:
            return st_ref[t, :, lanes]
        return u_ref[:, (t - POOL_HIST) * PW + lanes.start:(t - POOL_HIST) * PW + lanes.stop]

    for n in range(N_NEW):
        t = POOL_HIST + n
        d = []
        for g, w in enumerate(POOL_WINDOWS):
            lanes = slice(g * G, (g + 1) * G)
            cur = row(t, lanes)
            tot = cur
            for j in range(1, w):
                tot = tot + row(t - j, lanes)
            d.append(tot / float(min(w, PAST_LEN + n + 1)) - cur)
        pool_out = _pool_mix(d, wp_ref, ps_ref)
        cols = slice(n * D_MODEL, (n + 1) * D_MODEL)
        attn = a_ref[:, n * ATTN_WIDTH:(n + 1) * ATTN_WIDTH]
        h_ref[:, cols] = _out_proj(x_ref[:, cols], attn, pool_out, wo_ref)


def _mix_sample(x, attn, u, state_pool, w_pool, pool_scale, w_out):
    nseq = x.shape[0]
    full = lambda a: pl.BlockSpec(a.shape, lambda i: (0,) * a.ndim)
    args = (x, attn, u, state_pool, w_pool, pool_scale, w_out)
    return pl.pallas_call(
        _mix_sample_kernel,
        grid=(1,),
        in_specs=[full(a) for a in args],
        out_specs=pl.BlockSpec((nseq, N_NEW * D_MODEL), lambda i: (0, 0)),
        out_shape=jax.ShapeDtypeStruct((nseq, N_NEW * D_MODEL), F32),
        compiler_params=_params(("arbitrary",)),
        name="mix_sample",
    )(*args)


def _rms(h, g_ref):
    return (h * lax.rsqrt(jnp.mean(h * h, axis=-1, keepdims=True) + EPS) * g_ref[...]).astype(BF16)


def _silu_gate(gate, val):
    return (gate / (1.0 + jnp.exp(-gate)) * val).astype(BF16)


def _tail_prompt_kernel(x_ref, a_ref, u_ref, wp_ref, ps_ref, wo_ref, g_ref, wu_ref, cw_ref, cb_ref, wd_ref,
                        slope_ref, qs_ref, kn_ref, vn_ref, kt_ref, vt_ref,
                        y_ref, hist_ref, os_ref,
                        e_s, s2_s, s4_s, s8_s, h_s, ext_s, carry_s, act_s, bias_s, mult_s, *, tiles_per_seq):
    i = pl.program_id(0)
    tm = x_ref.shape[0]
    C = FF_CHUNK
    K1 = CONV_WIDTH - 1
    nr = tm // SUBLANES
    n_lane_blocks = D_MODEL // LANES

    @pl.when(i == 0)
    def _():
        _attn_sample_tables(slope_ref, bias_s, mult_s)

    @pl.when(i % tiles_per_seq == 0)
    def _():
        carry_s[...] = jnp.zeros(carry_s.shape, F32)

    h = _mix_prompt_tile(x_ref, a_ref, u_ref, wp_ref, ps_ref, wo_ref, e_s, s2_s, s4_s, s8_s, tiles_per_seq)
    for s in range(SUBLANES):
        for lb in range(n_lane_blocks):
            h_s[lb, pl.ds(s, nr, stride=SUBLANES), :] = h[s * nr:(s + 1) * nr, lb * LANES:(lb + 1) * LANES]
    _attn_sample_step(slope_ref, qs_ref, kn_ref, vn_ref, kt_ref, vt_ref, os_ref, bias_s, mult_s)
    hn = _rms(jnp.concatenate([h_s[lb] for lb in range(n_lane_blocks)], axis=1), g_ref)
    first_sub = lax.broadcasted_iota(jnp.int32, (SUBLANES, C), 0) == 0
    for c in range(N_FF_CHUNKS):
        halves = []
        for half in range(2):
            cols = slice(half * D_FF + c * C, half * D_FF + (c + 1) * C)
            up = jnp.dot(hn, wu_ref[:, cols], preferred_element_type=F32)
            ext = ext_s.at[half]
            for j in range(K1):
                rows = slice(j * SUBLANES, (j + 1) * SUBLANES)
                cur = up[(nr - K1 + j) * SUBLANES:(nr - K1 + j + 1) * SUBLANES]
                ext[rows, :] = jnp.where(first_sub, pltpu.roll(carry_s[rows, cols], 1, 0),
                                         pltpu.roll(cur, 1, 0))
            ext[K1 * SUBLANES:, :] = up
            conv = cb_ref[:, cols] + sum(
                ext[j * SUBLANES:j * SUBLANES + tm, :] * cw_ref[j:j + 1, cols] for j in range(CONV_WIDTH))
            carry_s[:, cols] = up[tm - K1 * SUBLANES:]
            halves.append(conv)
        act_s[:, c * C:(c + 1) * C] = _silu_gate(halves[0], halves[1])
    y = (jnp.concatenate([h_s[lb] for lb in range(n_lane_blocks)], axis=1)
         + jnp.dot(act_s[...], wd_ref[...], preferred_element_type=F32))
    for lb in range(n_lane_blocks):
        h_s[lb] = y[:, lb * LANES:(lb + 1) * LANES]
    for s in range(SUBLANES):
        for lb in range(n_lane_blocks):
            y_ref[s * nr:(s + 1) * nr, lb * LANES:(lb + 1) * LANES] = h_s[lb, pl.ds(s, nr, stride=SUBLANES), :]
    hist_ref[0] = carry_s[...]


def _tail_prompt(x2d, attn, u, w_pool, pool_scale, w_out, g_ffn, w_up, conv_w, conv_b, w_down, seq,
                 slope_rows, q_s, kn_s, vn_s, cache_kt, cache_vt):
    n_tok = x2d.shape[0]
    nseq, W, wb = cache_kt.shape
    tm = n_tok // nseq
    assert tm * nseq == n_tok and tm % 8 == 0 and seq % tm == 0
    tps = seq // tm
    G = POOL_GROUP
    pool_rows = tm + POOL_HEAD
    new_spec = pl.BlockSpec((1, N_NEW, W), lambda i: (i, 0, 0))
    cache_spec = pl.BlockSpec((1, W, wb), lambda i: (i, 0, 0))
    return pl.pallas_call(
        functools.partial(_tail_prompt_kernel, tiles_per_seq=tps),
        grid=(nseq,),
        in_specs=[
            pl.BlockSpec((tm, D_MODEL), lambda i: (i, 0)),
            pl.BlockSpec((tm, ATTN_WIDTH), lambda i: (i, 0)),
            pl.BlockSpec((tm, POOL_WIDTH), lambda i: (i, 0)),
            _const_spec((len(POOL_WINDOWS), G, G)),
            _const_spec((1, POOL_WIDTH)),
            _const_spec((D_MODEL, D_MODEL)),
            _const_spec((1, D_MODEL)),
            _const_spec((D_MODEL, 2 * D_FF)),
            _const_spec((CONV_WIDTH, 2 * D_FF)),
            _const_spec((1, 2 * D_FF)),
            _const_spec((D_FF, D_MODEL)),
            _const_spec((QROWS, 1)),
            new_spec, new_spec, new_spec, cache_spec, cache_spec,
        ],
        out_specs=[
            pl.BlockSpec((tm, D_MODEL), lambda i: (i, 0)),
            pl.BlockSpec((1, CONV_HEAD, 2 * D_FF), lambda i: (i // tps, 0, 0)),
            new_spec,
        ],
        out_shape=[
            jax.ShapeDtypeStruct((n_tok, D_MODEL), F32),
            jax.ShapeDtypeStruct((n_tok // seq, CONV_HEAD, 2 * D_FF), F32),
            jax.ShapeDtypeStruct((nseq, N_NEW, W), BF16),
        ],
        scratch_shapes=[
            pltpu.VMEM((pool_rows, POOL_WIDTH), F32),
            pltpu.VMEM((pool_rows, POOL_WIDTH), F32),
            pltpu.VMEM((pool_rows, 3 * G), F32),
            pltpu.VMEM((pool_rows, 2 * G), F32),
            pltpu.VMEM((D_MODEL // LANES, tm, LANES), F32),
            pltpu.VMEM((2, tm + CONV_HEAD, FF_CHUNK), F32),
            pltpu.VMEM((CONV_HEAD, 2 * D_FF), F32),
            pltpu.VMEM((tm, D_FF), BF16),
            pltpu.VMEM((QROWS, wb), F32),
            pltpu.VMEM((QROWS, wb), F32),
        ],
        compiler_params=_params(("arbitrary",)),
        name="tail_prompt",
    )(x2d, attn, u, w_pool, pool_scale, w_out, g_ffn, w_up, conv_w, conv_b, w_down,
      slope_rows, q_s, kn_s, vn_s, cache_kt, cache_vt)


def _ffn_sample_kernel(h_ref, st_ref, g_ref, wu_ref, cw_ref, cb_ref, wd_ref, y_ref, hist_ref):
    nseq = h_ref.shape[0]
    C = FF_CHUNK
    K = CONV_WIDTH
    hs = [h_ref[:, n * D_MODEL:(n + 1) * D_MODEL] for n in range(N_NEW)]
    hn = jnp.concatenate([_rms(h, g_ref) for h in hs], axis=0)
    for n in range(N_NEW):
        y_ref[:, n * D_MODEL:(n + 1) * D_MODEL] = hs[n]
    for c in range(N_FF_CHUNKS):
        halves = []
        for half in range(2):
            cols = slice(half * D_FF + c * C, half * D_FF + (c + 1) * C)
            up = jnp.dot(hn, wu_ref[:, cols], preferred_element_type=F32)
            rows = [st_ref[:, t * 2 * D_FF + cols.start:t * 2 * D_FF + cols.stop] for t in range(K - 1)]
            rows += [up[n * nseq:(n + 1) * nseq] for n in range(N_NEW)]
            conv = [cb_ref[:, cols] + sum(rows[n + j] * cw_ref[j:j + 1, cols] for j in range(K))
                    for n in range(N_NEW)]
            for t in range(K - 1):
                hist_ref[:, t * 2 * D_FF + cols.start:t * 2 * D_FF + cols.stop] = rows[N_NEW + t]
            halves.append(jnp.concatenate(conv, axis=0))
        act = _silu_gate(halves[0], halves[1])
        out = jnp.dot(act, wd_ref[c * C:(c + 1) * C, :], preferred_element_type=F32)
        for n in range(N_NEW):
            y_ref[:, n * D_MODEL:(n + 1) * D_MODEL] += out[n * nseq:(n + 1) * nseq]


def _ffn_sample(h, state_ffn, g_ffn, w_up, conv_w, conv_b, w_down):
    nseq = h.shape[0]
    full = lambda a: pl.BlockSpec(a.shape, lambda i: (0,) * a.ndim, pipeline_mode=pl.Buffered(1))
    args = (h, state_ffn, g_ffn, w_up, conv_w, conv_b, w_down)
    return pl.pallas_call(
        _ffn_sample_kernel,
        grid=(1,),
        in_specs=[full(a) for a in args],
        out_specs=[
            pl.BlockSpec((nseq, N_NEW * D_MODEL), lambda i: (0, 0)),
            pl.BlockSpec((nseq, (CONV_WIDTH - 1) * 2 * D_FF), lambda i: (0, 0)),
        ],
        out_shape=[
            jax.ShapeDtypeStruct((nseq, N_NEW * D_MODEL), F32),
            jax.ShapeDtypeStruct((nseq, (CONV_WIDTH - 1) * 2 * D_FF), F32),
        ],
        compiler_params=_params(("arbitrary",)),
        name="ffn_sample",
    )(*args)


def kernel(x_prompt, x_sample, cache_k, cache_v, state_pool, state_ffn_conv, g_attn_norm, w_in, g_q, g_k,
           w_pool, pool_scale, w_out, g_ffn_norm, w_up, conv_w, conv_b, w_down):
    depth = w_in.shape[0]
    assert depth == 1
    batch, seq, _ = x_prompt.shape
    nseq, n_new, _ = x_sample.shape
    assert n_new == N_NEW and seq % (16 * BLK) == 0 and seq % TM == 0
    assert cache_k.shape[2] == 16 * N_STRIDED

    slopes = jnp.asarray(2.0 ** (-8.0 * np.arange(1, N_HEADS + 1) / N_HEADS), dtype=F32)
    head_of_lane = np.arange(ATTN_WIDTH) // HEAD_DIM
    head_mean = jnp.asarray((head_of_lane[:, None] == head_of_lane[None, :]) / HEAD_DIM, dtype=BF16)

    l = 0
    w_in_b = w_in[l].astype(BF16)
    w_pool_b = w_pool[l].astype(BF16)
    w_out_b = w_out[l].astype(BF16)
    w_up_b = w_up[l].astype(BF16)
    w_down_b = w_down[l].astype(BF16)
    g_attn = g_attn_norm[l].reshape(1, D_MODEL)
    g_ffn = g_ffn_norm[l].reshape(1, D_MODEL)
    gq = g_q[l].reshape(1, ATTN_WIDTH)
    gk = g_k[l].reshape(1, ATTN_WIDTH)
    ps = pool_scale[l].reshape(1, POOL_WIDTH)
    cb = conv_b[l].reshape(1, 2 * D_FF)
    cw = conv_w[l]

    xs = x_sample.reshape(nseq, N_NEW * D_MODEL)
    qs, ks, vs, us = _proj(xs, nseq, N_NEW, lambda n: (0, n), lambda n: (0, n), (nseq, N_NEW * ATTN_WIDTH),
                           g_attn, w_in_b, gq, gk, head_mean)
    per_seq = lambda a: a.reshape(nseq, N_NEW, ATTN_WIDTH)
    cache_kt = jnp.transpose(cache_k[l], (0, 2, 3, 1)).reshape(nseq, ATTN_WIDTH, -1)
    cache_vt = jnp.transpose(cache_v[l], (0, 2, 3, 1)).reshape(nseq, ATTN_WIDTH, -1)
    slope_rows = jnp.tile(slopes, N_NEW).reshape(QROWS, 1)

    n_tok = batch * seq
    xp = x_prompt.reshape(n_tok, D_MODEL)
    qp, kp, vp, up, kpt, vpt = _proj(xp, TM, n_tok // TM, lambda i: (i, 0), lambda i: (i, 0),
                                     (n_tok, ATTN_WIDTH), g_attn, w_in_b, gq, gk, head_mean, seq=seq)
    attn_p = _attn_prompt(slopes, qp, kp, vp, batch, seq)
    yp, hist_p, attn_s = _tail_prompt(xp, attn_p, up, w_pool_b, ps, w_out_b, g_ffn, w_up_b, cw, cb, w_down_b,
                                      seq, slope_rows, per_seq(qs), per_seq(ks), per_seq(vs), cache_kt, cache_vt)

    attn_s = attn_s.reshape(nseq, N_NEW * ATTN_WIDTH)
    st_pool = jnp.transpose(state_pool[l], (1, 0, 2))
    hs = _mix_sample(xs, attn_s, us, st_pool, w_pool_b, ps, w_out_b)
    st_ffn = state_ffn_conv[l].reshape(nseq, (CONV_WIDTH - 1) * 2 * D_FF)
    ys, hist_s = _ffn_sample(hs, st_ffn, g_ffn, w_up_b, cw, cb, w_down_b)

    n_keep = min(16 * N_STRIDED, seq)
    window = lambda t: jnp.transpose(
        t.reshape(batch, N_HEADS, HEAD_DIM, seq), (0, 3, 1, 2))[None, :, seq - n_keep:]
    kp5, vp5 = window(kpt), window(vpt)
    new_pool_p = up.reshape(batch, seq, POOL_WIDTH)[None, :, seq - POOL_HIST:]
    new_ffn_p = hist_p[None, :, SUBLANES - 1::SUBLANES]
    u_time = jnp.transpose(us.reshape(nseq, N_NEW, POOL_WIDTH), (1, 0, 2))
    new_pool_s = jnp.transpose(jnp.concatenate([st_pool[N_NEW:], u_time], axis=0), (1, 0, 2))[None]
    return (
        yp.reshape(batch, seq, D_MODEL),
        ys.reshape(nseq, N_NEW, D_MODEL),
        kp5, vp5, new_pool_p, new_ffn_p,
        ks.reshape(1, nseq, N_NEW, N_HEADS, HEAD_DIM),
        vs.reshape(1, nseq, N_NEW, N_HEADS, HEAD_DIM),
        new_pool_s,
        hist_s.reshape(1, nseq, CONV_WIDTH - 1, 2 * D_FF),
    )
```

```python
import functools

import numpy as np
import jax
import jax.numpy as jnp
from jax import lax
from jax.experimental import pallas as pl
from jax.experimental.pallas import tpu as pltpu

D_MODEL = 1024
HEAD_DIM = 64
N_HEADS = 8
ATTN_WIDTH = N_HEADS * HEAD_DIM
POOL_WIDTH = 512
POOL_WINDOWS = (2, 4, 8, 16)
POOL_GROUP = 128
POOL_HIST = 15
PROJ_WIDTH = 3 * ATTN_WIDTH + POOL_WIDTH
DILATIONS = (1, 4, 16)
N_STRIDED = 128
BLK = 128
D_FF = 2816
CONV_WIDTH = 3
PAST_LEN = 8192
EPS = 1e-6
NEG = -1e30
LOG2E = 1.4426950408889634

F32 = jnp.float32
BF16 = jnp.bfloat16

TM = 1024
ATTN_MERGE_UNITS = 8
ATTN_UNITS = 8
FF_CHUNK = 256
N_FF_CHUNKS = D_FF // FF_CHUNK
TAIL_FF_CHUNK = 256
POOL_HEAD = 24
SUBLANES = 8
LANES = 128
CONV_HEAD = (CONV_WIDTH - 1) * SUBLANES
VMEM_LIMIT = 56 * 1024 * 1024


def _params(sem, vmem=VMEM_LIMIT):
    return pltpu.CompilerParams(dimension_semantics=sem, vmem_limit_bytes=vmem)


def _const_spec(shape):
    nd = len(shape)
    return pl.BlockSpec(shape, lambda *_: (0,) * nd, pipeline_mode=pl.Buffered(1))


def _proj_kernel(x_ref, g_ref, w_ref, gq_ref, gk_ref, hm_ref, q_ref, k_ref, v_ref, u_ref, *kvt_refs):
    x = x_ref[...]
    xn = x * lax.rsqrt(jnp.mean(x * x, axis=-1, keepdims=True) + EPS) * g_ref[...]
    proj = jnp.dot(xn.astype(BF16), w_ref[...], preferred_element_type=F32)

    def head_norm(t, g):
        ms = jnp.dot((t * t).astype(BF16), hm_ref[...], preferred_element_type=F32)
        return t * lax.rsqrt(ms + EPS) * g

    q = head_norm(proj[:, :ATTN_WIDTH], gq_ref[...])
    q_ref[...] = q * (HEAD_DIM ** -0.5 * LOG2E)
    k = head_norm(proj[:, ATTN_WIDTH:2 * ATTN_WIDTH], gk_ref[...])
    v = proj[:, 2 * ATTN_WIDTH:3 * ATTN_WIDTH]
    k_ref[...] = k
    v_ref[...] = v
    u_ref[...] = proj[:, 3 * ATTN_WIDTH:]
    if kvt_refs:
        kvt_refs[0][0] = k.T
        kvt_refs[1][0] = v.T


def _proj(x2d, rows, n_steps, x_map, o_map, out_rows, g_attn, w_in, gq, gk, head_mean, seq=None):
    out = jax.ShapeDtypeStruct((out_rows[0], out_rows[1]), F32)
    out_specs = [pl.BlockSpec((rows, ATTN_WIDTH), o_map)] * 4
    out_shape = [out] * 4
    if seq is not None:
        tps = seq // rows
        out_specs += [pl.BlockSpec((1, ATTN_WIDTH, rows), lambda i: (i // tps, 0, i % tps))] * 2
        out_shape += [jax.ShapeDtypeStruct((out_rows[0] // seq, ATTN_WIDTH, seq), F32)] * 2
    return pl.pallas_call(
        _proj_kernel,
        grid=(n_steps,),
        in_specs=[
            pl.BlockSpec((rows, D_MODEL), x_map),
            _const_spec((1, D_MODEL)),
            _const_spec((D_MODEL, PROJ_WIDTH)),
            _const_spec((1, ATTN_WIDTH)),
            _const_spec((1, ATTN_WIDTH)),
            _const_spec((ATTN_WIDTH, ATTN_WIDTH)),
        ],
        out_specs=out_specs,
        out_shape=out_shape,
        compiler_params=_params(("arbitrary",)),
        name="proj",
    )(x2d, g_attn, w_in, gq, gk, head_mean)


def _attn_prompt_kernel(slopes_ref, q_ref, k_ref, v_ref, o_ref, bias_s, acc_s, m_s, l_s):
    hp = pl.program_id(1)
    seq = q_ref.shape[0]
    lane = lax.broadcasted_iota(jnp.int32, (BLK, 2 * HEAD_DIM), 1)
    left = lane < HEAD_DIM

    row = lax.broadcasted_iota(jnp.int32, (2 * BLK, 2 * BLK), 0)
    col = lax.broadcasted_iota(jnp.int32, (2 * BLK, 2 * BLK), 1)
    diff = (row & (BLK - 1)) + BLK - col
    valid = (diff >= 0) & (diff <= N_STRIDED)
    slope = jnp.where(row < BLK, slopes_ref[2 * hp], slopes_ref[2 * hp + 1])
    for pi, r in enumerate(DILATIONS):
        bias_s[pi] = jnp.where(valid, (-LOG2E) * slope * (r * diff).astype(F32), NEG)

    def rows_of(start, r, n=BLK):
        return pl.ds(start, n) if r == 1 else pl.ds(start, n, stride=r)

    ones_cols = jnp.ones((2 * BLK, 2 * HEAD_DIM), BF16)

    def units(specs, with_prev):
        curs = [rows_of(st, r) for _, r, st in specs]
        qqs, kks, vvs = [], [], []
        for (_, r, st), cur in zip(specs, curs):
            q2 = q_ref[cur, :]
            qqs.append(
                jnp.concatenate([jnp.where(left, q2, 0.0), jnp.where(left, 0.0, q2)], axis=0).astype(BF16))
            if not with_prev:
                kk, vv = k_ref[cur, :], v_ref[cur, :]
            elif r == 1:
                both = pl.ds(st - BLK, 2 * BLK)
                kk, vv = k_ref[both, :], v_ref[both, :]
            else:
                prev = rows_of(st - BLK * r, r)
                kk = jnp.concatenate([k_ref[prev, :], k_ref[cur, :]], axis=0)
                vv = jnp.concatenate([v_ref[prev, :], v_ref[cur, :]], axis=0)
            kks.append(kk.astype(BF16))
            vvs.append(jnp.concatenate([vv.astype(BF16), ones_cols[:vv.shape[0]]], axis=1))
        ss = [lax.dot_general(qq, kk, (((1,), (1,)), ((), ())), preferred_element_type=F32)
              + (bias_s[pi] if with_prev else bias_s[pi, :, BLK:])
              for (pi, _, _), qq, kk in zip(specs, qqs, kks)]
        ms = [jnp.max(s, axis=-1, keepdims=True) for s in ss]
        ps = [jnp.exp2(s - m).astype(BF16) for s, m in zip(ss, ms)]
        pvs = [jnp.dot(p, vv, preferred_element_type=F32) for p, vv in zip(ps, vvs)]
        W2 = 2 * HEAD_DIM
        for (pi, _, _), cur, pv, m in zip(specs, curs, pvs, ms):
            acc = jnp.where(left, pv[:BLK, :W2], pv[BLK:, :W2])
            mt = jnp.where(left, m[:BLK], m[BLK:])
            lt = jnp.where(left, pv[:BLK, W2:], pv[BLK:, W2:])
            if pi > 0:
                acc_s[pi - 1, cur, :] = acc
                m_s[pi - 1, cur, :] = mt
                l_s[pi - 1, cur, :] = lt
                continue
            parts = [(acc, mt, lt)] + [(acc_s[j, cur, :], m_s[j, cur, :], l_s[j, cur, :])
                                       for j in range(len(DILATIONS) - 1)]
            mx = parts[0][1]
            for _, mj, _ in parts[1:]:
                mx = jnp.maximum(mx, mj)
            num = jnp.zeros((BLK, W2), F32)
            den = jnp.zeros((BLK, W2), F32)
            for aj, mj, lj in parts:
                w = jnp.exp2(mj - mx)
                num = num + w * aj
                den = den + w * lj
            o_ref[cur, :] = (num / den).astype(o_ref.dtype)

    U = ATTN_UNITS
    assert DILATIONS[0] == 1
    for pats, per_group in ((range(1, len(DILATIONS)), U), ((0,), ATTN_MERGE_UNITS)):
        first = [(pi, DILATIONS[pi], c) for pi in pats for c in range(DILATIONS[pi])]
        rest = [(pi, DILATIONS[pi], c + blk * (BLK * DILATIONS[pi])) for pi in pats
                for blk in range(1, seq // (DILATIONS[pi] * BLK)) for c in range(DILATIONS[pi])]
        for group, with_prev in ((first, False), (rest, True)):
            for i0 in range(0, len(group), per_group):
                units(group[i0:i0 + per_group], with_prev)


def _attn_prompt(slopes, q, k, v, batch, seq):
    blk = pl.BlockSpec((seq, 2 * HEAD_DIM), lambda b, h: (b, h))
    return pl.pallas_call(
        _attn_prompt_kernel,
        grid=(batch, N_HEADS // 2),
        in_specs=[pl.BlockSpec(memory_space=pltpu.SMEM), blk, blk, blk],
        out_specs=blk,
        out_shape=jax.ShapeDtypeStruct((batch * seq, ATTN_WIDTH), BF16),
        scratch_shapes=[
            pltpu.VMEM((len(DILATIONS), 2 * BLK, 2 * BLK), F32),
            pltpu.VMEM((len(DILATIONS) - 1, seq, 2 * HEAD_DIM), F32),
            pltpu.VMEM((len(DILATIONS) - 1, seq, 2 * HEAD_DIM), F32),
            pltpu.VMEM((len(DILATIONS) - 1, seq, 2 * HEAD_DIM), F32),
        ],
        compiler_params=_params(("arbitrary", "arbitrary")),
        name="attn_prompt",
    )(slopes, q, k, v)


N_NEW = 4
QROWS = N_NEW * N_HEADS


SAMPLE_TCHUNK = 512


def _attn_sample_tables(slope_ref, bias_s, mult_s):
    wb = bias_s.shape[1]
    n_of_row = lax.broadcasted_iota(jnp.int32, (QROWS, 1), 0) // N_HEADS
    t = lax.broadcasted_iota(jnp.int32, (QROWS, wb), 1)
    dist = wb + n_of_row - t
    mult = jnp.zeros((QROWS, wb), F32)
    for r in DILATIONS:
        hit = ((dist & (r - 1)) == 0) & (dist <= r * N_STRIDED)
        mult = mult + jnp.where(hit, 1.0, 0.0)
    mult_s[...] = mult
    bias_s[...] = jnp.where(mult > 0.0, (-LOG2E) * slope_ref[...] * dist.astype(F32), NEG)


def _attn_sample_step(slope_ref, q_ref, kn_ref, vn_ref, kt_ref, vt_ref, o_ref, bias_s, mult_s):
    W = ATTN_WIDTH
    wb = kt_ref.shape[2]
    chunks = [slice(c, c + SAMPLE_TCHUNK) for c in range(0, wb, SAMPLE_TCHUNK)]
    slope = slope_ref[...]
    n_of_row = lax.broadcasted_iota(jnp.int32, (QROWS, 1), 0) // N_HEADS
    q = q_ref[0]
    kn = kn_ref[0]
    vn = vn_ref[0]
    sub = lax.broadcasted_iota(jnp.int32, (N_HEADS, W), 0)
    own = sub == lax.broadcasted_iota(jnp.int32, (N_HEADS, W), 1) // HEAD_DIM
    own_rows = jnp.concatenate([own] * N_NEW, axis=0)
    qbd = jnp.concatenate(
        [jnp.where(own, jnp.broadcast_to(q[n:n + 1], (N_HEADS, W)), 0.0) for n in range(N_NEW)], axis=0)
    qbd16 = qbd.astype(BF16)

    ss = [jnp.dot(qbd16, kt_ref[0, :, c].astype(BF16), preferred_element_type=F32) + bias_s[:, c]
          for c in chunks]
    s_new, w_new = [], []
    for m in range(N_NEW):
        sm = jnp.sum(qbd * kn[m:m + 1], axis=-1, keepdims=True)
        gap = n_of_row - m
        s_new.append(jnp.where(gap >= 0, sm - LOG2E * slope * gap.astype(F32), NEG))
        w_new.append(jnp.where(gap == 0, float(len(DILATIONS)), jnp.where(gap > 0, 1.0, 0.0)))
    mx = s_new[0]
    for sm in s_new[1:] + [jnp.max(s, axis=-1, keepdims=True) for s in ss]:
        mx = jnp.maximum(mx, sm)
    den = jnp.zeros((QROWS, 1), F32)
    pv = jnp.zeros((QROWS, W), F32)
    for s, c in zip(ss, chunks):
        p = jnp.exp2(s - mx) * mult_s[:, c]
        den = den + jnp.sum(p, axis=-1, keepdims=True)
        pv = pv + lax.dot_general(p.astype(BF16), vt_ref[0, :, c].astype(BF16), (((1,), (1,)), ((), ())),
                                  preferred_element_type=F32)
    for m in range(N_NEW):
        pm = jnp.exp2(s_new[m] - mx) * w_new[m]
        den = den + pm
        pv = pv + pm * vn[m:m + 1]
    full = jnp.where(own_rows, pv / den, 0.0)
    o_ref[0] = jnp.sum(full.reshape(N_NEW, N_HEADS, W), axis=1).astype(o_ref.dtype)


def _pool_mix(d_groups, wp_ref, ps_ref):
    mixed = [jnp.dot(d.astype(BF16), wp_ref[g], preferred_element_type=F32) for g, d in enumerate(d_groups)]
    return jnp.concatenate(mixed, axis=-1) * ps_ref[...]


def _out_proj(x, attn, pool_out, wo_ref):
    h = x + jnp.dot(attn, wo_ref[:ATTN_WIDTH, :], preferred_element_type=F32)
    return h + jnp.dot(pool_out.astype(BF16), wo_ref[ATTN_WIDTH:, :], preferred_element_type=F32)


def _mix_prompt_tile(x_ref, a_ref, u_ref, wp_ref, ps_ref, wo_ref, e_s, s2_s, s4_s, s8_s, tiles_per_seq):
    i = pl.program_id(0)
    tm = u_ref.shape[0]
    H = POOL_HEAD
    G = POOL_GROUP

    @pl.when(i == 0)
    def _():
        e_s[0:8, :] = jnp.zeros((8, POOL_WIDTH), F32)
        s2_s[0:8, :] = jnp.zeros((8, POOL_WIDTH), F32)
        s4_s[0:8, :] = jnp.zeros((8, 3 * G), F32)
        s8_s[0:8, :] = jnp.zeros((8, 2 * G), F32)

    @pl.when(i % tiles_per_seq == 0)
    def _():
        e_s[8:H, :] = jnp.zeros((H - 8, POOL_WIDTH), F32)

    u = u_ref[...]
    e_s[H:, :] = u
    n = tm + H - 8
    s2_s[8:, :] = e_s[8:, :] + e_s[7:7 + n, :]
    s4_s[8:, :] = s2_s[8:, G:] + s2_s[6:6 + n, G:]
    s8_s[8:, :] = s4_s[8:, G:] + s4_s[4:4 + n, G:]
    sums = [
        s2_s[H:, :G],
        s4_s[H:, :G],
        s8_s[H:, :G],
        s8_s[H:, G:] + s8_s[H - 8:H - 8 + tm, G:],
    ]
    pos = (i % tiles_per_seq) * tm + lax.broadcasted_iota(jnp.int32, (tm, 1), 0)
    d = []
    for g, w in enumerate(POOL_WINDOWS):
        cnt = jnp.minimum(w, pos + 1).astype(F32)
        d.append(sums[g] / cnt - u[:, g * G:(g + 1) * G])
    pool_out = _pool_mix(d, wp_ref, ps_ref)
    h = _out_proj(x_ref[...], a_ref[...], pool_out, wo_ref)
    e_s[8:H, :] = e_s[tm + 8:tm + H, :]
    return h


def _mix_sample_kernel(x_ref, a_ref, u_ref, st_ref, wp_ref, ps_ref, wo_ref, h_ref):
    G = POOL_GROUP
    PW = POOL_WIDTH

    def row(t, lanes):
        if t < POOL_HIST:
            return st_ref[t, :, lanes]
        return u_ref[:, (t - POOL_HIST) * PW + lanes.start:(t - POOL_HIST) * PW + lanes.stop]

    for n in range(N_NEW):
        t = POOL_HIST + n
        d = []
        for g, w in enumerate(POOL_WINDOWS):
            lanes = slice(g * G, (g + 1) * G)
            cur = row(t, lanes)
            tot = cur
            for j in range(1, w):
                tot = tot + row(t - j, lanes)
            d.append(tot / float(min(w, PAST_LEN + n + 1)) - cur)
        pool_out = _pool_mix(d, wp_ref, ps_ref)
        cols = slice(n * D_MODEL, (n + 1) * D_MODEL)
        attn = a_ref[:, n * ATTN_WIDTH:(n + 1) * ATTN_WIDTH]
        h_ref[:, cols] = _out_proj(x_ref[:, cols], attn, pool_out, wo_ref)


def _mix_sample(x, attn, u, state_pool, w_pool, pool_scale, w_out):
    nseq = x.shape[0]
    full = lambda a: pl.BlockSpec(a.shape, lambda i: (0,) * a.ndim)
    args = (x, attn, u, state_pool, w_pool, pool_scale, w_out)
    return pl.pallas_call(
        _mix_sample_kernel,
        grid=(1,),
        in_specs=[full(a) for a in args],
        out_specs=pl.BlockSpec((nseq, N_NEW * D_MODEL), lambda i: (0, 0)),
        out_shape=jax.ShapeDtypeStruct((nseq, N_NEW * D_MODEL), F32),
        compiler_params=_params(("arbitrary",)),
        name="mix_sample",
    )(*args)


def _rms(h, g_ref):
    return (h * lax.rsqrt(jnp.mean(h * h, axis=-1, keepdims=True) + EPS) * g_ref[...]).astype(BF16)


def _silu_gate(gate, val):
    return (gate / (1.0 + jnp.exp(-gate)) * val).astype(BF16)


def _tail_prompt_kernel(x_ref, a_ref, u_ref, wp_ref, ps_ref, wo_ref, g_ref, wu_ref, cw_ref, cb_ref, wd_ref,
                        slope_ref, qs_ref, kn_ref, vn_ref, kt_ref, vt_ref,
                        y_ref, hist_ref, os_ref,
                        e_s, s2_s, s4_s, s8_s, h_s, ext_s, carry_s, act_s, bias_s, mult_s, *, tiles_per_seq):
    i = pl.program_id(0)
    tm = x_ref.shape[0]
    C = TAIL_FF_CHUNK
    K1 = CONV_WIDTH - 1
    nr = tm // SUBLANES
    n_lane_blocks = D_MODEL // LANES

    @pl.when(i == 0)
    def _():
        _attn_sample_tables(slope_ref, bias_s, mult_s)

    @pl.when(i % tiles_per_seq == 0)
    def _():
        carry_s[...] = jnp.zeros(carry_s.shape, F32)

    h = _mix_prompt_tile(x_ref, a_ref, u_ref, wp_ref, ps_ref, wo_ref, e_s, s2_s, s4_s, s8_s, tiles_per_seq)
    for s in range(SUBLANES):
        for lb in range(n_lane_blocks):
            h_s[lb, pl.ds(s, nr, stride=SUBLANES), :] = h[s * nr:(s + 1) * nr, lb * LANES:(lb + 1) * LANES]
    _attn_sample_step(slope_ref, qs_ref, kn_ref, vn_ref, kt_ref, vt_ref, os_ref, bias_s, mult_s)
    hn = _rms(jnp.concatenate([h_s[lb] for lb in range(n_lane_blocks)], axis=1), g_ref)
    for c0 in range(0, D_FF, C):
        cw = min(C, D_FF - c0)
        first_sub = lax.broadcasted_iota(jnp.int32, (SUBLANES, cw), 0) == 0
        halves = []
        for half in range(2):
            cols = slice(half * D_FF + c0, half * D_FF + c0 + cw)
            up = jnp.dot(hn, wu_ref[:, cols], preferred_element_type=F32)
            ext = ext_s.at[half, :, :cw]
            for j in range(K1):
                rows = slice(j * SUBLANES, (j + 1) * SUBLANES)
                cur = up[(nr - K1 + j) * SUBLANES:(nr - K1 + j + 1) * SUBLANES]
                ext[rows, :] = jnp.where(first_sub, pltpu.roll(carry_s[rows, cols], 1, 0),
                                         pltpu.roll(cur, 1, 0))
            ext[K1 * SUBLANES:, :] = up
            conv = cb_ref[:, cols] + sum(
                ext[j * SUBLANES:j * SUBLANES + tm, :] * cw_ref[j:j + 1, cols] for j in range(CONV_WIDTH))
            carry_s[:, cols] = up[tm - K1 * SUBLANES:]
            halves.append(conv)
        act_s[:, c0:c0 + cw] = _silu_gate(halves[0], halves[1])
    y = (jnp.concatenate([h_s[lb] for lb in range(n_lane_blocks)], axis=1)
         + jnp.dot(act_s[...], wd_ref[...], preferred_element_type=F32))
    for lb in range(n_lane_blocks):
        h_s[lb] = y[:, lb * LANES:(lb + 1) * LANES]
    for s in range(SUBLANES):
        for lb in range(n_lane_blocks):
            y_ref[s * nr:(s + 1) * nr, lb * LANES:(lb + 1) * LANES] = h_s[lb, pl.ds(s, nr, stride=SUBLANES), :]

    @pl.when(i % tiles_per_seq == tiles_per_seq - 1)
    def _():
        hist_ref[0] = jnp.zeros(hist_ref.shape[1:], F32)
        for j in range(K1):
            hist_ref[0, j:j + 1, :] = carry_s[(j + 1) * SUBLANES - 1:(j + 1) * SUBLANES, :]


def _tail_prompt(x2d, attn, u, w_pool, pool_scale, w_out, g_ffn, w_up, conv_w, conv_b, w_down, seq,
                 slope_rows, q_s, kn_s, vn_s, cache_kt, cache_vt):
    n_tok = x2d.shape[0]
    nseq, W, wb = cache_kt.shape
    tm = n_tok // nseq
    assert tm * nseq == n_tok and tm % 8 == 0 and seq % tm == 0
    tps = seq // tm
    G = POOL_GROUP
    pool_rows = tm + POOL_HEAD
    new_spec = pl.BlockSpec((1, N_NEW, W), lambda i: (i, 0, 0))
    cache_spec = pl.BlockSpec((1, W, wb), lambda i: (i, 0, 0))
    return pl.pallas_call(
        functools.partial(_tail_prompt_kernel, tiles_per_seq=tps),
        grid=(nseq,),
        in_specs=[
            pl.BlockSpec((tm, D_MODEL), lambda i: (i, 0)),
            pl.BlockSpec((tm, ATTN_WIDTH), lambda i: (i, 0)),
            pl.BlockSpec((tm, POOL_WIDTH), lambda i: (i, 0)),
            _const_spec((len(POOL_WINDOWS), G, G)),
            _const_spec((1, POOL_WIDTH)),
            _const_spec((D_MODEL, D_MODEL)),
            _const_spec((1, D_MODEL)),
            _const_spec((D_MODEL, 2 * D_FF)),
            _const_spec((CONV_WIDTH, 2 * D_FF)),
            _const_spec((1, 2 * D_FF)),
            _const_spec((D_FF, D_MODEL)),
            _const_spec((QROWS, 1)),
            new_spec, new_spec, new_spec, cache_spec, cache_spec,
        ],
        out_specs=[
            pl.BlockSpec((tm, D_MODEL), lambda i: (i, 0)),
            pl.BlockSpec((1, SUBLANES, 2 * D_FF), lambda i: (i // tps, 0, 0)),
            new_spec,
        ],
        out_shape=[
            jax.ShapeDtypeStruct((n_tok, D_MODEL), F32),
            jax.ShapeDtypeStruct((n_tok // seq, SUBLANES, 2 * D_FF), F32),
            jax.ShapeDtypeStruct((nseq, N_NEW, W), BF16),
        ],
        scratch_shapes=[
            pltpu.VMEM((pool_rows, POOL_WIDTH), F32),
            pltpu.VMEM((pool_rows, POOL_WIDTH), F32),
            pltpu.VMEM((pool_rows, 3 * G), F32),
            pltpu.VMEM((pool_rows, 2 * G), F32),
            pltpu.VMEM((D_MODEL // LANES, tm, LANES), F32),
            pltpu.VMEM((2, tm + CONV_HEAD, TAIL_FF_CHUNK), F32),
            pltpu.VMEM((CONV_HEAD, 2 * D_FF), F32),
            pltpu.VMEM((tm, D_FF), BF16),
            pltpu.VMEM((QROWS, wb), F32),
            pltpu.VMEM((QROWS, wb), F32),
        ],
        compiler_params=_params(("arbitrary",)),
        name="tail_prompt",
    )(x2d, attn, u, w_pool, pool_scale, w_out, g_ffn, w_up, conv_w, conv_b, w_down,
      slope_rows, q_s, kn_s, vn_s, cache_kt, cache_vt)


def _ffn_sample_kernel(h_ref, st_ref, g_ref, wu_ref, cw_ref, cb_ref, wd_ref, y_ref, hist_ref):
    nseq = h_ref.shape[0]
    C = FF_CHUNK
    K = CONV_WIDTH
    hs = [h_ref[:, n * D_MODEL:(n + 1) * D_MODEL] for n in range(N_NEW)]
    hn = jnp.concatenate([_rms(h, g_ref) for h in hs], axis=0)
    for n in range(N_NEW):
        y_ref[:, n * D_MODEL:(n + 1) * D_MODEL] = hs[n]
    for c in range(N_FF_CHUNKS):
        halves = []
        for half in range(2):
            cols = slice(half * D_FF + c * C, half * D_FF + (c + 1) * C)
            up = jnp.dot(hn, wu_ref[:, cols], preferred_element_type=F32)
            rows = [st_ref[:, t * 2 * D_FF + cols.start:t * 2 * D_FF + cols.stop] for t in range(K - 1)]
            rows += [up[n * nseq:(n + 1) * nseq] for n in range(N_NEW)]
            conv = [cb_ref[:, cols] + sum(rows[n + j] * cw_ref[j:j + 1, cols] for j in range(K))
                    for n in range(N_NEW)]
            for t in range(K - 1):
                hist_ref[:, t * 2 * D_FF + cols.start:t * 2 * D_FF + cols.stop] = rows[N_NEW + t]
            halves.append(jnp.concatenate(conv, axis=0))
        act = _silu_gate(halves[0], halves[1])
        out = jnp.dot(act, wd_ref[c * C:(c + 1) * C, :], preferred_element_type=F32)
        for n in range(N_NEW):
            y_ref[:, n * D_MODEL:(n + 1) * D_MODEL] += out[n * nseq:(n + 1) * nseq]


def _ffn_sample(h, state_ffn, g_ffn, w_up, conv_w, conv_b, w_down):
    nseq = h.shape[0]
    full = lambda a: pl.BlockSpec(a.shape, lambda i: (0,) * a.ndim, pipeline_mode=pl.Buffered(1))
    args = (h, state_ffn, g_ffn, w_up, conv_w, conv_b, w_down)
    return pl.pallas_call(
        _ffn_sample_kernel,
        grid=(1,),
        in_specs=[full(a) for a in args],
        out_specs=[
            pl.BlockSpec((nseq, N_NEW * D_MODEL), lambda i: (0, 0)),
            pl.BlockSpec((nseq, (CONV_WIDTH - 1) * 2 * D_FF), lambda i: (0, 0)),
        ],
        out_shape=[
            jax.ShapeDtypeStruct((nseq, N_NEW * D_MODEL), F32),
            jax.ShapeDtypeStruct((nseq, (CONV_WIDTH - 1) * 2 * D_FF), F32),
        ],
        compiler_params=_params(("arbitrary",)),
        name="ffn_sample",
    )(*args)


def kernel(x_prompt, x_sample, cache_k, cache_v, state_pool, state_ffn_conv, g_attn_norm, w_in, g_q, g_k,
           w_pool, pool_scale, w_out, g_ffn_norm, w_up, conv_w, conv_b, w_down):
    depth = w_in.shape[0]
    assert depth == 1
    batch, seq, _ = x_prompt.shape
    nseq, n_new, _ = x_sample.shape
    assert n_new == N_NEW and seq % (16 * BLK) == 0 and seq % TM == 0
    assert cache_k.shape[2] == 16 * N_STRIDED

    slopes = jnp.asarray(2.0 ** (-8.0 * np.arange(1, N_HEADS + 1) / N_HEADS), dtype=F32)
    head_of_lane = np.arange(ATTN_WIDTH) // HEAD_DIM
    head_mean = jnp.asarray((head_of_lane[:, None] == head_of_lane[None, :]) / HEAD_DIM, dtype=BF16)

    l = 0
    w_in_b = w_in[l].astype(BF16)
    w_pool_b = w_pool[l].astype(BF16)
    w_out_b = w_out[l].astype(BF16)
    w_up_b = w_up[l].astype(BF16)
    w_down_b = w_down[l].astype(BF16)
    g_attn = g_attn_norm[l].reshape(1, D_MODEL)
    g_ffn = g_ffn_norm[l].reshape(1, D_MODEL)
    gq = g_q[l].reshape(1, ATTN_WIDTH)
    gk = g_k[l].reshape(1, ATTN_WIDTH)
    ps = pool_scale[l].reshape(1, POOL_WIDTH)
    cb = conv_b[l].reshape(1, 2 * D_FF)
    cw = conv_w[l]

    xs = x_sample.reshape(nseq, N_NEW * D_MODEL)
    qs, ks, vs, us = _proj(xs, nseq, N_NEW, lambda n: (0, n), lambda n: (0, n), (nseq, N_NEW * ATTN_WIDTH),
                           g_attn, w_in_b, gq, gk, head_mean)
    per_seq = lambda a: a.reshape(nseq, N_NEW, ATTN_WIDTH)
    cache_kt = jnp.transpose(cache_k[l], (0, 2, 3, 1)).reshape(nseq, ATTN_WIDTH, -1)
    cache_vt = jnp.transpose(cache_v[l], (0, 2, 3, 1)).reshape(nseq, ATTN_WIDTH, -1)
    slope_rows = jnp.tile(slopes, N_NEW).reshape(QROWS, 1)

    n_tok = batch * seq
    xp = x_prompt.reshape(n_tok, D_MODEL)
    qp, kp, vp, up, kpt, vpt = _proj(xp, TM, n_tok // TM, lambda i: (i, 0), lambda i: (i, 0),
                                     (n_tok, ATTN_WIDTH), g_attn, w_in_b, gq, gk, head_mean, seq=seq)
    attn_p = _attn_prompt(slopes, qp, kp, vp, batch, seq)
    yp, hist_p, attn_s = _tail_prompt(xp, attn_p, up, w_pool_b, ps, w_out_b, g_ffn, w_up_b, cw, cb, w_down_b,
                                      seq, slope_rows, per_seq(qs), per_seq(ks), per_seq(vs), cache_kt, cache_vt)

    attn_s = attn_s.reshape(nseq, N_NEW * ATTN_WIDTH)
    st_pool = jnp.transpose(state_pool[l], (1, 0, 2))
    hs = _mix_sample(xs, attn_s, us, st_pool, w_pool_b, ps, w_out_b)
    st_ffn = state_ffn_conv[l].reshape(nseq, (CONV_WIDTH - 1) * 2 * D_FF)
    ys, hist_s = _ffn_sample(hs, st_ffn, g_ffn, w_up_b, cw, cb, w_down_b)

    n_keep = min(16 * N_STRIDED, seq)
    window = lambda t: jnp.transpose(
        t.reshape(batch, N_HEADS, HEAD_DIM, seq), (0, 3, 1, 2))[None, :, seq - n_keep:]
    kp5, vp5 = window(kpt), window(vpt)
    new_pool_p = up.reshape(batch, seq, POOL_WIDTH)[None, :, seq - POOL_HIST:]
    new_ffn_p = hist_p[None, :, :CONV_WIDTH - 1]
    u_time = jnp.transpose(us.reshape(nseq, N_NEW, POOL_WIDTH), (1, 0, 2))
    new_pool_s = jnp.transpose(jnp.concatenate([st_pool[N_NEW:], u_time], axis=0), (1, 0, 2))[None]
    return (
        yp.reshape(batch, seq, D_MODEL),
        ys.reshape(nseq, N_NEW, D_MODEL),
        kp5, vp5, new_pool_p, new_ffn_p,
        ks.reshape(1, nseq, N_NEW, N_HEADS, HEAD_DIM),
        vs.reshape(1, nseq, N_NEW, N_HEADS, HEAD_DIM),
        new_pool_s,
        hist_s.reshape(1, nseq, CONV_WIDTH - 1, 2 * D_FF),
    )
```

```python
import functools

import numpy as np
import jax
import jax.numpy as jnp
from jax import lax
from jax.experimental import pallas as pl
from jax.experimental.pallas import tpu as pltpu

D_MODEL = 1024
HEAD_DIM = 64
N_HEADS = 8
ATTN_WIDTH = N_HEADS * HEAD_DIM
POOL_WIDTH = 512
POOL_WINDOWS = (2, 4, 8, 16)
POOL_GROUP = 128
POOL_HIST = 15
PROJ_WIDTH = 3 * ATTN_WIDTH + POOL_WIDTH
DILATIONS = (1, 4, 16)
N_STRIDED = 128
BLK = 128
D_FF = 2816
CONV_WIDTH = 3
PAST_LEN = 8192
EPS = 1e-6
NEG = -1e30
LOG2E = 1.4426950408889634

F32 = jnp.float32
BF16 = jnp.bfloat16

TM = 1024
ATTN_MERGE_UNITS = 8
ATTN_UNITS = 8
FF_CHUNK = 256
N_FF_CHUNKS = D_FF // FF_CHUNK
TAIL_FF_CHUNK = 256
POOL_HEAD = 24
SUBLANES = 8
LANES = 128
CONV_HEAD = (CONV_WIDTH - 1) * SUBLANES
VMEM_LIMIT = 56 * 1024 * 1024


def _params(sem, vmem=VMEM_LIMIT):
    return pltpu.CompilerParams(dimension_semantics=sem, vmem_limit_bytes=vmem)


def _const_spec(shape):
    nd = len(shape)
    return pl.BlockSpec(shape, lambda *_: (0,) * nd, pipeline_mode=pl.Buffered(1))


def _proj_kernel(x_ref, g_ref, w_ref, gq_ref, gk_ref, hm_ref, q_ref, k_ref, v_ref, u_ref, *kvt_refs):
    x = x_ref[...]
    xn = x * lax.rsqrt(jnp.mean(x * x, axis=-1, keepdims=True) + EPS) * g_ref[...]
    proj = jnp.dot(xn.astype(BF16), w_ref[...], preferred_element_type=F32)

    def head_norm(t, g):
        ms = jnp.dot((t * t).astype(BF16), hm_ref[...], preferred_element_type=F32)
        return t * lax.rsqrt(ms + EPS) * g

    q = head_norm(proj[:, :ATTN_WIDTH], gq_ref[...])
    q_ref[...] = q * (HEAD_DIM ** -0.5 * LOG2E)
    k = head_norm(proj[:, ATTN_WIDTH:2 * ATTN_WIDTH], gk_ref[...])
    v = proj[:, 2 * ATTN_WIDTH:3 * ATTN_WIDTH]
    k_ref[...] = k
    v_ref[...] = v
    u_ref[...] = proj[:, 3 * ATTN_WIDTH:]
    if kvt_refs:
        kvt_refs[0][0] = k.T
        kvt_refs[1][0] = v.T


def _proj(x2d, rows, n_steps, x_map, o_map, out_rows, g_attn, w_in, gq, gk, head_mean, seq=None):
    out = jax.ShapeDtypeStruct((out_rows[0], out_rows[1]), F32)
    out_specs = [pl.BlockSpec((rows, ATTN_WIDTH), o_map)] * 4
    out_shape = [out] * 4
    if seq is not None:
        tps = seq // rows
        out_specs += [pl.BlockSpec((1, ATTN_WIDTH, rows), lambda i: (i // tps, 0, i % tps))] * 2
        out_shape += [jax.ShapeDtypeStruct((out_rows[0] // seq, ATTN_WIDTH, seq), F32)] * 2
    return pl.pallas_call(
        _proj_kernel,
        grid=(n_steps,),
        in_specs=[
            pl.BlockSpec((rows, D_MODEL), x_map),
            _const_spec((1, D_MODEL)),
            _const_spec((D_MODEL, PROJ_WIDTH)),
            _const_spec((1, ATTN_WIDTH)),
            _const_spec((1, ATTN_WIDTH)),
            _const_spec((ATTN_WIDTH, ATTN_WIDTH)),
        ],
        out_specs=out_specs,
        out_shape=out_shape,
        compiler_params=_params(("arbitrary",)),
        name="proj",
    )(x2d, g_attn, w_in, gq, gk, head_mean)


def _attn_prompt_kernel(slopes_ref, q_ref, k_ref, v_ref, o_ref, bias_s, acc_s, m_s, l_s):
    hp = pl.program_id(1)
    seq = q_ref.shape[0]
    lane = lax.broadcasted_iota(jnp.int32, (BLK, 2 * HEAD_DIM), 1)
    left = lane < HEAD_DIM

    row = lax.broadcasted_iota(jnp.int32, (2 * BLK, 2 * BLK), 0)
    col = lax.broadcasted_iota(jnp.int32, (2 * BLK, 2 * BLK), 1)
    diff = (row & (BLK - 1)) + BLK - col
    valid = (diff >= 0) & (diff <= N_STRIDED)
    slope = jnp.where(row < BLK, slopes_ref[2 * hp], slopes_ref[2 * hp + 1])
    for pi, r in enumerate(DILATIONS):
        bias_s[pi] = jnp.where(valid, (-LOG2E) * slope * (r * diff).astype(F32), NEG)

    def rows_of(start, r, n=BLK):
        return pl.ds(start, n) if r == 1 else pl.ds(start, n, stride=r)

    ones_cols = jnp.ones((2 * BLK, 2 * HEAD_DIM), BF16)

    def units(specs, with_prev):
        curs = [rows_of(st, r) for _, r, st in specs]
        qqs, kks, vvs = [], [], []
        for (_, r, st), cur in zip(specs, curs):
            q2 = q_ref[cur, :]
            qqs.append(
                jnp.concatenate([jnp.where(left, q2, 0.0), jnp.where(left, 0.0, q2)], axis=0).astype(BF16))
            if not with_prev:
                kk, vv = k_ref[cur, :], v_ref[cur, :]
            elif r == 1:
                both = pl.ds(st - BLK, 2 * BLK)
                kk, vv = k_ref[both, :], v_ref[both, :]
            else:
                prev = rows_of(st - BLK * r, r)
                kk = jnp.concatenate([k_ref[prev, :], k_ref[cur, :]], axis=0)
                vv = jnp.concatenate([v_ref[prev, :], v_ref[cur, :]], axis=0)
            kks.append(kk.astype(BF16))
            vvs.append(jnp.concatenate([vv.astype(BF16), ones_cols[:vv.shape[0]]], axis=1))
        ss = [lax.dot_general(qq, kk, (((1,), (1,)), ((), ())), preferred_element_type=F32)
              + (bias_s[pi] if with_prev else bias_s[pi, :, BLK:])
              for (pi, _, _), qq, kk in zip(specs, qqs, kks)]
        ms = [jnp.max(s, axis=-1, keepdims=True) for s in ss]
        ps = [jnp.exp2(s - m).astype(BF16) for s, m in zip(ss, ms)]
        pvs = [jnp.dot(p, vv, preferred_element_type=F32) for p, vv in zip(ps, vvs)]
        W2 = 2 * HEAD_DIM
        for (pi, _, _), cur, pv, m in zip(specs, curs, pvs, ms):
            acc = jnp.where(left, pv[:BLK, :W2], pv[BLK:, :W2])
            mt = jnp.where(left, m[:BLK], m[BLK:])
            lt = jnp.where(left, pv[:BLK, W2:], pv[BLK:, W2:])
            if pi > 0:
                acc_s[pi - 1, cur, :] = acc
                m_s[pi - 1, cur, :] = mt
                l_s[pi - 1, cur, :] = lt
                continue
            parts = [(acc, mt, lt)] + [(acc_s[j, cur, :], m_s[j, cur, :], l_s[j, cur, :])
                                       for j in range(len(DILATIONS) - 1)]
            mx = parts[0][1]
            for _, mj, _ in parts[1:]:
                mx = jnp.maximum(mx, mj)
            num = jnp.zeros((BLK, W2), F32)
            den = jnp.zeros((BLK, W2), F32)
            for aj, mj, lj in parts:
                w = jnp.exp2(mj - mx)
                num = num + w * aj
                den = den + w * lj
            o_ref[cur, :] = (num / den).astype(o_ref.dtype)

    U = ATTN_UNITS
    assert DILATIONS[0] == 1
    for pats, per_group in ((range(1, len(DILATIONS)), U), ((0,), ATTN_MERGE_UNITS)):
        first = [(pi, DILATIONS[pi], c) for pi in pats for c in range(DILATIONS[pi])]
        rest = [(pi, DILATIONS[pi], c + blk * (BLK * DILATIONS[pi])) for pi in pats
                for blk in range(1, seq // (DILATIONS[pi] * BLK)) for c in range(DILATIONS[pi])]
        for group, with_prev in ((first, False), (rest, True)):
            for i0 in range(0, len(group), per_group):
                units(group[i0:i0 + per_group], with_prev)


def _attn_prompt(slopes, q, k, v, batch, seq):
    blk = pl.BlockSpec((seq, 2 * HEAD_DIM), lambda b, h: (b, h))
    return pl.pallas_call(
        _attn_prompt_kernel,
        grid=(batch, N_HEADS // 2),
        in_specs=[pl.BlockSpec(memory_space=pltpu.SMEM), blk, blk, blk],
        out_specs=blk,
        out_shape=jax.ShapeDtypeStruct((batch * seq, ATTN_WIDTH), BF16),
        scratch_shapes=[
            pltpu.VMEM((len(DILATIONS), 2 * BLK, 2 * BLK), F32),
            pltpu.VMEM((len(DILATIONS) - 1, seq, 2 * HEAD_DIM), F32),
            pltpu.VMEM((len(DILATIONS) - 1, seq, 2 * HEAD_DIM), F32),
            pltpu.VMEM((len(DILATIONS) - 1, seq, 2 * HEAD_DIM), F32),
        ],
        compiler_params=_params(("arbitrary", "arbitrary")),
        name="attn_prompt",
    )(slopes, q, k, v)


N_NEW = 4
QROWS = N_NEW * N_HEADS


SAMPLE_TCHUNK = 512


def _attn_sample_tables(slope_ref, bias_s, mult_s):
    wb = bias_s.shape[1]
    n_of_row = lax.broadcasted_iota(jnp.int32, (QROWS, 1), 0) // N_HEADS
    t = lax.broadcasted_iota(jnp.int32, (QROWS, wb), 1)
    dist = wb + n_of_row - t
    mult = jnp.zeros((QROWS, wb), F32)
    for r in DILATIONS:
        hit = ((dist & (r - 1)) == 0) & (dist <= r * N_STRIDED)
        mult = mult + jnp.where(hit, 1.0, 0.0)
    mult_s[...] = mult
    bias_s[...] = jnp.where(mult > 0.0, (-LOG2E) * slope_ref[...] * dist.astype(F32), NEG)


def _attn_sample_pieces(slope_ref, q_ref, kn_ref, vn_ref, kt_ref, vt_ref, o_ref, bias_s, mult_s):
    W = ATTN_WIDTH
    wb = kt_ref.shape[2]
    chunks = [slice(c, c + SAMPLE_TCHUNK) for c in range(0, wb, SAMPLE_TCHUNK)]
    slope = slope_ref[...]
    n_of_row = lax.broadcasted_iota(jnp.int32, (QROWS, 1), 0) // N_HEADS
    q = q_ref[0]
    kn = kn_ref[0]
    vn = vn_ref[0]
    sub = lax.broadcasted_iota(jnp.int32, (N_HEADS, W), 0)
    own = sub == lax.broadcasted_iota(jnp.int32, (N_HEADS, W), 1) // HEAD_DIM
    own_rows = jnp.concatenate([own] * N_NEW, axis=0)
    qbd = jnp.concatenate(
        [jnp.where(own, jnp.broadcast_to(q[n:n + 1], (N_HEADS, W)), 0.0) for n in range(N_NEW)], axis=0)
    qbd16 = qbd.astype(BF16)

    ss = []
    for c in chunks:
        ss.append(jnp.dot(qbd16, kt_ref[0, :, c].astype(BF16), preferred_element_type=F32) + bias_s[:, c])
        yield
    s_new, w_new = [], []
    for m in range(N_NEW):
        sm = jnp.sum(qbd * kn[m:m + 1], axis=-1, keepdims=True)
        gap = n_of_row - m
        s_new.append(jnp.where(gap >= 0, sm - LOG2E * slope * gap.astype(F32), NEG))
        w_new.append(jnp.where(gap == 0, float(len(DILATIONS)), jnp.where(gap > 0, 1.0, 0.0)))
    mx = s_new[0]
    for sm in s_new[1:] + [jnp.max(s, axis=-1, keepdims=True) for s in ss]:
        mx = jnp.maximum(mx, sm)
    den = jnp.zeros((QROWS, 1), F32)
    pv = jnp.zeros((QROWS, W), F32)
    for s, c in zip(ss, chunks):
        p = jnp.exp2(s - mx) * mult_s[:, c]
        den = den + jnp.sum(p, axis=-1, keepdims=True)
        pv = pv + lax.dot_general(p.astype(BF16), vt_ref[0, :, c].astype(BF16), (((1,), (1,)), ((), ())),
                                  preferred_element_type=F32)
        if c is not chunks[-1]:
            yield
    for m in range(N_NEW):
        pm = jnp.exp2(s_new[m] - mx) * w_new[m]
        den = den + pm
        pv = pv + pm * vn[m:m + 1]
    full = jnp.where(own_rows, pv / den, 0.0)
    o_ref[0] = jnp.sum(full.reshape(N_NEW, N_HEADS, W), axis=1).astype(o_ref.dtype)


def _pool_mix(d_groups, wp_ref, ps_ref):
    mixed = [jnp.dot(d.astype(BF16), wp_ref[g], preferred_element_type=F32) for g, d in enumerate(d_groups)]
    return jnp.concatenate(mixed, axis=-1) * ps_ref[...]


def _out_proj(x, attn, pool_out, wo_ref):
    h = x + jnp.dot(attn, wo_ref[:ATTN_WIDTH, :], preferred_element_type=F32)
    return h + jnp.dot(pool_out.astype(BF16), wo_ref[ATTN_WIDTH:, :], preferred_element_type=F32)


def _mix_prompt_tile(x_ref, a_ref, u_ref, wp_ref, ps_ref, wo_ref, e_s, s2_s, s4_s, s8_s, tiles_per_seq):
    i = pl.program_id(0)
    tm = u_ref.shape[0]
    H = POOL_HEAD
    G = POOL_GROUP

    @pl.when(i == 0)
    def _():
        e_s[0:8, :] = jnp.zeros((8, POOL_WIDTH), F32)
        s2_s[0:8, :] = jnp.zeros((8, POOL_WIDTH), F32)
        s4_s[0:8, :] = jnp.zeros((8, 3 * G), F32)
        s8_s[0:8, :] = jnp.zeros((8, 2 * G), F32)

    @pl.when(i % tiles_per_seq == 0)
    def _():
        e_s[8:H, :] = jnp.zeros((H - 8, POOL_WIDTH), F32)

    u = u_ref[...]
    e_s[H:, :] = u
    n = tm + H - 8
    s2_s[8:, :] = e_s[8:, :] + e_s[7:7 + n, :]
    s4_s[8:, :] = s2_s[8:, G:] + s2_s[6:6 + n, G:]
    s8_s[8:, :] = s4_s[8:, G:] + s4_s[4:4 + n, G:]
    sums = [
        s2_s[H:, :G],
        s4_s[H:, :G],
        s8_s[H:, :G],
        s8_s[H:, G:] + s8_s[H - 8:H - 8 + tm, G:],
    ]
    pos = (i % tiles_per_seq) * tm + lax.broadcasted_iota(jnp.int32, (tm, 1), 0)
    d = []
    for g, w in enumerate(POOL_WINDOWS):
        cnt = jnp.minimum(w, pos + 1).astype(F32)
        d.append(sums[g] / cnt - u[:, g * G:(g + 1) * G])
    pool_out = _pool_mix(d, wp_ref, ps_ref)
    h = _out_proj(x_ref[...], a_ref[...], pool_out, wo_ref)
    e_s[8:H, :] = e_s[tm + 8:tm + H, :]
    return h


def _mix_sample_kernel(x_ref, a_ref, u_ref, st_ref, wp_ref, ps_ref, wo_ref, h_ref):
    G = POOL_GROUP
    PW = POOL_WIDTH

    def row(t, lanes):
        if t < POOL_HIST:
            return st_ref[t, :, lanes]
        return u_ref[:, (t - POOL_HIST) * PW + lanes.start:(t - POOL_HIST) * PW + lanes.stop]

    for n in range(N_NEW):
        t = POOL_HIST + n
        d = []
        for g, w in enumerate(POOL_WINDOWS):
            lanes = slice(g * G, (g + 1) * G)
            cur = row(t, lanes)
            tot = cur
            for j in range(1, w):
                tot = tot + row(t - j, lanes)
            d.append(tot / float(min(w, PAST_LEN + n + 1)) - cur)
        pool_out = _pool_mix(d, wp_ref, ps_ref)
        cols = slice(n * D_MODEL, (n + 1) * D_MODEL)
        attn = a_ref[:, n * ATTN_WIDTH:(n + 1) * ATTN_WIDTH]
        h_ref[:, cols] = _out_proj(x_ref[:, cols], attn, pool_out, wo_ref)


def _mix_sample(x, attn, u, state_pool, w_pool, pool_scale, w_out):
    nseq = x.shape[0]
    full = lambda a: pl.BlockSpec(a.shape, lambda i: (0,) * a.ndim)
    args = (x, attn, u, state_pool, w_pool, pool_scale, w_out)
    return pl.pallas_call(
        _mix_sample_kernel,
        grid=(1,),
        in_specs=[full(a) for a in args],
        out_specs=pl.BlockSpec((nseq, N_NEW * D_MODEL), lambda i: (0, 0)),
        out_shape=jax.ShapeDtypeStruct((nseq, N_NEW * D_MODEL), F32),
        compiler_params=_params(("arbitrary",)),
        name="mix_sample",
    )(*args)


def _rms(h, g_ref):
    return (h * lax.rsqrt(jnp.mean(h * h, axis=-1, keepdims=True) + EPS) * g_ref[...]).astype(BF16)


def _silu_gate(gate, val):
    return (gate / (1.0 + jnp.exp(-gate)) * val).astype(BF16)


def _tail_prompt_kernel(x_ref, a_ref, u_ref, wp_ref, ps_ref, wo_ref, g_ref, wu_ref, cw_ref, cb_ref, wd_ref,
                        slope_ref, qs_ref, kn_ref, vn_ref, kt_ref, vt_ref,
                        y_ref, hist_ref, os_ref,
                        e_s, s2_s, s4_s, s8_s, h_s, ext_s, carry_s, act_s, bias_s, mult_s, *, tiles_per_seq):
    i = pl.program_id(0)
    tm = x_ref.shape[0]
    C = TAIL_FF_CHUNK
    K1 = CONV_WIDTH - 1
    nr = tm // SUBLANES
    n_lane_blocks = D_MODEL // LANES

    @pl.when(i == 0)
    def _():
        _attn_sample_tables(slope_ref, bias_s, mult_s)

    @pl.when(i % tiles_per_seq == 0)
    def _():
        carry_s[...] = jnp.zeros(carry_s.shape, F32)

    h = _mix_prompt_tile(x_ref, a_ref, u_ref, wp_ref, ps_ref, wo_ref, e_s, s2_s, s4_s, s8_s, tiles_per_seq)
    for s in range(SUBLANES):
        for lb in range(n_lane_blocks):
            h_s[lb, pl.ds(s, nr, stride=SUBLANES), :] = h[s * nr:(s + 1) * nr, lb * LANES:(lb + 1) * LANES]
    sample = _attn_sample_pieces(slope_ref, qs_ref, kn_ref, vn_ref, kt_ref, vt_ref, os_ref, bias_s, mult_s)
    hn = _rms(jnp.concatenate([h_s[lb] for lb in range(n_lane_blocks)], axis=1), g_ref)
    for c0 in range(0, D_FF, C):
        cw = min(C, D_FF - c0)
        first_sub = lax.broadcasted_iota(jnp.int32, (SUBLANES, cw), 0) == 0
        halves = []
        for half in range(2):
            cols = slice(half * D_FF + c0, half * D_FF + c0 + cw)
            up = jnp.dot(hn, wu_ref[:, cols], preferred_element_type=F32)
            ext = ext_s.at[half, :, :cw]
            for j in range(K1):
                rows = slice(j * SUBLANES, (j + 1) * SUBLANES)
                cur = up[(nr - K1 + j) * SUBLANES:(nr - K1 + j + 1) * SUBLANES]
                ext[rows, :] = jnp.where(first_sub, pltpu.roll(carry_s[rows, cols], 1, 0),
                                         pltpu.roll(cur, 1, 0))
            ext[K1 * SUBLANES:, :] = up
            conv = cb_ref[:, cols] + sum(
                ext[j * SUBLANES:j * SUBLANES + tm, :] * cw_ref[j:j + 1, cols] for j in range(CONV_WIDTH))
            carry_s[:, cols] = up[tm - K1 * SUBLANES:]
            halves.append(conv)
        act_s[:, c0:c0 + cw] = _silu_gate(halves[0], halves[1])
        next(sample, None)
    for _ in sample:
        pass
    y = (jnp.concatenate([h_s[lb] for lb in range(n_lane_blocks)], axis=1)
         + jnp.dot(act_s[...], wd_ref[...], preferred_element_type=F32))
    for lb in range(n_lane_blocks):
        h_s[lb] = y[:, lb * LANES:(lb + 1) * LANES]
    for s in range(SUBLANES):
        for lb in range(n_lane_blocks):
            y_ref[s * nr:(s + 1) * nr, lb * LANES:(lb + 1) * LANES] = h_s[lb, pl.ds(s, nr, stride=SUBLANES), :]

    @pl.when(i % tiles_per_seq == tiles_per_seq - 1)
    def _():
        hist_ref[0] = jnp.zeros(hist_ref.shape[1:], F32)
        for j in range(K1):
            hist_ref[0, j:j + 1, :] = carry_s[(j + 1) * SUBLANES - 1:(j + 1) * SUBLANES, :]


def _tail_prompt(x2d, attn, u, w_pool, pool_scale, w_out, g_ffn, w_up, conv_w, conv_b, w_down, seq,
                 slope_rows, q_s, kn_s, vn_s, cache_kt, cache_vt):
    n_tok = x2d.shape[0]
    nseq, W, wb = cache_kt.shape
    tm = n_tok // nseq
    assert tm * nseq == n_tok and tm % 8 == 0 and seq % tm == 0
    tps = seq // tm
    G = POOL_GROUP
    pool_rows = tm + POOL_HEAD
    new_spec = pl.BlockSpec((1, N_NEW, W), lambda i: (i, 0, 0))
    cache_spec = pl.BlockSpec((1, W, wb), lambda i: (i, 0, 0))
    return pl.pallas_call(
        functools.partial(_tail_prompt_kernel, tiles_per_seq=tps),
        grid=(nseq,),
        in_specs=[
            pl.BlockSpec((tm, D_MODEL), lambda i: (i, 0)),
            pl.BlockSpec((tm, ATTN_WIDTH), lambda i: (i, 0)),
            pl.BlockSpec((tm, POOL_WIDTH), lambda i: (i, 0)),
            _const_spec((len(POOL_WINDOWS), G, G)),
            _const_spec((1, POOL_WIDTH)),
            _const_spec((D_MODEL, D_MODEL)),
            _const_spec((1, D_MODEL)),
            _const_spec((D_MODEL, 2 * D_FF)),
            _const_spec((CONV_WIDTH, 2 * D_FF)),
            _const_spec((1, 2 * D_FF)),
            _const_spec((D_FF, D_MODEL)),
            _const_spec((QROWS, 1)),
            new_spec, new_spec, new_spec, cache_spec, cache_spec,
        ],
        out_specs=[
            pl.BlockSpec((tm, D_MODEL), lambda i: (i, 0)),
            pl.BlockSpec((1, SUBLANES, 2 * D_FF), lambda i: (i // tps, 0, 0)),
            new_spec,
        ],
        out_shape=[
            jax.ShapeDtypeStruct((n_tok, D_MODEL), F32),
            jax.ShapeDtypeStruct((n_tok // seq, SUBLANES, 2 * D_FF), F32),
            jax.ShapeDtypeStruct((nseq, N_NEW, W), BF16),
        ],
        scratch_shapes=[
            pltpu.VMEM((pool_rows, POOL_WIDTH), F32),
            pltpu.VMEM((pool_rows, POOL_WIDTH), F32),
            pltpu.VMEM((pool_rows, 3 * G), F32),
            pltpu.VMEM((pool_rows, 2 * G), F32),
            pltpu.VMEM((D_MODEL // LANES, tm, LANES), F32),
            pltpu.VMEM((2, tm + CONV_HEAD, TAIL_FF_CHUNK), F32),
            pltpu.VMEM((CONV_HEAD, 2 * D_FF), F32),
            pltpu.VMEM((tm, D_FF), BF16),
            pltpu.VMEM((QROWS, wb), F32),
            pltpu.VMEM((QROWS, wb), F32),
        ],
        compiler_params=_params(("arbitrary",)),
        name="tail_prompt",
    )(x2d, attn, u, w_pool, pool_scale, w_out, g_ffn, w_up, conv_w, conv_b, w_down,
      slope_rows, q_s, kn_s, vn_s, cache_kt, cache_vt)


def _ffn_sample_kernel(h_ref, st_ref, g_ref, wu_ref, cw_ref, cb_ref, wd_ref, y_ref, hist_ref):
    nseq = h_ref.shape[0]
    C = FF_CHUNK
    K = CONV_WIDTH
    hs = [h_ref[:, n * D_MODEL:(n + 1) * D_MODEL] for n in range(N_NEW)]
    hn = jnp.concatenate([_rms(h, g_ref) for h in hs], axis=0)
    for n in range(N_NEW):
        y_ref[:, n * D_MODEL:(n + 1) * D_MODEL] = hs[n]
    for c in range(N_FF_CHUNKS):
        halves = []
        for half in range(2):
            cols = slice(half * D_FF + c * C, half * D_FF + (c + 1) * C)
            up = jnp.dot(hn, wu_ref[:, cols], preferred_element_type=F32)
            rows = [st_ref[:, t * 2 * D_FF + cols.start:t * 2 * D_FF + cols.stop] for t in range(K - 1)]
            rows += [up[n * nseq:(n + 1) * nseq] for n in range(N_NEW)]
            conv = [cb_ref[:, cols] + sum(rows[n + j] * cw_ref[j:j + 1, cols] for j in range(K))
                    for n in range(N_NEW)]
            for t in range(K - 1):
                hist_ref[:, t * 2 * D_FF + cols.start:t * 2 * D_FF + cols.stop] = rows[N_NEW + t]
            halves.append(jnp.concatenate(conv, axis=0))
        act = _silu_gate(halves[0], halves[1])
        out = jnp.dot(act, wd_ref[c * C:(c + 1) * C, :], preferred_element_type=F32)
        for n in range(N_NEW):
            y_ref[:, n * D_MODEL:(n + 1) * D_MODEL] += out[n * nseq:(n + 1) * nseq]


def _ffn_sample(h, state_ffn, g_ffn, w_up, conv_w, conv_b, w_down):
    nseq = h.shape[0]
    full = lambda a: pl.BlockSpec(a.shape, lambda i: (0,) * a.ndim, pipeline_mode=pl.Buffered(1))
    args = (h, state_ffn, g_ffn, w_up, conv_w, conv_b, w_down)
    return pl.pallas_call(
        _ffn_sample_kernel,
        grid=(1,),
        in_specs=[full(a) for a in args],
        out_specs=[
            pl.BlockSpec((nseq, N_NEW * D_MODEL), lambda i: (0, 0)),
            pl.BlockSpec((nseq, (CONV_WIDTH - 1) * 2 * D_FF), lambda i: (0, 0)),
        ],
        out_shape=[
            jax.ShapeDtypeStruct((nseq, N_NEW * D_MODEL), F32),
            jax.ShapeDtypeStruct((nseq, (CONV_WIDTH - 1) * 2 * D_FF), F32),
        ],
        compiler_params=_params(("arbitrary",)),
        name="ffn_sample",
    )(*args)


def kernel(x_prompt, x_sample, cache_k, cache_v, state_pool, state_ffn_conv, g_attn_norm, w_in, g_q, g_k,
           w_pool, pool_scale, w_out, g_ffn_norm, w_up, conv_w, conv_b, w_down):
    depth = w_in.shape[0]
    assert depth == 1
    batch, seq, _ = x_prompt.shape
    nseq, n_new, _ = x_sample.shape
    assert n_new == N_NEW and seq % (16 * BLK) == 0 and seq % TM == 0
    assert cache_k.shape[2] == 16 * N_STRIDED

    slopes = jnp.asarray(2.0 ** (-8.0 * np.arange(1, N_HEADS + 1) / N_HEADS), dtype=F32)
    head_of_lane = np.arange(ATTN_WIDTH) // HEAD_DIM
    head_mean = jnp.asarray((head_of_lane[:, None] == head_of_lane[None, :]) / HEAD_DIM, dtype=BF16)

    l = 0
    w_in_b = w_in[l].astype(BF16)
    w_pool_b = w_pool[l].astype(BF16)
    w_out_b = w_out[l].astype(BF16)
    w_up_b = w_up[l].astype(BF16)
    w_down_b = w_down[l].astype(BF16)
    g_attn = g_attn_norm[l].reshape(1, D_MODEL)
    g_ffn = g_ffn_norm[l].reshape(1, D_MODEL)
    gq = g_q[l].reshape(1, ATTN_WIDTH)
    gk = g_k[l].reshape(1, ATTN_WIDTH)
    ps = pool_scale[l].reshape(1, POOL_WIDTH)
    cb = conv_b[l].reshape(1, 2 * D_FF)
    cw = conv_w[l]

    xs = x_sample.reshape(nseq, N_NEW * D_MODEL)
    qs, ks, vs, us = _proj(xs, nseq, N_NEW, lambda n: (0, n), lambda n: (0, n), (nseq, N_NEW * ATTN_WIDTH),
                           g_attn, w_in_b, gq, gk, head_mean)
    per_seq = lambda a: a.reshape(nseq, N_NEW, ATTN_WIDTH)
    cache_kt = jnp.transpose(cache_k[l], (0, 2, 3, 1)).reshape(nseq, ATTN_WIDTH, -1)
    cache_vt = jnp.transpose(cache_v[l], (0, 2, 3, 1)).reshape(nseq, ATTN_WIDTH, -1)
    slope_rows = jnp.tile(slopes, N_NEW).reshape(QROWS, 1)

    n_tok = batch * seq
    xp = x_prompt.reshape(n_tok, D_MODEL)
    qp, kp, vp, up, kpt, vpt = _proj(xp, TM, n_tok // TM, lambda i: (i, 0), lambda i: (i, 0),
                                     (n_tok, ATTN_WIDTH), g_attn, w_in_b, gq, gk, head_mean, seq=seq)
    attn_p = _attn_prompt(slopes, qp, kp, vp, batch, seq)
    yp, hist_p, attn_s = _tail_prompt(xp, attn_p, up, w_pool_b, ps, w_out_b, g_ffn, w_up_b, cw, cb, w_down_b,
                                      seq, slope_rows, per_seq(qs), per_seq(ks), per_seq(vs), cache_kt, cache_vt)

    attn_s = attn_s.reshape(nseq, N_NEW * ATTN_WIDTH)
    st_pool = jnp.transpose(state_pool[l], (1, 0, 2))
    hs = _mix_sample(xs, attn_s, us, st_pool, w_pool_b, ps, w_out_b)
    st_ffn = state_ffn_conv[l].reshape(nseq, (CONV_WIDTH - 1) * 2 * D_FF)
    ys, hist_s = _ffn_sample(hs, st_ffn, g_ffn, w_up_b, cw, cb, w_down_b)

    n_keep = min(16 * N_STRIDED, seq)
    window = lambda t: jnp.transpose(
        t.reshape(batch, N_HEADS, HEAD_DIM, seq), (0, 3, 1, 2))[None, :, seq - n_keep:]
    kp5, vp5 = window(kpt), window(vpt)
    new_pool_p = up.reshape(batch, seq, POOL_WIDTH)[None, :, seq - POOL_HIST:]
    new_ffn_p = hist_p[None, :, :CONV_WIDTH - 1]
    u_time = jnp.transpose(us.reshape(nseq, N_NEW, POOL_WIDTH), (1, 0, 2))
    new_pool_s = jnp.transpose(jnp.concatenate([st_pool[N_NEW:], u_time], axis=0), (1, 0, 2))[None]
    return (
        yp.reshape(batch, seq, D_MODEL),
        ys.reshape(nseq, N_NEW, D_MODEL),
        kp5, vp5, new_pool_p, new_ffn_p,
        ks.reshape(1, nseq, N_NEW, N_HEADS, HEAD_DIM),
        vs.reshape(1, nseq, N_NEW, N_HEADS, HEAD_DIM),
        new_pool_s,
        hist_s.reshape(1, nseq, CONV_WIDTH - 1, 2 * D_FF),
    )
```

```python
import functools

import numpy as np
import jax
import jax.numpy as jnp
from jax import lax
from jax.experimental import pallas as pl
from jax.experimental.pallas import tpu as pltpu

D_MODEL = 1024
HEAD_DIM = 64
N_HEADS = 8
ATTN_WIDTH = N_HEADS * HEAD_DIM
POOL_WIDTH = 512
POOL_WINDOWS = (2, 4, 8, 16)
POOL_GROUP = 128
POOL_HIST = 15
PROJ_WIDTH = 3 * ATTN_WIDTH + POOL_WIDTH
DILATIONS = (1, 4, 16)
N_STRIDED = 128
BLK = 128
D_FF = 2816
CONV_WIDTH = 3
PAST_LEN = 8192
EPS = 1e-6
NEG = -1e30
LOG2E = 1.4426950408889634

F32 = jnp.float32
BF16 = jnp.bfloat16

TM = 1024
ATTN_MERGE_UNITS = 8
ATTN_UNITS = 8
FF_CHUNK = 256
N_FF_CHUNKS = D_FF // FF_CHUNK
TAIL_FF_CHUNK = 256
POOL_HEAD = 24
SUBLANES = 8
LANES = 128
CONV_HEAD = (CONV_WIDTH - 1) * SUBLANES
VMEM_LIMIT = 56 * 1024 * 1024


def _params(sem, vmem=VMEM_LIMIT):
    return pltpu.CompilerParams(dimension_semantics=sem, vmem_limit_bytes=vmem)


def _const_spec(shape):
    nd = len(shape)
    return pl.BlockSpec(shape, lambda *_: (0,) * nd, pipeline_mode=pl.Buffered(1))


def _proj_kernel(x_ref, g_ref, w_ref, gq_ref, gk_ref, hm_ref, q_ref, k_ref, v_ref, u_ref, *kvt_refs):
    x = x_ref[...]
    xn = x * lax.rsqrt(jnp.mean(x * x, axis=-1, keepdims=True) + EPS) * g_ref[...]
    proj = jnp.dot(xn.astype(BF16), w_ref[...], preferred_element_type=F32)

    def head_norm(t, g):
        ms = jnp.dot((t * t).astype(BF16), hm_ref[...], preferred_element_type=F32)
        return t * lax.rsqrt(ms + EPS) * g

    def put(ref, val):
        if len(ref.shape) == 2:
            ref[...] = val
        else:
            for hp in range(ref.shape[0]):
                ref[hp] = val[:, hp * LANES:(hp + 1) * LANES]

    q = head_norm(proj[:, :ATTN_WIDTH], gq_ref[...])
    k = head_norm(proj[:, ATTN_WIDTH:2 * ATTN_WIDTH], gk_ref[...])
    v = proj[:, 2 * ATTN_WIDTH:3 * ATTN_WIDTH]
    put(q_ref, q * (HEAD_DIM ** -0.5 * LOG2E))
    put(k_ref, k)
    put(v_ref, v)
    u_ref[...] = proj[:, 3 * ATTN_WIDTH:]
    if kvt_refs:
        kvt_refs[0][0] = k.T
        kvt_refs[1][0] = v.T


def _proj(x2d, rows, n_steps, x_map, o_map, out_rows, g_attn, w_in, gq, gk, head_mean, seq=None):
    out = jax.ShapeDtypeStruct((out_rows[0], out_rows[1]), F32)
    out_specs = [pl.BlockSpec((rows, ATTN_WIDTH), o_map)] * 4
    out_shape = [out] * 4
    if seq is not None:
        pairs = ATTN_WIDTH // LANES
        out_specs[:3] = [pl.BlockSpec((pairs, rows, LANES), lambda i: (0, i, 0))] * 3
        out_shape[:3] = [jax.ShapeDtypeStruct((pairs, out_rows[0], LANES), F32)] * 3
        tps = seq // rows
        out_specs += [pl.BlockSpec((1, ATTN_WIDTH, rows), lambda i: (i // tps, 0, i % tps))] * 2
        out_shape += [jax.ShapeDtypeStruct((out_rows[0] // seq, ATTN_WIDTH, seq), F32)] * 2
    return pl.pallas_call(
        _proj_kernel,
        grid=(n_steps,),
        in_specs=[
            pl.BlockSpec((rows, D_MODEL), x_map),
            _const_spec((1, D_MODEL)),
            _const_spec((D_MODEL, PROJ_WIDTH)),
            _const_spec((1, ATTN_WIDTH)),
            _const_spec((1, ATTN_WIDTH)),
            _const_spec((ATTN_WIDTH, ATTN_WIDTH)),
        ],
        out_specs=out_specs,
        out_shape=out_shape,
        compiler_params=_params(("arbitrary",)),
        name="proj",
    )(x2d, g_attn, w_in, gq, gk, head_mean)


def _attn_prompt_kernel(slopes_ref, q_ref, k_ref, v_ref, o_ref, bias_s, acc_s, m_s, l_s):
    hp = pl.program_id(1)
    seq = q_ref.shape[0]
    lane = lax.broadcasted_iota(jnp.int32, (BLK, 2 * HEAD_DIM), 1)
    left = lane < HEAD_DIM

    row = lax.broadcasted_iota(jnp.int32, (2 * BLK, 2 * BLK), 0)
    col = lax.broadcasted_iota(jnp.int32, (2 * BLK, 2 * BLK), 1)
    diff = (row & (BLK - 1)) + BLK - col
    valid = (diff >= 0) & (diff <= N_STRIDED)
    slope = jnp.where(row < BLK, slopes_ref[2 * hp], slopes_ref[2 * hp + 1])
    for pi, r in enumerate(DILATIONS):
        bias_s[pi] = jnp.where(valid, (-LOG2E) * slope * (r * diff).astype(F32), NEG)

    def rows_of(start, r, n=BLK):
        return pl.ds(start, n) if r == 1 else pl.ds(start, n, stride=r)

    ones_cols = jnp.ones((2 * BLK, 2 * HEAD_DIM), BF16)

    def units(specs, with_prev):
        curs = [rows_of(st, r) for _, r, st in specs]
        qqs, kks, vvs = [], [], []
        for (_, r, st), cur in zip(specs, curs):
            q2 = q_ref[cur, :]
            qqs.append(
                jnp.concatenate([jnp.where(left, q2, 0.0), jnp.where(left, 0.0, q2)], axis=0).astype(BF16))
            if not with_prev:
                kk, vv = k_ref[cur, :], v_ref[cur, :]
            elif r == 1:
                both = pl.ds(st - BLK, 2 * BLK)
                kk, vv = k_ref[both, :], v_ref[both, :]
            else:
                prev = rows_of(st - BLK * r, r)
                kk = jnp.concatenate([k_ref[prev, :], k_ref[cur, :]], axis=0)
                vv = jnp.concatenate([v_ref[prev, :], v_ref[cur, :]], axis=0)
            kks.append(kk.astype(BF16))
            vvs.append(jnp.concatenate([vv.astype(BF16), ones_cols[:vv.shape[0]]], axis=1))
        ss = [lax.dot_general(qq, kk, (((1,), (1,)), ((), ())), preferred_element_type=F32)
              + (bias_s[pi] if with_prev else bias_s[pi, :, BLK:])
              for (pi, _, _), qq, kk in zip(specs, qqs, kks)]
        ms = [jnp.max(s, axis=-1, keepdims=True) for s in ss]
        ps = [jnp.exp2(s - m).astype(BF16) for s, m in zip(ss, ms)]
        pvs = [jnp.dot(p, vv, preferred_element_type=F32) for p, vv in zip(ps, vvs)]
        W2 = 2 * HEAD_DIM
        for (pi, _, _), cur, pv, m in zip(specs, curs, pvs, ms):
            acc = jnp.where(left, pv[:BLK, :W2], pv[BLK:, :W2])
            mt = jnp.where(left, m[:BLK], m[BLK:])
            lt = jnp.where(left, pv[:BLK, W2:], pv[BLK:, W2:])
            if pi > 0:
                acc_s[pi - 1, cur, :] = acc
                m_s[pi - 1, cur, :] = mt
                l_s[pi - 1, cur, :] = lt
                continue
            parts = [(acc, mt, lt)] + [(acc_s[j, cur, :], m_s[j, cur, :], l_s[j, cur, :])
                                       for j in range(len(DILATIONS) - 1)]
            mx = parts[0][1]
            for _, mj, _ in parts[1:]:
                mx = jnp.maximum(mx, mj)
            num = jnp.zeros((BLK, W2), F32)
            den = jnp.zeros((BLK, W2), F32)
            for aj, mj, lj in parts:
                w = jnp.exp2(mj - mx)
                num = num + w * aj
                den = den + w * lj
            o_ref[cur, :] = (num / den).astype(o_ref.dtype)

    U = ATTN_UNITS
    assert DILATIONS[0] == 1
    for pats, per_group in ((range(1, len(DILATIONS)), U), ((0,), ATTN_MERGE_UNITS)):
        first = [(pi, DILATIONS[pi], c) for pi in pats for c in range(DILATIONS[pi])]
        rest = [(pi, DILATIONS[pi], c + blk * (BLK * DILATIONS[pi])) for pi in pats
                for blk in range(1, seq // (DILATIONS[pi] * BLK)) for c in range(DILATIONS[pi])]
        for group, with_prev in ((first, False), (rest, True)):
            for i0 in range(0, len(group), per_group):
                units(group[i0:i0 + per_group], with_prev)


def _attn_prompt(slopes, q, k, v, batch, seq):
    blk = pl.BlockSpec((None, seq, 2 * HEAD_DIM), lambda b, h: (h, b, 0))
    return pl.pallas_call(
        _attn_prompt_kernel,
        grid=(batch, N_HEADS // 2),
        in_specs=[pl.BlockSpec(memory_space=pltpu.SMEM), blk, blk, blk],
        out_specs=blk,
        out_shape=jax.ShapeDtypeStruct((N_HEADS // 2, batch * seq, 2 * HEAD_DIM), BF16),
        scratch_shapes=[
            pltpu.VMEM((len(DILATIONS), 2 * BLK, 2 * BLK), F32),
            pltpu.VMEM((len(DILATIONS) - 1, seq, 2 * HEAD_DIM), F32),
            pltpu.VMEM((len(DILATIONS) - 1, seq, 2 * HEAD_DIM), F32),
            pltpu.VMEM((len(DILATIONS) - 1, seq, 2 * HEAD_DIM), F32),
        ],
        compiler_params=_params(("arbitrary", "arbitrary")),
        name="attn_prompt",
    )(slopes, q, k, v)


N_NEW = 4
QROWS = N_NEW * N_HEADS


SAMPLE_TCHUNK = 512


def _attn_sample_tables(slope_ref, bias_s, mult_s):
    wb = bias_s.shape[1]
    n_of_row = lax.broadcasted_iota(jnp.int32, (QROWS, 1), 0) // N_HEADS
    t = lax.broadcasted_iota(jnp.int32, (QROWS, wb), 1)
    dist = wb + n_of_row - t
    mult = jnp.zeros((QROWS, wb), F32)
    for r in DILATIONS:
        hit = ((dist & (r - 1)) == 0) & (dist <= r * N_STRIDED)
        mult = mult + jnp.where(hit, 1.0, 0.0)
    mult_s[...] = mult
    bias_s[...] = jnp.where(mult > 0.0, (-LOG2E) * slope_ref[...] * dist.astype(F32), NEG)


def _attn_sample_step(slope_ref, q_ref, kn_ref, vn_ref, kt_ref, vt_ref, o_ref, bias_s, mult_s):
    W = ATTN_WIDTH
    wb = kt_ref.shape[2]
    chunks = [slice(c, c + SAMPLE_TCHUNK) for c in range(0, wb, SAMPLE_TCHUNK)]
    slope = slope_ref[...]
    n_of_row = lax.broadcasted_iota(jnp.int32, (QROWS, 1), 0) // N_HEADS
    q = q_ref[0]
    kn = kn_ref[0]
    vn = vn_ref[0]
    sub = lax.broadcasted_iota(jnp.int32, (N_HEADS, W), 0)
    own = sub == lax.broadcasted_iota(jnp.int32, (N_HEADS, W), 1) // HEAD_DIM
    own_rows = jnp.concatenate([own] * N_NEW, axis=0)
    qbd = jnp.concatenate(
        [jnp.where(own, jnp.broadcast_to(q[n:n + 1], (N_HEADS, W)), 0.0) for n in range(N_NEW)], axis=0)
    qbd16 = qbd.astype(BF16)

    ss = [jnp.dot(qbd16, kt_ref[0, :, c].astype(BF16), preferred_element_type=F32) + bias_s[:, c]
          for c in chunks]
    s_new, w_new = [], []
    for m in range(N_NEW):
        sm = jnp.sum(qbd * kn[m:m + 1], axis=-1, keepdims=True)
        gap = n_of_row - m
        s_new.append(jnp.where(gap >= 0, sm - LOG2E * slope * gap.astype(F32), NEG))
        w_new.append(jnp.where(gap == 0, float(len(DILATIONS)), jnp.where(gap > 0, 1.0, 0.0)))
    mx = s_new[0]
    for sm in s_new[1:] + [jnp.max(s, axis=-1, keepdims=True) for s in ss]:
        mx = jnp.maximum(mx, sm)
    den = jnp.zeros((QROWS, 1), F32)
    pv = jnp.zeros((QROWS, W), F32)
    for s, c in zip(ss, chunks):
        p = jnp.exp2(s - mx) * mult_s[:, c]
        den = den + jnp.sum(p, axis=-1, keepdims=True)
        pv = pv + lax.dot_general(p.astype(BF16), vt_ref[0, :, c].astype(BF16), (((1,), (1,)), ((), ())),
                                  preferred_element_type=F32)
    for m in range(N_NEW):
        pm = jnp.exp2(s_new[m] - mx) * w_new[m]
        den = den + pm
        pv = pv + pm * vn[m:m + 1]
    full = jnp.where(own_rows, pv / den, 0.0)
    o_ref[0] = jnp.sum(full.reshape(N_NEW, N_HEADS, W), axis=1).astype(o_ref.dtype)


def _pool_mix(d_groups, wp_ref, ps_ref):
    mixed = [jnp.dot(d.astype(BF16), wp_ref[g], preferred_element_type=F32) for g, d in enumerate(d_groups)]
    return jnp.concatenate(mixed, axis=-1) * ps_ref[...]


def _out_proj(x, attn, pool_out, wo_ref):
    h = x + jnp.dot(attn, wo_ref[:ATTN_WIDTH, :], preferred_element_type=F32)
    return h + jnp.dot(pool_out.astype(BF16), wo_ref[ATTN_WIDTH:, :], preferred_element_type=F32)


def _mix_prompt_tile(x_ref, a_ref, u_ref, wp_ref, ps_ref, wo_ref, e_s, s2_s, s4_s, s8_s, tiles_per_seq):
    i = pl.program_id(0)
    tm = u_ref.shape[0]
    H = POOL_HEAD
    G = POOL_GROUP

    @pl.when(i == 0)
    def _():
        e_s[0:8, :] = jnp.zeros((8, POOL_WIDTH), F32)
        s2_s[0:8, :] = jnp.zeros((8, POOL_WIDTH), F32)
        s4_s[0:8, :] = jnp.zeros((8, 3 * G), F32)
        s8_s[0:8, :] = jnp.zeros((8, 2 * G), F32)

    @pl.when(i % tiles_per_seq == 0)
    def _():
        e_s[8:H, :] = jnp.zeros((H - 8, POOL_WIDTH), F32)

    u = u_ref[...]
    e_s[H:, :] = u
    n = tm + H - 8
    s2_s[8:, :] = e_s[8:, :] + e_s[7:7 + n, :]
    s4_s[8:, :] = s2_s[8:, G:] + s2_s[6:6 + n, G:]
    s8_s[8:, :] = s4_s[8:, G:] + s4_s[4:4 + n, G:]
    sums = [
        s2_s[H:, :G],
        s4_s[H:, :G],
        s8_s[H:, :G],
        s8_s[H:, G:] + s8_s[H - 8:H - 8 + tm, G:],
    ]
    pos = (i % tiles_per_seq) * tm + lax.broadcasted_iota(jnp.int32, (tm, 1), 0)
    d = []
    for g, w in enumerate(POOL_WINDOWS):
        cnt = jnp.minimum(w, pos + 1).astype(F32)
        d.append(sums[g] / cnt - u[:, g * G:(g + 1) * G])
    pool_out = _pool_mix(d, wp_ref, ps_ref)
    attn = jnp.concatenate([a_ref[hp] for hp in range(a_ref.shape[0])], axis=1)
    h = _out_proj(x_ref[...], attn, pool_out, wo_ref)
    e_s[8:H, :] = e_s[tm + 8:tm + H, :]
    return h


def _mix_sample_kernel(x_ref, a_ref, u_ref, st_ref, wp_ref, ps_ref, wo_ref, h_ref):
    G = POOL_GROUP
    PW = POOL_WIDTH

    def row(t, lanes):
        if t < POOL_HIST:
            return st_ref[t, :, lanes]
        return u_ref[:, (t - POOL_HIST) * PW + lanes.start:(t - POOL_HIST) * PW + lanes.stop]

    for n in range(N_NEW):
        t = POOL_HIST + n
        d = []
        for g, w in enumerate(POOL_WINDOWS):
            lanes = slice(g * G, (g + 1) * G)
            cur = row(t, lanes)
            tot = cur
            for j in range(1, w):
                tot = tot + row(t - j, lanes)
            d.append(tot / float(min(w, PAST_LEN + n + 1)) - cur)
        pool_out = _pool_mix(d, wp_ref, ps_ref)
        cols = slice(n * D_MODEL, (n + 1) * D_MODEL)
        attn = a_ref[:, n * ATTN_WIDTH:(n + 1) * ATTN_WIDTH]
        h_ref[:, cols] = _out_proj(x_ref[:, cols], attn, pool_out, wo_ref)


def _mix_sample(x, attn, u, state_pool, w_pool, pool_scale, w_out):
    nseq = x.shape[0]
    full = lambda a: pl.BlockSpec(a.shape, lambda i: (0,) * a.ndim)
    args = (x, attn, u, state_pool, w_pool, pool_scale, w_out)
    return pl.pallas_call(
        _mix_sample_kernel,
        grid=(1,),
        in_specs=[full(a) for a in args],
        out_specs=pl.BlockSpec((nseq, N_NEW * D_MODEL), lambda i: (0, 0)),
        out_shape=jax.ShapeDtypeStruct((nseq, N_NEW * D_MODEL), F32),
        compiler_params=_params(("arbitrary",)),
        name="mix_sample",
    )(*args)


def _rms(h, g_ref):
    return (h * lax.rsqrt(jnp.mean(h * h, axis=-1, keepdims=True) + EPS) * g_ref[...]).astype(BF16)


def _silu_gate(gate, val):
    return (gate / (1.0 + jnp.exp(-gate)) * val).astype(BF16)


def _tail_prompt_kernel(x_ref, a_ref, u_ref, wp_ref, ps_ref, wo_ref, g_ref, wu_ref, cw_ref, cb_ref, wd_ref,
                        slope_ref, qs_ref, kn_ref, vn_ref, kt_ref, vt_ref,
                        y_ref, hist_ref, os_ref,
                        e_s, s2_s, s4_s, s8_s, h_s, ext_s, carry_s, act_s, bias_s, mult_s, *, tiles_per_seq):
    i = pl.program_id(0)
    tm = x_ref.shape[0]
    C = TAIL_FF_CHUNK
    K1 = CONV_WIDTH - 1
    nr = tm // SUBLANES
    n_lane_blocks = D_MODEL // LANES

    @pl.when(i == 0)
    def _():
        _attn_sample_tables(slope_ref, bias_s, mult_s)

    @pl.when(i % tiles_per_seq == 0)
    def _():
        carry_s[...] = jnp.zeros(carry_s.shape, F32)

    h = _mix_prompt_tile(x_ref, a_ref, u_ref, wp_ref, ps_ref, wo_ref, e_s, s2_s, s4_s, s8_s, tiles_per_seq)
    for s in range(SUBLANES):
        for lb in range(n_lane_blocks):
            h_s[lb, pl.ds(s, nr, stride=SUBLANES), :] = h[s * nr:(s + 1) * nr, lb * LANES:(lb + 1) * LANES]
    _attn_sample_step(slope_ref, qs_ref, kn_ref, vn_ref, kt_ref, vt_ref, os_ref, bias_s, mult_s)
    hn = _rms(jnp.concatenate([h_s[lb] for lb in range(n_lane_blocks)], axis=1), g_ref)
    for c0 in range(0, D_FF, C):
        cw = min(C, D_FF - c0)
        first_sub = lax.broadcasted_iota(jnp.int32, (SUBLANES, cw), 0) == 0
        halves = []
        for half in range(2):
            cols = slice(half * D_FF + c0, half * D_FF + c0 + cw)
            up = jnp.dot(hn, wu_ref[:, cols], preferred_element_type=F32)
            ext = ext_s.at[half, :, :cw]
            for j in range(K1):
                rows = slice(j * SUBLANES, (j + 1) * SUBLANES)
                cur = up[(nr - K1 + j) * SUBLANES:(nr - K1 + j + 1) * SUBLANES]
                ext[rows, :] = jnp.where(first_sub, pltpu.roll(carry_s[rows, cols], 1, 0),
                                         pltpu.roll(cur, 1, 0))
            ext[K1 * SUBLANES:, :] = up
            conv = cb_ref[:, cols] + sum(
                ext[j * SUBLANES:j * SUBLANES + tm, :] * cw_ref[j:j + 1, cols] for j in range(CONV_WIDTH))
            carry_s[:, cols] = up[tm - K1 * SUBLANES:]
            halves.append(conv)
        act_s[:, c0:c0 + cw] = _silu_gate(halves[0], halves[1])
    y = (jnp.concatenate([h_s[lb] for lb in range(n_lane_blocks)], axis=1)
         + jnp.dot(act_s[...], wd_ref[...], preferred_element_type=F32))
    for lb in range(n_lane_blocks):
        h_s[lb] = y[:, lb * LANES:(lb + 1) * LANES]
    for s in range(SUBLANES):
        for lb in range(n_lane_blocks):
            y_ref[s * nr:(s + 1) * nr, lb * LANES:(lb + 1) * LANES] = h_s[lb, pl.ds(s, nr, stride=SUBLANES), :]

    @pl.when(i % tiles_per_seq == tiles_per_seq - 1)
    def _():
        hist_ref[0] = jnp.zeros(hist_ref.shape[1:], F32)
        for j in range(K1):
            hist_ref[0, j:j + 1, :] = carry_s[(j + 1) * SUBLANES - 1:(j + 1) * SUBLANES, :]


def _tail_prompt(x2d, attn, u, w_pool, pool_scale, w_out, g_ffn, w_up, conv_w, conv_b, w_down, seq,
                 slope_rows, q_s, kn_s, vn_s, cache_kt, cache_vt):
    n_tok = x2d.shape[0]
    nseq, W, wb = cache_kt.shape
    tm = n_tok // nseq
    assert tm * nseq == n_tok and tm % 8 == 0 and seq % tm == 0
    tps = seq // tm
    G = POOL_GROUP
    pool_rows = tm + POOL_HEAD
    new_spec = pl.BlockSpec((1, N_NEW, W), lambda i: (i, 0, 0))
    cache_spec = pl.BlockSpec((1, W, wb), lambda i: (i, 0, 0))
    return pl.pallas_call(
        functools.partial(_tail_prompt_kernel, tiles_per_seq=tps),
        grid=(nseq,),
        in_specs=[
            pl.BlockSpec((tm, D_MODEL), lambda i: (i, 0)),
            pl.BlockSpec((ATTN_WIDTH // LANES, tm, LANES), lambda i: (0, i, 0)),
            pl.BlockSpec((tm, POOL_WIDTH), lambda i: (i, 0)),
            _const_spec((len(POOL_WINDOWS), G, G)),
            _const_spec((1, POOL_WIDTH)),
            _const_spec((D_MODEL, D_MODEL)),
            _const_spec((1, D_MODEL)),
            _const_spec((D_MODEL, 2 * D_FF)),
            _const_spec((CONV_WIDTH, 2 * D_FF)),
            _const_spec((1, 2 * D_FF)),
            _const_spec((D_FF, D_MODEL)),
            _const_spec((QROWS, 1)),
            new_spec, new_spec, new_spec, cache_spec, cache_spec,
        ],
        out_specs=[
            pl.BlockSpec((tm, D_MODEL), lambda i: (i, 0)),
            pl.BlockSpec((1, SUBLANES, 2 * D_FF), lambda i: (i // tps, 0, 0)),
            new_spec,
        ],
        out_shape=[
            jax.ShapeDtypeStruct((n_tok, D_MODEL), F32),
            jax.ShapeDtypeStruct((n_tok // seq, SUBLANES, 2 * D_FF), F32),
            jax.ShapeDtypeStruct((nseq, N_NEW, W), BF16),
        ],
        scratch_shapes=[
            pltpu.VMEM((pool_rows, POOL_WIDTH), F32),
            pltpu.VMEM((pool_rows, POOL_WIDTH), F32),
            pltpu.VMEM((pool_rows, 3 * G), F32),
            pltpu.VMEM((pool_rows, 2 * G), F32),
            pltpu.VMEM((D_MODEL // LANES, tm, LANES), F32),
            pltpu.VMEM((2, tm + CONV_HEAD, TAIL_FF_CHUNK), F32),
            pltpu.VMEM((CONV_HEAD, 2 * D_FF), F32),
            pltpu.VMEM((tm, D_FF), BF16),
            pltpu.VMEM((QROWS, wb), F32),
            pltpu.VMEM((QROWS, wb), F32),
        ],
        compiler_params=_params(("arbitrary",)),
        name="tail_prompt",
    )(x2d, attn, u, w_pool, pool_scale, w_out, g_ffn, w_up, conv_w, conv_b, w_down,
      slope_rows, q_s, kn_s, vn_s, cache_kt, cache_vt)


def _ffn_sample_kernel(h_ref, st_ref, g_ref, wu_ref, cw_ref, cb_ref, wd_ref, y_ref, hist_ref):
    nseq = h_ref.shape[0]
    C = FF_CHUNK
    K = CONV_WIDTH
    hs = [h_ref[:, n * D_MODEL:(n + 1) * D_MODEL] for n in range(N_NEW)]
    hn = jnp.concatenate([_rms(h, g_ref) for h in hs], axis=0)
    for n in range(N_NEW):
        y_ref[:, n * D_MODEL:(n + 1) * D_MODEL] = hs[n]
    for c in range(N_FF_CHUNKS):
        halves = []
        for half in range(2):
            cols = slice(half * D_FF + c * C, half * D_FF + (c + 1) * C)
            up = jnp.dot(hn, wu_ref[:, cols], preferred_element_type=F32)
            rows = [st_ref[:, t, cols] for t in range(K - 1)]
            rows += [up[n * nseq:(n + 1) * nseq] for n in range(N_NEW)]
            conv = [cb_ref[:, cols] + sum(rows[n + j] * cw_ref[j:j + 1, cols] for j in range(K))
                    for n in range(N_NEW)]
            for t in range(K - 1):
                hist_ref[:, t, cols] = rows[N_NEW + t]
            halves.append(jnp.concatenate(conv, axis=0))
        act = _silu_gate(halves[0], halves[1])
        out = jnp.dot(act, wd_ref[c * C:(c + 1) * C, :], preferred_element_type=F32)
        for n in range(N_NEW):
            y_ref[:, n * D_MODEL:(n + 1) * D_MODEL] += out[n * nseq:(n + 1) * nseq]


def _ffn_sample(h, state_ffn, g_ffn, w_up, conv_w, conv_b, w_down):
    nseq = h.shape[0]
    full = lambda a: pl.BlockSpec(a.shape, lambda i: (0,) * a.ndim, pipeline_mode=pl.Buffered(1))
    args = (h, state_ffn, g_ffn, w_up, conv_w, conv_b, w_down)
    return pl.pallas_call(
        _ffn_sample_kernel,
        grid=(1,),
        in_specs=[full(a) for a in args],
        out_specs=[
            pl.BlockSpec((nseq, N_NEW * D_MODEL), lambda i: (0, 0)),
            pl.BlockSpec((nseq, CONV_WIDTH - 1, 2 * D_FF), lambda i: (0, 0, 0)),
        ],
        out_shape=[
            jax.ShapeDtypeStruct((nseq, N_NEW * D_MODEL), F32),
            jax.ShapeDtypeStruct((nseq, CONV_WIDTH - 1, 2 * D_FF), F32),
        ],
        compiler_params=_params(("arbitrary",)),
        name="ffn_sample",
    )(*args)


def kernel(x_prompt, x_sample, cache_k, cache_v, state_pool, state_ffn_conv, g_attn_norm, w_in, g_q, g_k,
           w_pool, pool_scale, w_out, g_ffn_norm, w_up, conv_w, conv_b, w_down):
    depth = w_in.shape[0]
    assert depth == 1
    batch, seq, _ = x_prompt.shape
    nseq, n_new, _ = x_sample.shape
    assert n_new == N_NEW and seq % (16 * BLK) == 0 and seq % TM == 0
    assert cache_k.shape[2] == 16 * N_STRIDED

    slopes = jnp.asarray(2.0 ** (-8.0 * np.arange(1, N_HEADS + 1) / N_HEADS), dtype=F32)
    head_of_lane = np.arange(ATTN_WIDTH) // HEAD_DIM
    head_mean = jnp.asarray((head_of_lane[:, None] == head_of_lane[None, :]) / HEAD_DIM, dtype=BF16)

    l = 0
    w_in_b = w_in[l].astype(BF16)
    w_pool_b = w_pool[l].astype(BF16)
    w_out_b = w_out[l].astype(BF16)
    w_up_b = w_up[l].astype(BF16)
    w_down_b = w_down[l].astype(BF16)
    g_attn = g_attn_norm[l].reshape(1, D_MODEL)
    g_ffn = g_ffn_norm[l].reshape(1, D_MODEL)
    gq = g_q[l].reshape(1, ATTN_WIDTH)
    gk = g_k[l].reshape(1, ATTN_WIDTH)
    ps = pool_scale[l].reshape(1, POOL_WIDTH)
    cb = conv_b[l].reshape(1, 2 * D_FF)
    cw = conv_w[l]

    xs = x_sample.reshape(nseq, N_NEW * D_MODEL)
    qs, ks, vs, us = _proj(xs, nseq, N_NEW, lambda n: (0, n), lambda n: (0, n), (nseq, N_NEW * ATTN_WIDTH),
                           g_attn, w_in_b, gq, gk, head_mean)
    per_seq = lambda a: a.reshape(nseq, N_NEW, ATTN_WIDTH)
    cache_kt = jnp.transpose(cache_k[l], (0, 2, 3, 1)).reshape(nseq, ATTN_WIDTH, -1)
    cache_vt = jnp.transpose(cache_v[l], (0, 2, 3, 1)).reshape(nseq, ATTN_WIDTH, -1)
    slope_rows = jnp.tile(slopes, N_NEW).reshape(QROWS, 1)

    n_tok = batch * seq
    xp = x_prompt.reshape(n_tok, D_MODEL)
    qp, kp, vp, up, kpt, vpt = _proj(xp, TM, n_tok // TM, lambda i: (i, 0), lambda i: (i, 0),
                                     (n_tok, ATTN_WIDTH), g_attn, w_in_b, gq, gk, head_mean, seq=seq)
    attn_p = _attn_prompt(slopes, qp, kp, vp, batch, seq)
    yp, hist_p, attn_s = _tail_prompt(xp, attn_p, up, w_pool_b, ps, w_out_b, g_ffn, w_up_b, cw, cb, w_down_b,
                                      seq, slope_rows, per_seq(qs), per_seq(ks), per_seq(vs), cache_kt, cache_vt)

    attn_s = attn_s.reshape(nseq, N_NEW * ATTN_WIDTH)
    st_pool = jnp.transpose(state_pool[l], (1, 0, 2))
    hs = _mix_sample(xs, attn_s, us, st_pool, w_pool_b, ps, w_out_b)
    ys, hist_s = _ffn_sample(hs, state_ffn_conv[l], g_ffn, w_up_b, cw, cb, w_down_b)

    n_keep = min(16 * N_STRIDED, seq)
    window = lambda t: jnp.transpose(
        t.reshape(batch, N_HEADS, HEAD_DIM, seq), (0, 3, 1, 2))[None, :, seq - n_keep:]
    kp5, vp5 = window(kpt), window(vpt)
    new_pool_p = up.reshape(batch, seq, POOL_WIDTH)[None, :, seq - POOL_HIST:]
    new_ffn_p = hist_p[None, :, :CONV_WIDTH - 1]
    u_time = jnp.transpose(us.reshape(nseq, N_NEW, POOL_WIDTH), (1, 0, 2))
    new_pool_s = jnp.transpose(jnp.concatenate([st_pool[N_NEW:], u_time], axis=0), (1, 0, 2))[None]
    return (
        yp.reshape(batch, seq, D_MODEL),
        ys.reshape(nseq, N_NEW, D_MODEL),
        kp5, vp5, new_pool_p, new_ffn_p,
        ks.reshape(1, nseq, N_NEW, N_HEADS, HEAD_DIM),
        vs.reshape(1, nseq, N_NEW, N_HEADS, HEAD_DIM),
        new_pool_s,
        hist_s.reshape(1, nseq, CONV_WIDTH - 1, 2 * D_FF),
    )
```

```python
import functools

import numpy as np
import jax
import jax.numpy as jnp
from jax import lax
from jax.experimental import pallas as pl
from jax.experimental.pallas import tpu as pltpu

D_MODEL = 1024
HEAD_DIM = 64
N_HEADS = 8
ATTN_WIDTH = N_HEADS * HEAD_DIM
POOL_WIDTH = 512
POOL_WINDOWS = (2, 4, 8, 16)
POOL_GROUP = 128
POOL_HIST = 15
PROJ_WIDTH = 3 * ATTN_WIDTH + POOL_WIDTH
DILATIONS = (1, 4, 16)
N_STRIDED = 128
BLK = 128
D_FF = 2816
CONV_WIDTH = 3
PAST_LEN = 8192
EPS = 1e-6
NEG = -1e30
LOG2E = 1.4426950408889634

F32 = jnp.float32
BF16 = jnp.bfloat16

SUBLANES = 8
LANES = 128
VMEM_LIMIT = 56 * 1024 * 1024

TM = 1024
ATTN_UNITS = 8
ATTN_MERGE_UNITS = 8
FF_CHUNK = 256
N_FF_CHUNKS = D_FF // FF_CHUNK
TAIL_FF_CHUNK = 256
POOL_HEAD = 24
CONV_HEAD = (CONV_WIDTH - 1) * SUBLANES


def _params(sem, vmem=VMEM_LIMIT):
    return pltpu.CompilerParams(dimension_semantics=sem, vmem_limit_bytes=vmem)


def _const_spec(shape):
    nd = len(shape)
    return pl.BlockSpec(shape, lambda *_: (0,) * nd, pipeline_mode=pl.Buffered(1))


def _proj_kernel(x_ref, g_ref, w_ref, gq_ref, gk_ref, hm_ref, q_ref, k_ref, v_ref, u_ref, *kvt_refs):
    x = x_ref[...]
    xn = x * lax.rsqrt(jnp.mean(x * x, axis=-1, keepdims=True) + EPS) * g_ref[...]
    proj = jnp.dot(xn.astype(BF16), w_ref[...], preferred_element_type=F32)

    def head_norm(t, g):
        ms = jnp.dot((t * t).astype(BF16), hm_ref[...], preferred_element_type=F32)
        return t * lax.rsqrt(ms + EPS) * g

    def put(ref, val):
        if len(ref.shape) == 2:
            ref[...] = val
        else:
            for hp in range(ref.shape[0]):
                ref[hp] = val[:, hp * LANES:(hp + 1) * LANES]

    q = head_norm(proj[:, :ATTN_WIDTH], gq_ref[...])
    k = head_norm(proj[:, ATTN_WIDTH:2 * ATTN_WIDTH], gk_ref[...])
    v = proj[:, 2 * ATTN_WIDTH:3 * ATTN_WIDTH]
    put(q_ref, q * (HEAD_DIM ** -0.5 * LOG2E))
    put(k_ref, k)
    put(v_ref, v)
    u_ref[...] = proj[:, 3 * ATTN_WIDTH:]
    if kvt_refs:
        kvt_refs[0][0] = k.T
        kvt_refs[1][0] = v.T


def _proj(x2d, rows, n_steps, x_map, o_map, out_rows, g_attn, w_in, gq, gk, head_mean, seq=None):
    out = jax.ShapeDtypeStruct((out_rows[0], out_rows[1]), F32)
    out_specs = [pl.BlockSpec((rows, ATTN_WIDTH), o_map)] * 4
    out_shape = [out] * 4
    if seq is not None:
        pairs = ATTN_WIDTH // LANES
        out_specs[:3] = [pl.BlockSpec((pairs, rows, LANES), lambda i: (0, i, 0))] * 3
        out_shape[:3] = [jax.ShapeDtypeStruct((pairs, out_rows[0], LANES), F32)] * 3
        tps = seq // rows
        out_specs += [pl.BlockSpec((1, ATTN_WIDTH, rows), lambda i: (i // tps, 0, i % tps))] * 2
        out_shape += [jax.ShapeDtypeStruct((out_rows[0] // seq, ATTN_WIDTH, seq), F32)] * 2
    return pl.pallas_call(
        _proj_kernel,
        grid=(n_steps,),
        in_specs=[
            pl.BlockSpec((rows, D_MODEL), x_map),
            _const_spec((1, D_MODEL)),
            _const_spec((D_MODEL, PROJ_WIDTH)),
            _const_spec((1, ATTN_WIDTH)),
            _const_spec((1, ATTN_WIDTH)),
            _const_spec((ATTN_WIDTH, ATTN_WIDTH)),
        ],
        out_specs=out_specs,
        out_shape=out_shape,
        compiler_params=_params(("arbitrary",)),
        name="proj",
    )(x2d, g_attn, w_in, gq, gk, head_mean)


def _attn_prompt_kernel(slopes_ref, q_ref, k_ref, v_ref, o_ref, bias_s, acc_s, m_s, l_s):
    hp = pl.program_id(1)
    seq = q_ref.shape[0]
    lane = lax.broadcasted_iota(jnp.int32, (BLK, 2 * HEAD_DIM), 1)
    left = lane < HEAD_DIM

    row = lax.broadcasted_iota(jnp.int32, (2 * BLK, 2 * BLK), 0)
    col = lax.broadcasted_iota(jnp.int32, (2 * BLK, 2 * BLK), 1)
    diff = (row & (BLK - 1)) + BLK - col
    valid = (diff >= 0) & (diff <= N_STRIDED)
    slope = jnp.where(row < BLK, slopes_ref[2 * hp], slopes_ref[2 * hp + 1])
    for pi, r in enumerate(DILATIONS):
        bias_s[pi] = jnp.where(valid, (-LOG2E) * slope * (r * diff).astype(F32), NEG)

    def rows_of(start, r, n=BLK):
        return pl.ds(start, n) if r == 1 else pl.ds(start, n, stride=r)

    ones_cols = jnp.ones((2 * BLK, 2 * HEAD_DIM), BF16)

    def units(specs, with_prev):
        curs = [rows_of(st, r) for _, r, st in specs]
        qqs, kks, vvs = [], [], []
        for (_, r, st), cur in zip(specs, curs):
            q2 = q_ref[cur, :]
            qqs.append(
                jnp.concatenate([jnp.where(left, q2, 0.0), jnp.where(left, 0.0, q2)], axis=0).astype(BF16))
            if not with_prev:
                kk, vv = k_ref[cur, :], v_ref[cur, :]
            elif r == 1:
                both = pl.ds(st - BLK, 2 * BLK)
                kk, vv = k_ref[both, :], v_ref[both, :]
            else:
                prev = rows_of(st - BLK * r, r)
                kk = jnp.concatenate([k_ref[prev, :], k_ref[cur, :]], axis=0)
                vv = jnp.concatenate([v_ref[prev, :], v_ref[cur, :]], axis=0)
            kks.append(kk.astype(BF16))
            vvs.append(jnp.concatenate([vv.astype(BF16), ones_cols[:vv.shape[0]]], axis=1))
        ss = [lax.dot_general(qq, kk, (((1,), (1,)), ((), ())), preferred_element_type=F32)
              + (bias_s[pi] if with_prev else bias_s[pi, :, BLK:])
              for (pi, _, _), qq, kk in zip(specs, qqs, kks)]
        ms = [jnp.max(s, axis=-1, keepdims=True) for s in ss]
        ps = [jnp.exp2(s - m).astype(BF16) for s, m in zip(ss, ms)]
        pvs = [jnp.dot(p, vv, preferred_element_type=F32) for p, vv in zip(ps, vvs)]
        W2 = 2 * HEAD_DIM
        for (pi, _, _), cur, pv, m in zip(specs, curs, pvs, ms):
            acc = jnp.where(left, pv[:BLK, :W2], pv[BLK:, :W2])
            mt = jnp.where(left, m[:BLK], m[BLK:])
            lt = jnp.where(left, pv[:BLK, W2:], pv[BLK:, W2:])
            if pi > 0:
                acc_s[pi - 1, cur, :] = acc
                m_s[pi - 1, cur, :] = mt
                l_s[pi - 1, cur, :] = lt
                continue
            parts = [(acc, mt, lt)] + [(acc_s[j, cur, :], m_s[j, cur, :], l_s[j, cur, :])
                                       for j in range(len(DILATIONS) - 1)]
            mx = parts[0][1]
            for _, mj, _ in parts[1:]:
                mx = jnp.maximum(mx, mj)
            num = jnp.zeros((BLK, W2), F32)
            den = jnp.zeros((BLK, W2), F32)
            for aj, mj, lj in parts:
                w = jnp.exp2(mj - mx)
                num = num + w * aj
                den = den + w * lj
            o_ref[cur, :] = (num / den).astype(o_ref.dtype)

    U = ATTN_UNITS
    assert DILATIONS[0] == 1
    for pats, per_group in ((range(1, len(DILATIONS)), U), ((0,), ATTN_MERGE_UNITS)):
        first = [(pi, DILATIONS[pi], c) for pi in pats for c in range(DILATIONS[pi])]
        rest = [(pi, DILATIONS[pi], c + blk * (BLK * DILATIONS[pi])) for pi in pats
                for blk in range(1, seq // (DILATIONS[pi] * BLK)) for c in range(DILATIONS[pi])]
        for group, with_prev in ((first, False), (rest, True)):
            for i0 in range(0, len(group), per_group):
                units(group[i0:i0 + per_group], with_prev)


def _attn_prompt(slopes, q, k, v, batch, seq):
    blk = pl.BlockSpec((None, seq, 2 * HEAD_DIM), lambda b, h: (h, b, 0))
    return pl.pallas_call(
        _attn_prompt_kernel,
        grid=(batch, N_HEADS // 2),
        in_specs=[pl.BlockSpec(memory_space=pltpu.SMEM), blk, blk, blk],
        out_specs=blk,
        out_shape=jax.ShapeDtypeStruct((N_HEADS // 2, batch * seq, 2 * HEAD_DIM), BF16),
        scratch_shapes=[
            pltpu.VMEM((len(DILATIONS), 2 * BLK, 2 * BLK), F32),
            pltpu.VMEM((len(DILATIONS) - 1, seq, 2 * HEAD_DIM), F32),
            pltpu.VMEM((len(DILATIONS) - 1, seq, 2 * HEAD_DIM), F32),
            pltpu.VMEM((len(DILATIONS) - 1, seq, 2 * HEAD_DIM), F32),
        ],
        compiler_params=_params(("arbitrary", "arbitrary")),
        name="attn_prompt",
    )(slopes, q, k, v)


N_NEW = 4
QROWS = N_NEW * N_HEADS


SAMPLE_TCHUNK = 512


def _attn_sample_tables(slope_ref, bias_s, mult_s):
    wb = bias_s.shape[1]
    n_of_row = lax.broadcasted_iota(jnp.int32, (QROWS, 1), 0) // N_HEADS
    t = lax.broadcasted_iota(jnp.int32, (QROWS, wb), 1)
    dist = wb + n_of_row - t
    mult = jnp.zeros((QROWS, wb), F32)
    for r in DILATIONS:
        hit = ((dist & (r - 1)) == 0) & (dist <= r * N_STRIDED)
        mult = mult + jnp.where(hit, 1.0, 0.0)
    mult_s[...] = mult
    bias_s[...] = jnp.where(mult > 0.0, (-LOG2E) * slope_ref[...] * dist.astype(F32), NEG)


def _attn_sample_step(slope_ref, q_ref, kn_ref, vn_ref, kt_ref, vt_ref, o_ref, bias_s, mult_s):
    W = ATTN_WIDTH
    wb = kt_ref.shape[2]
    chunks = [slice(c, c + SAMPLE_TCHUNK) for c in range(0, wb, SAMPLE_TCHUNK)]
    slope = slope_ref[...]
    n_of_row = lax.broadcasted_iota(jnp.int32, (QROWS, 1), 0) // N_HEADS
    q = q_ref[0]
    kn = kn_ref[0]
    vn = vn_ref[0]
    sub = lax.broadcasted_iota(jnp.int32, (N_HEADS, W), 0)
    own = sub == lax.broadcasted_iota(jnp.int32, (N_HEADS, W), 1) // HEAD_DIM
    own_rows = jnp.concatenate([own] * N_NEW, axis=0)
    qbd = jnp.concatenate(
        [jnp.where(own, jnp.broadcast_to(q[n:n + 1], (N_HEADS, W)), 0.0) for n in range(N_NEW)], axis=0)
    qbd16 = qbd.astype(BF16)

    ss = [jnp.dot(qbd16, kt_ref[0, :, c].astype(BF16), preferred_element_type=F32) + bias_s[:, c]
          for c in chunks]
    s_new, w_new = [], []
    for m in range(N_NEW):
        sm = jnp.sum(qbd * kn[m:m + 1], axis=-1, keepdims=True)
        gap = n_of_row - m
        s_new.append(jnp.where(gap >= 0, sm - LOG2E * slope * gap.astype(F32), NEG))
        w_new.append(jnp.where(gap == 0, float(len(DILATIONS)), jnp.where(gap > 0, 1.0, 0.0)))
    mx = s_new[0]
    for sm in s_new[1:] + [jnp.max(s, axis=-1, keepdims=True) for s in ss]:
        mx = jnp.maximum(mx, sm)
    den = jnp.zeros((QROWS, 1), F32)
    pv = jnp.zeros((QROWS, W), F32)
    for s, c in zip(ss, chunks):
        p = jnp.exp2(s - mx) * mult_s[:, c]
        den = den + jnp.sum(p, axis=-1, keepdims=True)
        pv = pv + lax.dot_general(p.astype(BF16), vt_ref[0, :, c].astype(BF16), (((1,), (1,)), ((), ())),
                                  preferred_element_type=F32)
    for m in range(N_NEW):
        pm = jnp.exp2(s_new[m] - mx) * w_new[m]
        den = den + pm
        pv = pv + pm * vn[m:m + 1]
    full = jnp.where(own_rows, pv / den, 0.0)
    o_ref[0] = jnp.sum(full.reshape(N_NEW, N_HEADS, W), axis=1).astype(o_ref.dtype)


def _pool_mix(d_groups, wp_ref, ps_ref):
    mixed = [jnp.dot(d.astype(BF16), wp_ref[g], preferred_element_type=F32) for g, d in enumerate(d_groups)]
    return jnp.concatenate(mixed, axis=-1) * ps_ref[...]


def _out_proj(x, attn, pool_out, wo_ref):
    h = x + jnp.dot(attn, wo_ref[:ATTN_WIDTH, :], preferred_element_type=F32)
    return h + jnp.dot(pool_out.astype(BF16), wo_ref[ATTN_WIDTH:, :], preferred_element_type=F32)


def _mix_prompt_tile(x_ref, a_ref, u_ref, wp_ref, ps_ref, wo_ref, e_s, s2_s, s4_s, s8_s, tiles_per_seq):
    i = pl.program_id(0)
    tm = u_ref.shape[0]
    H = POOL_HEAD
    G = POOL_GROUP

    @pl.when(i == 0)
    def _():
        e_s[0:8, :] = jnp.zeros((8, POOL_WIDTH), F32)
        s2_s[0:8, :] = jnp.zeros((8, POOL_WIDTH), F32)
        s4_s[0:8, :] = jnp.zeros((8, 3 * G), F32)
        s8_s[0:8, :] = jnp.zeros((8, 2 * G), F32)

    @pl.when(i % tiles_per_seq == 0)
    def _():
        e_s[8:H, :] = jnp.zeros((H - 8, POOL_WIDTH), F32)

    u = u_ref[...]
    e_s[H:, :] = u
    n = tm + H - 8
    s2_s[8:, :] = e_s[8:, :] + e_s[7:7 + n, :]
    s4_s[8:, :] = s2_s[8:, G:] + s2_s[6:6 + n, G:]
    s8_s[8:, :] = s4_s[8:, G:] + s4_s[4:4 + n, G:]
    sums = [
        s2_s[H:, :G],
        s4_s[H:, :G],
        s8_s[H:, :G],
        s8_s[H:, G:] + s8_s[H - 8:H - 8 + tm, G:],
    ]
    pos = (i % tiles_per_seq) * tm + lax.broadcasted_iota(jnp.int32, (tm, 1), 0)
    d = []
    for g, w in enumerate(POOL_WINDOWS):
        cnt = jnp.minimum(w, pos + 1).astype(F32)
        d.append(sums[g] / cnt - u[:, g * G:(g + 1) * G])
    pool_out = _pool_mix(d, wp_ref, ps_ref)
    attn = jnp.concatenate([a_ref[hp] for hp in range(a_ref.shape[0])], axis=1)
    h = _out_proj(x_ref[...], attn, pool_out, wo_ref)
    e_s[8:H, :] = e_s[tm + 8:tm + H, :]
    return h


def _mix_sample_kernel(x_ref, a_ref, u_ref, st_ref, wp_ref, ps_ref, wo_ref, h_ref):
    G = POOL_GROUP
    PW = POOL_WIDTH

    def row(t, lanes):
        if t < POOL_HIST:
            return st_ref[t, :, lanes]
        return u_ref[:, (t - POOL_HIST) * PW + lanes.start:(t - POOL_HIST) * PW + lanes.stop]

    for n in range(N_NEW):
        t = POOL_HIST + n
        d = []
        for g, w in enumerate(POOL_WINDOWS):
            lanes = slice(g * G, (g + 1) * G)
            cur = row(t, lanes)
            tot = cur
            for j in range(1, w):
                tot = tot + row(t - j, lanes)
            d.append(tot / float(min(w, PAST_LEN + n + 1)) - cur)
        pool_out = _pool_mix(d, wp_ref, ps_ref)
        cols = slice(n * D_MODEL, (n + 1) * D_MODEL)
        attn = a_ref[:, n * ATTN_WIDTH:(n + 1) * ATTN_WIDTH]
        h_ref[:, cols] = _out_proj(x_ref[:, cols], attn, pool_out, wo_ref)


def _mix_sample(x, attn, u, state_pool, w_pool, pool_scale, w_out):
    nseq = x.shape[0]
    full = lambda a: pl.BlockSpec(a.shape, lambda i: (0,) * a.ndim)
    args = (x, attn, u, state_pool, w_pool, pool_scale, w_out)
    return pl.pallas_call(
        _mix_sample_kernel,
        grid=(1,),
        in_specs=[full(a) for a in args],
        out_specs=pl.BlockSpec((nseq, N_NEW * D_MODEL), lambda i: (0, 0)),
        out_shape=jax.ShapeDtypeStruct((nseq, N_NEW * D_MODEL), F32),
        compiler_params=_params(("arbitrary",)),
        name="mix_sample",
    )(*args)


def _rms(h, g_ref):
    return (h * lax.rsqrt(jnp.mean(h * h, axis=-1, keepdims=True) + EPS) * g_ref[...]).astype(BF16)


def _silu_gate(gate, val):
    return (gate / (1.0 + jnp.exp(-gate)) * val).astype(BF16)


def _tail_prompt_kernel(x_ref, a_ref, u_ref, wp_ref, ps_ref, wo_ref, g_ref, wu_ref, cw_ref, cb_ref, wd_ref,
                        slope_ref, qs_ref, kn_ref, vn_ref, kt_ref, vt_ref,
                        y_ref, hist_ref, os_ref,
                        e_s, s2_s, s4_s, s8_s, h_s, ext_s, carry_s, act_s, bias_s, mult_s, *, tiles_per_seq):
    i = pl.program_id(0)
    tm = x_ref.shape[0]
    C = TAIL_FF_CHUNK
    K1 = CONV_WIDTH - 1
    nr = tm // SUBLANES
    n_lane_blocks = D_MODEL // LANES

    @pl.when(i == 0)
    def _():
        _attn_sample_tables(slope_ref, bias_s, mult_s)

    @pl.when(i % tiles_per_seq == 0)
    def _():
        carry_s[...] = jnp.zeros(carry_s.shape, F32)

    h = _mix_prompt_tile(x_ref, a_ref, u_ref, wp_ref, ps_ref, wo_ref, e_s, s2_s, s4_s, s8_s, tiles_per_seq)
    for s in range(SUBLANES):
        for lb in range(n_lane_blocks):
            h_s[lb, pl.ds(s, nr, stride=SUBLANES), :] = h[s * nr:(s + 1) * nr, lb * LANES:(lb + 1) * LANES]
    _attn_sample_step(slope_ref, qs_ref, kn_ref, vn_ref, kt_ref, vt_ref, os_ref, bias_s, mult_s)
    hn = _rms(jnp.concatenate([h_s[lb] for lb in range(n_lane_blocks)], axis=1), g_ref)
    for c0 in range(0, D_FF, C):
        cw = min(C, D_FF - c0)
        first_sub = lax.broadcasted_iota(jnp.int32, (SUBLANES, cw), 0) == 0
        halves = []
        for half in range(2):
            cols = slice(half * D_FF + c0, half * D_FF + c0 + cw)
            up = jnp.dot(hn, wu_ref[:, cols], preferred_element_type=F32)
            ext = ext_s.at[half, :, :cw]
            for j in range(K1):
                rows = slice(j * SUBLANES, (j + 1) * SUBLANES)
                cur = up[(nr - K1 + j) * SUBLANES:(nr - K1 + j + 1) * SUBLANES]
                ext[rows, :] = jnp.where(first_sub, pltpu.roll(carry_s[rows, cols], 1, 0),
                                         pltpu.roll(cur, 1, 0))
            ext[K1 * SUBLANES:, :] = up
            conv = cb_ref[:, cols] + sum(
                ext[j * SUBLANES:j * SUBLANES + tm, :] * cw_ref[j:j + 1, cols] for j in range(CONV_WIDTH))
            carry_s[:, cols] = up[tm - K1 * SUBLANES:]
            halves.append(conv)
        act_s[:, c0:c0 + cw] = _silu_gate(halves[0], halves[1])
    y = (jnp.concatenate([h_s[lb] for lb in range(n_lane_blocks)], axis=1)
         + jnp.dot(act_s[...], wd_ref[...], preferred_element_type=F32))
    for lb in range(n_lane_blocks):
        h_s[lb] = y[:, lb * LANES:(lb + 1) * LANES]
    for s in range(SUBLANES):
        for lb in range(n_lane_blocks):
            y_ref[s * nr:(s + 1) * nr, lb * LANES:(lb + 1) * LANES] = h_s[lb, pl.ds(s, nr, stride=SUBLANES), :]

    @pl.when(i % tiles_per_seq == tiles_per_seq - 1)
    def _():
        hist_ref[0] = jnp.zeros(hist_ref.shape[1:], F32)
        for j in range(K1):
            hist_ref[0, j:j + 1, :] = carry_s[(j + 1) * SUBLANES - 1:(j + 1) * SUBLANES, :]


def _tail_prompt(x2d, attn, u, w_pool, pool_scale, w_out, g_ffn, w_up, conv_w, conv_b, w_down, seq,
                 slope_rows, q_s, kn_s, vn_s, cache_kt, cache_vt):
    n_tok = x2d.shape[0]
    nseq, W, wb = cache_kt.shape
    tm = n_tok // nseq
    assert tm * nseq == n_tok and tm % 8 == 0 and seq % tm == 0
    tps = seq // tm
    G = POOL_GROUP
    pool_rows = tm + POOL_HEAD
    new_spec = pl.BlockSpec((1, N_NEW, W), lambda i: (i, 0, 0))
    cache_spec = pl.BlockSpec((1, W, wb), lambda i: (i, 0, 0))
    return pl.pallas_call(
        functools.partial(_tail_prompt_kernel, tiles_per_seq=tps),
        grid=(nseq,),
        in_specs=[
            pl.BlockSpec((tm, D_MODEL), lambda i: (i, 0)),
            pl.BlockSpec((ATTN_WIDTH // LANES, tm, LANES), lambda i: (0, i, 0)),
            pl.BlockSpec((tm, POOL_WIDTH), lambda i: (i, 0)),
            _const_spec((len(POOL_WINDOWS), G, G)),
            _const_spec((1, POOL_WIDTH)),
            _const_spec((D_MODEL, D_MODEL)),
            _const_spec((1, D_MODEL)),
            _const_spec((D_MODEL, 2 * D_FF)),
            _const_spec((CONV_WIDTH, 2 * D_FF)),
            _const_spec((1, 2 * D_FF)),
            _const_spec((D_FF, D_MODEL)),
            _const_spec((QROWS, 1)),
            new_spec, new_spec, new_spec, cache_spec, cache_spec,
        ],
        out_specs=[
            pl.BlockSpec((tm, D_MODEL), lambda i: (i, 0)),
            pl.BlockSpec((1, SUBLANES, 2 * D_FF), lambda i: (i // tps, 0, 0)),
            new_spec,
        ],
        out_shape=[
            jax.ShapeDtypeStruct((n_tok, D_MODEL), F32),
            jax.ShapeDtypeStruct((n_tok // seq, SUBLANES, 2 * D_FF), F32),
            jax.ShapeDtypeStruct((nseq, N_NEW, W), BF16),
        ],
        scratch_shapes=[
            pltpu.VMEM((pool_rows, POOL_WIDTH), F32),
            pltpu.VMEM((pool_rows, POOL_WIDTH), F32),
            pltpu.VMEM((pool_rows, 3 * G), F32),
            pltpu.VMEM((pool_rows, 2 * G), F32),
            pltpu.VMEM((D_MODEL // LANES, tm, LANES), F32),
            pltpu.VMEM((2, tm + CONV_HEAD, TAIL_FF_CHUNK), F32),
            pltpu.VMEM((CONV_HEAD, 2 * D_FF), F32),
            pltpu.VMEM((tm, D_FF), BF16),
            pltpu.VMEM((QROWS, wb), F32),
            pltpu.VMEM((QROWS, wb), F32),
        ],
        compiler_params=_params(("arbitrary",)),
        name="tail_prompt",
    )(x2d, attn, u, w_pool, pool_scale, w_out, g_ffn, w_up, conv_w, conv_b, w_down,
      slope_rows, q_s, kn_s, vn_s, cache_kt, cache_vt)


def _ffn_sample_kernel(h_ref, st_ref, g_ref, wu_ref, cw_ref, cb_ref, wd_ref, y_ref, hist_ref):
    nseq = h_ref.shape[0]
    C = FF_CHUNK
    K = CONV_WIDTH
    hs = [h_ref[:, n * D_MODEL:(n + 1) * D_MODEL] for n in range(N_NEW)]
    hn = jnp.concatenate([_rms(h, g_ref) for h in hs], axis=0)
    for n in range(N_NEW):
        y_ref[:, n * D_MODEL:(n + 1) * D_MODEL] = hs[n]
    for c in range(N_FF_CHUNKS):
        halves = []
        for half in range(2):
            cols = slice(half * D_FF + c * C, half * D_FF + (c + 1) * C)
            up = jnp.dot(hn, wu_ref[:, cols], preferred_element_type=F32)
            rows = [st_ref[:, t, cols] for t in range(K - 1)]
            rows += [up[n * nseq:(n + 1) * nseq] for n in range(N_NEW)]
            conv = [cb_ref[:, cols] + sum(rows[n + j] * cw_ref[j:j + 1, cols] for j in range(K))
                    for n in range(N_NEW)]
            for t in range(K - 1):
                hist_ref[:, t, cols] = rows[N_NEW + t]
            halves.append(jnp.concatenate(conv, axis=0))
        act = _silu_gate(halves[0], halves[1])
        out = jnp.dot(act, wd_ref[c * C:(c + 1) * C, :], preferred_element_type=F32)
        for n in range(N_NEW):
            y_ref[:, n * D_MODEL:(n + 1) * D_MODEL] += out[n * nseq:(n + 1) * nseq]


def _ffn_sample(h, state_ffn, g_ffn, w_up, conv_w, conv_b, w_down):
    nseq = h.shape[0]
    full = lambda a: pl.BlockSpec(a.shape, lambda i: (0,) * a.ndim, pipeline_mode=pl.Buffered(1))
    args = (h, state_ffn, g_ffn, w_up, conv_w, conv_b, w_down)
    return pl.pallas_call(
        _ffn_sample_kernel,
        grid=(1,),
        in_specs=[full(a) for a in args],
        out_specs=[
            pl.BlockSpec((nseq, N_NEW * D_MODEL), lambda i: (0, 0)),
            pl.BlockSpec((nseq, CONV_WIDTH - 1, 2 * D_FF), lambda i: (0, 0, 0)),
        ],
        out_shape=[
            jax.ShapeDtypeStruct((nseq, N_NEW * D_MODEL), F32),
            jax.ShapeDtypeStruct((nseq, CONV_WIDTH - 1, 2 * D_FF), F32),
        ],
        compiler_params=_params(("arbitrary",)),
        name="ffn_sample",
    )(*args)


def kernel(x_prompt, x_sample, cache_k, cache_v, state_pool, state_ffn_conv, g_attn_norm, w_in, g_q, g_k,
           w_pool, pool_scale, w_out, g_ffn_norm, w_up, conv_w, conv_b, w_down):
    depth = w_in.shape[0]
    assert depth == 1
    batch, seq, _ = x_prompt.shape
    nseq, n_new, _ = x_sample.shape
    assert n_new == N_NEW and seq % (16 * BLK) == 0 and seq % TM == 0
    assert cache_k.shape[2] == 16 * N_STRIDED

    slopes = jnp.asarray(2.0 ** (-8.0 * np.arange(1, N_HEADS + 1) / N_HEADS), dtype=F32)
    head_of_lane = np.arange(ATTN_WIDTH) // HEAD_DIM
    head_mean = jnp.asarray((head_of_lane[:, None] == head_of_lane[None, :]) / HEAD_DIM, dtype=BF16)

    l = 0
    w_in_b = w_in[l].astype(BF16)
    w_pool_b = w_pool[l].astype(BF16)
    w_out_b = w_out[l].astype(BF16)
    w_up_b = w_up[l].astype(BF16)
    w_down_b = w_down[l].astype(BF16)
    g_attn = g_attn_norm[l].reshape(1, D_MODEL)
    g_ffn = g_ffn_norm[l].reshape(1, D_MODEL)
    gq = g_q[l].reshape(1, ATTN_WIDTH)
    gk = g_k[l].reshape(1, ATTN_WIDTH)
    ps = pool_scale[l].reshape(1, POOL_WIDTH)
    cb = conv_b[l].reshape(1, 2 * D_FF)
    cw = conv_w[l]

    xs = x_sample.reshape(nseq, N_NEW * D_MODEL)
    qs, ks, vs, us = _proj(xs, nseq, N_NEW, lambda n: (0, n), lambda n: (0, n), (nseq, N_NEW * ATTN_WIDTH),
                           g_attn, w_in_b, gq, gk, head_mean)
    per_seq = lambda a: a.reshape(nseq, N_NEW, ATTN_WIDTH)
    cache_kt = jnp.transpose(cache_k[l], (0, 2, 3, 1)).reshape(nseq, ATTN_WIDTH, -1)
    cache_vt = jnp.transpose(cache_v[l], (0, 2, 3, 1)).reshape(nseq, ATTN_WIDTH, -1)
    slope_rows = jnp.tile(slopes, N_NEW).reshape(QROWS, 1)

    n_tok = batch * seq
    xp = x_prompt.reshape(n_tok, D_MODEL)
    qp, kp, vp, up, kpt, vpt = _proj(xp, TM, n_tok // TM, lambda i: (i, 0), lambda i: (i, 0),
                                     (n_tok, ATTN_WIDTH), g_attn, w_in_b, gq, gk, head_mean, seq=seq)
    attn_p = _attn_prompt(slopes, qp, kp, vp, batch, seq)
    yp, hist_p, attn_s = _tail_prompt(xp, attn_p, up, w_pool_b, ps, w_out_b, g_ffn, w_up_b, cw, cb, w_down_b,
                                      seq, slope_rows, per_seq(qs), per_seq(ks), per_seq(vs), cache_kt, cache_vt)

    attn_s = attn_s.reshape(nseq, N_NEW * ATTN_WIDTH)
    st_pool = jnp.transpose(state_pool[l], (1, 0, 2))
    hs = _mix_sample(xs, attn_s, us, st_pool, w_pool_b, ps, w_out_b)
    ys, hist_s = _ffn_sample(hs, state_ffn_conv[l], g_ffn, w_up_b, cw, cb, w_down_b)

    n_keep = min(16 * N_STRIDED, seq)
    window = lambda t: jnp.transpose(
        t.reshape(batch, N_HEADS, HEAD_DIM, seq), (0, 3, 1, 2))[None, :, seq - n_keep:]
    kp5, vp5 = window(kpt), window(vpt)
    new_pool_p = up.reshape(batch, seq, POOL_WIDTH)[None, :, seq - POOL_HIST:]
    new_ffn_p = hist_p[None, :, :CONV_WIDTH - 1]
    u_time = jnp.transpose(us.reshape(nseq, N_NEW, POOL_WIDTH), (1, 0, 2))
    new_pool_s = jnp.transpose(jnp.concatenate([st_pool[N_NEW:], u_time], axis=0), (1, 0, 2))[None]
    return (
        yp.reshape(batch, seq, D_MODEL),
        ys.reshape(nseq, N_NEW, D_MODEL),
        kp5, vp5, new_pool_p, new_ffn_p,
        ks.reshape(1, nseq, N_NEW, N_HEADS, HEAD_DIM),
        vs.reshape(1, nseq, N_NEW, N_HEADS, HEAD_DIM),
        new_pool_s,
        hist_s.reshape(1, nseq, CONV_WIDTH - 1, 2 * D_FF),
    )
```

```python
import functools

import numpy as np
import jax
import jax.numpy as jnp
from jax import lax
from jax.experimental import pallas as pl
from jax.experimental.pallas import tpu as pltpu

D_MODEL = 1024
HEAD_DIM = 64
N_HEADS = 8
ATTN_WIDTH = N_HEADS * HEAD_DIM
POOL_WIDTH = 512
POOL_WINDOWS = (2, 4, 8, 16)
POOL_GROUP = 128
POOL_HIST = 15
PROJ_WIDTH = 3 * ATTN_WIDTH + POOL_WIDTH
DILATIONS = (1, 4, 16)
N_STRIDED = 128
BLK = 128
D_FF = 2816
CONV_WIDTH = 3
PAST_LEN = 8192
EPS = 1e-6
NEG = -1e30
LOG2E = 1.4426950408889634

F32 = jnp.float32
BF16 = jnp.bfloat16

SUBLANES = 8
LANES = 128
VMEM_LIMIT = 56 * 1024 * 1024

TM = 1024
ATTN_UNITS = 4
ATTN_MERGE_UNITS = 4
FF_CHUNK = 256
N_FF_CHUNKS = D_FF // FF_CHUNK
TAIL_FF_CHUNK = 256
POOL_HEAD = 24
CONV_HEAD = (CONV_WIDTH - 1) * SUBLANES


def _params(sem, vmem=VMEM_LIMIT):
    return pltpu.CompilerParams(dimension_semantics=sem, vmem_limit_bytes=vmem)


def _const_spec(shape):
    nd = len(shape)
    return pl.BlockSpec(shape, lambda *_: (0,) * nd, pipeline_mode=pl.Buffered(1))


def _proj_kernel(x_ref, g_ref, w_ref, gq_ref, gk_ref, hm_ref, q_ref, k_ref, v_ref, u_ref, *kvt_refs):
    x = x_ref[...]
    xn = x * lax.rsqrt(jnp.mean(x * x, axis=-1, keepdims=True) + EPS) * g_ref[...]
    proj = jnp.dot(xn.astype(BF16), w_ref[...], preferred_element_type=F32)

    def head_norm(t, g):
        ms = jnp.dot((t * t).astype(BF16), hm_ref[...], preferred_element_type=F32)
        return t * lax.rsqrt(ms + EPS) * g

    def put(ref, val):
        if len(ref.shape) == 2:
            ref[...] = val
        else:
            for hp in range(ref.shape[0]):
                ref[hp] = val[:, hp * LANES:(hp + 1) * LANES]

    q = head_norm(proj[:, :ATTN_WIDTH], gq_ref[...])
    k = head_norm(proj[:, ATTN_WIDTH:2 * ATTN_WIDTH], gk_ref[...])
    v = proj[:, 2 * ATTN_WIDTH:3 * ATTN_WIDTH]
    put(q_ref, q * (HEAD_DIM ** -0.5 * LOG2E))
    put(k_ref, k)
    put(v_ref, v)
    u_ref[...] = proj[:, 3 * ATTN_WIDTH:]
    if kvt_refs:
        kvt_refs[0][0] = k.T
        kvt_refs[1][0] = v.T


def _proj(x2d, rows, n_steps, x_map, o_map, out_rows, g_attn, w_in, gq, gk, head_mean, seq=None):
    out = jax.ShapeDtypeStruct((out_rows[0], out_rows[1]), F32)
    out_specs = [pl.BlockSpec((rows, ATTN_WIDTH), o_map)] * 4
    out_shape = [out] * 4
    if seq is not None:
        pairs = ATTN_WIDTH // LANES
        out_specs[:3] = [pl.BlockSpec((pairs, rows, LANES), lambda i: (0, i, 0))] * 3
        out_shape[:3] = [jax.ShapeDtypeStruct((pairs, out_rows[0], LANES), F32)] * 3
        tps = seq // rows
        out_specs += [pl.BlockSpec((1, ATTN_WIDTH, rows), lambda i: (i // tps, 0, i % tps))] * 2
        out_shape += [jax.ShapeDtypeStruct((out_rows[0] // seq, ATTN_WIDTH, seq), F32)] * 2
    return pl.pallas_call(
        _proj_kernel,
        grid=(n_steps,),
        in_specs=[
            pl.BlockSpec((rows, D_MODEL), x_map),
            _const_spec((1, D_MODEL)),
            _const_spec((D_MODEL, PROJ_WIDTH)),
            _const_spec((1, ATTN_WIDTH)),
            _const_spec((1, ATTN_WIDTH)),
            _const_spec((ATTN_WIDTH, ATTN_WIDTH)),
        ],
        out_specs=out_specs,
        out_shape=out_shape,
        compiler_params=_params(("arbitrary",)),
        name="proj",
    )(x2d, g_attn, w_in, gq, gk, head_mean)


def _attn_prompt_kernel(slopes_ref, q_ref, k_ref, v_ref, o_ref, bias_s, acc_s, m_s, l_s):
    hp = pl.program_id(1)
    seq = q_ref.shape[0]
    lane = lax.broadcasted_iota(jnp.int32, (BLK, 2 * HEAD_DIM), 1)
    left = lane < HEAD_DIM

    row = lax.broadcasted_iota(jnp.int32, (2 * BLK, 2 * BLK), 0)
    col = lax.broadcasted_iota(jnp.int32, (2 * BLK, 2 * BLK), 1)
    diff = (row & (BLK - 1)) + BLK - col
    valid = (diff >= 0) & (diff <= N_STRIDED)
    slope = jnp.where(row < BLK, slopes_ref[2 * hp], slopes_ref[2 * hp + 1])
    for pi, r in enumerate(DILATIONS):
        bias_s[pi] = jnp.where(valid, (-LOG2E) * slope * (r * diff).astype(F32), NEG)

    def rows_of(start, r, n=BLK):
        return pl.ds(start, n) if r == 1 else pl.ds(start, n, stride=r)

    ones_cols = jnp.ones((2 * BLK, 2 * HEAD_DIM), BF16)

    def units(specs, with_prev):
        curs = [rows_of(st, r) for _, r, st in specs]
        qqs, kks, vvs = [], [], []
        for (_, r, st), cur in zip(specs, curs):
            q2 = q_ref[cur, :]
            qqs.append(
                jnp.concatenate([jnp.where(left, q2, 0.0), jnp.where(left, 0.0, q2)], axis=0).astype(BF16))
            if not with_prev:
                kk, vv = k_ref[cur, :], v_ref[cur, :]
            elif r == 1:
                both = pl.ds(st - BLK, 2 * BLK)
                kk, vv = k_ref[both, :], v_ref[both, :]
            else:
                prev = rows_of(st - BLK * r, r)
                kk = jnp.concatenate([k_ref[prev, :], k_ref[cur, :]], axis=0)
                vv = jnp.concatenate([v_ref[prev, :], v_ref[cur, :]], axis=0)
            kks.append(kk.astype(BF16))
            vvs.append(jnp.concatenate([vv.astype(BF16), ones_cols[:vv.shape[0]]], axis=1))
        ss = [lax.dot_general(qq, kk, (((1,), (1,)), ((), ())), preferred_element_type=F32)
              + (bias_s[pi] if with_prev else bias_s[pi, :, BLK:])
              for (pi, _, _), qq, kk in zip(specs, qqs, kks)]
        ms = [jnp.max(s, axis=-1, keepdims=True) for s in ss]
        ps = [jnp.exp2(s - m).astype(BF16) for s, m in zip(ss, ms)]
        pvs = [jnp.dot(p, vv, preferred_element_type=F32) for p, vv in zip(ps, vvs)]
        W2 = 2 * HEAD_DIM
        for (pi, _, _), cur, pv, m in zip(specs, curs, pvs, ms):
            acc = jnp.where(left, pv[:BLK, :W2], pv[BLK:, :W2])
            mt = jnp.where(left, m[:BLK], m[BLK:])
            lt = jnp.where(left, pv[:BLK, W2:], pv[BLK:, W2:])
            if pi > 0:
                acc_s[pi - 1, cur, :] = acc
                m_s[pi - 1, cur, :] = mt
                l_s[pi - 1, cur, :] = lt
                continue
            parts = [(acc, mt, lt)] + [(acc_s[j, cur, :], m_s[j, cur, :], l_s[j, cur, :])
                                       for j in range(len(DILATIONS) - 1)]
            mx = parts[0][1]
            for _, mj, _ in parts[1:]:
                mx = jnp.maximum(mx, mj)
            num = jnp.zeros((BLK, W2), F32)
            den = jnp.zeros((BLK, W2), F32)
            for aj, mj, lj in parts:
                w = jnp.exp2(mj - mx)
                num = num + w * aj
                den = den + w * lj
            o_ref[cur, :] = (num / den).astype(o_ref.dtype)

    U = ATTN_UNITS
    assert DILATIONS[0] == 1
    for pats, per_group in ((range(1, len(DILATIONS)), U), ((0,), ATTN_MERGE_UNITS)):
        first = [(pi, DILATIONS[pi], c) for pi in pats for c in range(DILATIONS[pi])]
        rest = [(pi, DILATIONS[pi], c + blk * (BLK * DILATIONS[pi])) for pi in pats
                for blk in range(1, seq // (DILATIONS[pi] * BLK)) for c in range(DILATIONS[pi])]
        for group, with_prev in ((first, False), (rest, True)):
            for i0 in range(0, len(group), per_group):
                units(group[i0:i0 + per_group], with_prev)


def _attn_prompt(slopes, q, k, v, batch, seq):
    blk = pl.BlockSpec((None, seq, 2 * HEAD_DIM), lambda b, h: (h, b, 0))
    return pl.pallas_call(
        _attn_prompt_kernel,
        grid=(batch, N_HEADS // 2),
        in_specs=[pl.BlockSpec(memory_space=pltpu.SMEM), blk, blk, blk],
        out_specs=blk,
        out_shape=jax.ShapeDtypeStruct((N_HEADS // 2, batch * seq, 2 * HEAD_DIM), BF16),
        scratch_shapes=[
            pltpu.VMEM((len(DILATIONS), 2 * BLK, 2 * BLK), F32),
            pltpu.VMEM((len(DILATIONS) - 1, seq, 2 * HEAD_DIM), F32),
            pltpu.VMEM((len(DILATIONS) - 1, seq, 2 * HEAD_DIM), F32),
            pltpu.VMEM((len(DILATIONS) - 1, seq, 2 * HEAD_DIM), F32),
        ],
        compiler_params=_params(("arbitrary", "arbitrary")),
        name="attn_prompt",
    )(slopes, q, k, v)


N_NEW = 4
QROWS = N_NEW * N_HEADS


SAMPLE_TCHUNK = 512


def _attn_sample_tables(slope_ref, bias_s, mult_s):
    wb = bias_s.shape[1]
    n_of_row = lax.broadcasted_iota(jnp.int32, (QROWS, 1), 0) // N_HEADS
    t = lax.broadcasted_iota(jnp.int32, (QROWS, wb), 1)
    dist = wb + n_of_row - t
    mult = jnp.zeros((QROWS, wb), F32)
    for r in DILATIONS:
        hit = ((dist & (r - 1)) == 0) & (dist <= r * N_STRIDED)
        mult = mult + jnp.where(hit, 1.0, 0.0)
    mult_s[...] = mult
    bias_s[...] = jnp.where(mult > 0.0, (-LOG2E) * slope_ref[...] * dist.astype(F32), NEG)


def _attn_sample_step(slope_ref, q_ref, kn_ref, vn_ref, kt_ref, vt_ref, o_ref, bias_s, mult_s):
    W = ATTN_WIDTH
    wb = kt_ref.shape[2]
    chunks = [slice(c, c + SAMPLE_TCHUNK) for c in range(0, wb, SAMPLE_TCHUNK)]
    slope = slope_ref[...]
    n_of_row = lax.broadcasted_iota(jnp.int32, (QROWS, 1), 0) // N_HEADS
    q = q_ref[0]
    kn = kn_ref[0]
    vn = vn_ref[0]
    sub = lax.broadcasted_iota(jnp.int32, (N_HEADS, W), 0)
    own = sub == lax.broadcasted_iota(jnp.int32, (N_HEADS, W), 1) // HEAD_DIM
    own_rows = jnp.concatenate([own] * N_NEW, axis=0)
    qbd = jnp.concatenate(
        [jnp.where(own, jnp.broadcast_to(q[n:n + 1], (N_HEADS, W)), 0.0) for n in range(N_NEW)], axis=0)
    qbd16 = qbd.astype(BF16)

    ss = [jnp.dot(qbd16, kt_ref[0, :, c].astype(BF16), preferred_element_type=F32) + bias_s[:, c]
          for c in chunks]
    s_new, w_new = [], []
    for m in range(N_NEW):
        sm = jnp.sum(qbd * kn[m:m + 1], axis=-1, keepdims=True)
        gap = n_of_row - m
        s_new.append(jnp.where(gap >= 0, sm - LOG2E * slope * gap.astype(F32), NEG))
        w_new.append(jnp.where(gap == 0, float(len(DILATIONS)), jnp.where(gap > 0, 1.0, 0.0)))
    mx = s_new[0]
    for sm in s_new[1:] + [jnp.max(s, axis=-1, keepdims=True) for s in ss]:
        mx = jnp.maximum(mx, sm)
    den = jnp.zeros((QROWS, 1), F32)
    pv = jnp.zeros((QROWS, W), F32)
    for s, c in zip(ss, chunks):
        p = jnp.exp2(s - mx) * mult_s[:, c]
        den = den + jnp.sum(p, axis=-1, keepdims=True)
        pv = pv + lax.dot_general(p.astype(BF16), vt_ref[0, :, c].astype(BF16), (((1,), (1,)), ((), ())),
                                  preferred_element_type=F32)
    for m in range(N_NEW):
        pm = jnp.exp2(s_new[m] - mx) * w_new[m]
        den = den + pm
        pv = pv + pm * vn[m:m + 1]
    full = jnp.where(own_rows, pv / den, 0.0)
    o_ref[0] = jnp.sum(full.reshape(N_NEW, N_HEADS, W), axis=1).astype(o_ref.dtype)


def _pool_mix(d_groups, wp_ref, ps_ref):
    mixed = [jnp.dot(d.astype(BF16), wp_ref[g], preferred_element_type=F32) for g, d in enumerate(d_groups)]
    return jnp.concatenate(mixed, axis=-1) * ps_ref[...]


def _out_proj(x, attn, pool_out, wo_ref):
    h = x + jnp.dot(attn, wo_ref[:ATTN_WIDTH, :], preferred_element_type=F32)
    return h + jnp.dot(pool_out.astype(BF16), wo_ref[ATTN_WIDTH:, :], preferred_element_type=F32)


def _mix_prompt_tile(x_ref, a_ref, u_ref, wp_ref, ps_ref, wo_ref, e_s, s2_s, s4_s, s8_s, tiles_per_seq):
    i = pl.program_id(0)
    tm = u_ref.shape[0]
    H = POOL_HEAD
    G = POOL_GROUP

    @pl.when(i == 0)
    def _():
        e_s[0:8, :] = jnp.zeros((8, POOL_WIDTH), F32)
        s2_s[0:8, :] = jnp.zeros((8, POOL_WIDTH), F32)
        s4_s[0:8, :] = jnp.zeros((8, 3 * G), F32)
        s8_s[0:8, :] = jnp.zeros((8, 2 * G), F32)

    @pl.when(i % tiles_per_seq == 0)
    def _():
        e_s[8:H, :] = jnp.zeros((H - 8, POOL_WIDTH), F32)

    u = u_ref[...]
    e_s[H:, :] = u
    n = tm + H - 8
    s2_s[8:, :] = e_s[8:, :] + e_s[7:7 + n, :]
    s4_s[8:, :] = s2_s[8:, G:] + s2_s[6:6 + n, G:]
    s8_s[8:, :] = s4_s[8:, G:] + s4_s[4:4 + n, G:]
    sums = [
        s2_s[H:, :G],
        s4_s[H:, :G],
        s8_s[H:, :G],
        s8_s[H:, G:] + s8_s[H - 8:H - 8 + tm, G:],
    ]
    pos = (i % tiles_per_seq) * tm + lax.broadcasted_iota(jnp.int32, (tm, 1), 0)
    d = []
    for g, w in enumerate(POOL_WINDOWS):
        cnt = jnp.minimum(w, pos + 1).astype(F32)
        d.append(sums[g] / cnt - u[:, g * G:(g + 1) * G])
    pool_out = _pool_mix(d, wp_ref, ps_ref)
    attn = jnp.concatenate([a_ref[hp] for hp in range(a_ref.shape[0])], axis=1)
    h = _out_proj(x_ref[...], attn, pool_out, wo_ref)
    e_s[8:H, :] = e_s[tm + 8:tm + H, :]
    return h


def _mix_sample_kernel(x_ref, a_ref, u_ref, st_ref, wp_ref, ps_ref, wo_ref, h_ref):
    G = POOL_GROUP
    PW = POOL_WIDTH

    def row(t, lanes):
        if t < POOL_HIST:
            return st_ref[t, :, lanes]
        return u_ref[:, (t - POOL_HIST) * PW + lanes.start:(t - POOL_HIST) * PW + lanes.stop]

    for n in range(N_NEW):
        t = POOL_HIST + n
        d = []
        for g, w in enumerate(POOL_WINDOWS):
            lanes = slice(g * G, (g + 1) * G)
            cur = row(t, lanes)
            tot = cur
            for j in range(1, w):
                tot = tot + row(t - j, lanes)
            d.append(tot / float(min(w, PAST_LEN + n + 1)) - cur)
        pool_out = _pool_mix(d, wp_ref, ps_ref)
        cols = slice(n * D_MODEL, (n + 1) * D_MODEL)
        attn = a_ref[:, n * ATTN_WIDTH:(n + 1) * ATTN_WIDTH]
        h_ref[:, cols] = _out_proj(x_ref[:, cols], attn, pool_out, wo_ref)


def _mix_sample(x, attn, u, state_pool, w_pool, pool_scale, w_out):
    nseq = x.shape[0]
    full = lambda a: pl.BlockSpec(a.shape, lambda i: (0,) * a.ndim)
    args = (x, attn, u, state_pool, w_pool, pool_scale, w_out)
    return pl.pallas_call(
        _mix_sample_kernel,
        grid=(1,),
        in_specs=[full(a) for a in args],
        out_specs=pl.BlockSpec((nseq, N_NEW * D_MODEL), lambda i: (0, 0)),
        out_shape=jax.ShapeDtypeStruct((nseq, N_NEW * D_MODEL), F32),
        compiler_params=_params(("arbitrary",)),
        name="mix_sample",
    )(*args)


def _rms(h, g_ref):
    return (h * lax.rsqrt(jnp.mean(h * h, axis=-1, keepdims=True) + EPS) * g_ref[...]).astype(BF16)


def _silu_gate(gate, val):
    return (gate / (1.0 + jnp.exp(-gate)) * val).astype(BF16)


def _tail_prompt_kernel(x_ref, a_ref, u_ref, wp_ref, ps_ref, wo_ref, g_ref, wu_ref, cw_ref, cb_ref, wd_ref,
                        slope_ref, qs_ref, kn_ref, vn_ref, kt_ref, vt_ref,
                        y_ref, hist_ref, os_ref,
                        e_s, s2_s, s4_s, s8_s, h_s, ext_s, carry_s, act_s, bias_s, mult_s, *, tiles_per_seq):
    i = pl.program_id(0)
    tm = x_ref.shape[0]
    C = TAIL_FF_CHUNK
    K1 = CONV_WIDTH - 1
    nr = tm // SUBLANES
    n_lane_blocks = D_MODEL // LANES

    @pl.when(i == 0)
    def _():
        _attn_sample_tables(slope_ref, bias_s, mult_s)

    @pl.when(i % tiles_per_seq == 0)
    def _():
        carry_s[...] = jnp.zeros(carry_s.shape, F32)

    h = _mix_prompt_tile(x_ref, a_ref, u_ref, wp_ref, ps_ref, wo_ref, e_s, s2_s, s4_s, s8_s, tiles_per_seq)
    for s in range(SUBLANES):
        for lb in range(n_lane_blocks):
            h_s[lb, pl.ds(s, nr, stride=SUBLANES), :] = h[s * nr:(s + 1) * nr, lb * LANES:(lb + 1) * LANES]
    _attn_sample_step(slope_ref, qs_ref, kn_ref, vn_ref, kt_ref, vt_ref, os_ref, bias_s, mult_s)
    hn = _rms(jnp.concatenate([h_s[lb] for lb in range(n_lane_blocks)], axis=1), g_ref)
    for c0 in range(0, D_FF, C):
        cw = min(C, D_FF - c0)
        first_sub = lax.broadcasted_iota(jnp.int32, (SUBLANES, cw), 0) == 0
        halves = []
        for half in range(2):
            cols = slice(half * D_FF + c0, half * D_FF + c0 + cw)
            up = jnp.dot(hn, wu_ref[:, cols], preferred_element_type=F32)
            ext = ext_s.at[half, :, :cw]
            for j in range(K1):
                rows = slice(j * SUBLANES, (j + 1) * SUBLANES)
                cur = up[(nr - K1 + j) * SUBLANES:(nr - K1 + j + 1) * SUBLANES]
                ext[rows, :] = jnp.where(first_sub, pltpu.roll(carry_s[rows, cols], 1, 0),
                                         pltpu.roll(cur, 1, 0))
            ext[K1 * SUBLANES:, :] = up
            conv = cb_ref[:, cols] + sum(
                ext[j * SUBLANES:j * SUBLANES + tm, :] * cw_ref[j:j + 1, cols] for j in range(CONV_WIDTH))
            carry_s[:, cols] = up[tm - K1 * SUBLANES:]
            halves.append(conv)
        act_s[:, c0:c0 + cw] = _silu_gate(halves[0], halves[1])
    y = (jnp.concatenate([h_s[lb] for lb in range(n_lane_blocks)], axis=1)
         + jnp.dot(act_s[...], wd_ref[...], preferred_element_type=F32))
    for lb in range(n_lane_blocks):
        h_s[lb] = y[:, lb * LANES:(lb + 1) * LANES]
    for s in range(SUBLANES):
        for lb in range(n_lane_blocks):
            y_ref[s * nr:(s + 1) * nr, lb * LANES:(lb + 1) * LANES] = h_s[lb, pl.ds(s, nr, stride=SUBLANES), :]

    @pl.when(i % tiles_per_seq == tiles_per_seq - 1)
    def _():
        hist_ref[0] = jnp.zeros(hist_ref.shape[1:], F32)
        for j in range(K1):
            hist_ref[0, j:j + 1, :] = carry_s[(j + 1) * SUBLANES - 1:(j + 1) * SUBLANES, :]


def _tail_prompt(x2d, attn, u, w_pool, pool_scale, w_out, g_ffn, w_up, conv_w, conv_b, w_down, seq,
                 slope_rows, q_s, kn_s, vn_s, cache_kt, cache_vt):
    n_tok = x2d.shape[0]
    nseq, W, wb = cache_kt.shape
    tm = n_tok // nseq
    assert tm * nseq == n_tok and tm % 8 == 0 and seq % tm == 0
    tps = seq // tm
    G = POOL_GROUP
    pool_rows = tm + POOL_HEAD
    new_spec = pl.BlockSpec((1, N_NEW, W), lambda i: (i, 0, 0))
    cache_spec = pl.BlockSpec((1, W, wb), lambda i: (i, 0, 0))
    return pl.pallas_call(
        functools.partial(_tail_prompt_kernel, tiles_per_seq=tps),
        grid=(nseq,),
        in_specs=[
            pl.BlockSpec((tm, D_MODEL), lambda i: (i, 0)),
            pl.BlockSpec((ATTN_WIDTH // LANES, tm, LANES), lambda i: (0, i, 0)),
            pl.BlockSpec((tm, POOL_WIDTH), lambda i: (i, 0)),
            _const_spec((len(POOL_WINDOWS), G, G)),
            _const_spec((1, POOL_WIDTH)),
            _const_spec((D_MODEL, D_MODEL)),
            _const_spec((1, D_MODEL)),
            _const_spec((D_MODEL, 2 * D_FF)),
            _const_spec((CONV_WIDTH, 2 * D_FF)),
            _const_spec((1, 2 * D_FF)),
            _const_spec((D_FF, D_MODEL)),
            _const_spec((QROWS, 1)),
            new_spec, new_spec, new_spec, cache_spec, cache_spec,
        ],
        out_specs=[
            pl.BlockSpec((tm, D_MODEL), lambda i: (i, 0)),
            pl.BlockSpec((1, SUBLANES, 2 * D_FF), lambda i: (i // tps, 0, 0)),
            new_spec,
        ],
        out_shape=[
            jax.ShapeDtypeStruct((n_tok, D_MODEL), F32),
            jax.ShapeDtypeStruct((n_tok // seq, SUBLANES, 2 * D_FF), F32),
            jax.ShapeDtypeStruct((nseq, N_NEW, W), BF16),
        ],
        scratch_shapes=[
            pltpu.VMEM((pool_rows, POOL_WIDTH), F32),
            pltpu.VMEM((pool_rows, POOL_WIDTH), F32),
            pltpu.VMEM((pool_rows, 3 * G), F32),
            pltpu.VMEM((pool_rows, 2 * G), F32),
            pltpu.VMEM((D_MODEL // LANES, tm, LANES), F32),
            pltpu.VMEM((2, tm + CONV_HEAD, TAIL_FF_CHUNK), F32),
            pltpu.VMEM((CONV_HEAD, 2 * D_FF), F32),
            pltpu.VMEM((tm, D_FF), BF16),
            pltpu.VMEM((QROWS, wb), F32),
            pltpu.VMEM((QROWS, wb), F32),
        ],
        compiler_params=_params(("arbitrary",)),
        name="tail_prompt",
    )(x2d, attn, u, w_pool, pool_scale, w_out, g_ffn, w_up, conv_w, conv_b, w_down,
      slope_rows, q_s, kn_s, vn_s, cache_kt, cache_vt)


def _ffn_sample_kernel(h_ref, st_ref, g_ref, wu_ref, cw_ref, cb_ref, wd_ref, y_ref, hist_ref):
    nseq = h_ref.shape[0]
    C = FF_CHUNK
    K = CONV_WIDTH
    hs = [h_ref[:, n * D_MODEL:(n + 1) * D_MODEL] for n in range(N_NEW)]
    hn = jnp.concatenate([_rms(h, g_ref) for h in hs], axis=0)
    for n in range(N_NEW):
        y_ref[:, n * D_MODEL:(n + 1) * D_MODEL] = hs[n]
    for c in range(N_FF_CHUNKS):
        halves = []
        for half in range(2):
            cols = slice(half * D_FF + c * C, half * D_FF + (c + 1) * C)
            up = jnp.dot(hn, wu_ref[:, cols], preferred_element_type=F32)
            rows = [st_ref[:, t, cols] for t in range(K - 1)]
            rows += [up[n * nseq:(n + 1) * nseq] for n in range(N_NEW)]
            conv = [cb_ref[:, cols] + sum(rows[n + j] * cw_ref[j:j + 1, cols] for j in range(K))
                    for n in range(N_NEW)]
            for t in range(K - 1):
                hist_ref[:, t, cols] = rows[N_NEW + t]
            halves.append(jnp.concatenate(conv, axis=0))
        act = _silu_gate(halves[0], halves[1])
        out = jnp.dot(act, wd_ref[c * C:(c + 1) * C, :], preferred_element_type=F32)
        for n in range(N_NEW):
            y_ref[:, n * D_MODEL:(n + 1) * D_MODEL] += out[n * nseq:(n + 1) * nseq]


def _ffn_sample(h, state_ffn, g_ffn, w_up, conv_w, conv_b, w_down):
    nseq = h.shape[0]
    full = lambda a: pl.BlockSpec(a.shape, lambda i: (0,) * a.ndim, pipeline_mode=pl.Buffered(1))
    args = (h, state_ffn, g_ffn, w_up, conv_w, conv_b, w_down)
    return pl.pallas_call(
        _ffn_sample_kernel,
        grid=(1,),
        in_specs=[full(a) for a in args],
        out_specs=[
            pl.BlockSpec((nseq, N_NEW * D_MODEL), lambda i: (0, 0)),
            pl.BlockSpec((nseq, CONV_WIDTH - 1, 2 * D_FF), lambda i: (0, 0, 0)),
        ],
        out_shape=[
            jax.ShapeDtypeStruct((nseq, N_NEW * D_MODEL), F32),
            jax.ShapeDtypeStruct((nseq, CONV_WIDTH - 1, 2 * D_FF), F32),
        ],
        compiler_params=_params(("arbitrary",)),
        name="ffn_sample",
    )(*args)


def kernel(x_prompt, x_sample, cache_k, cache_v, state_pool, state_ffn_conv, g_attn_norm, w_in, g_q, g_k,
           w_pool, pool_scale, w_out, g_ffn_norm, w_up, conv_w, conv_b, w_down):
    depth = w_in.shape[0]
    assert depth == 1
    batch, seq, _ = x_prompt.shape
    nseq, n_new, _ = x_sample.shape
    assert n_new == N_NEW and seq % (16 * BLK) == 0 and seq % TM == 0
    assert cache_k.shape[2] == 16 * N_STRIDED

    slopes = jnp.asarray(2.0 ** (-8.0 * np.arange(1, N_HEADS + 1) / N_HEADS), dtype=F32)
    head_of_lane = np.arange(ATTN_WIDTH) // HEAD_DIM
    head_mean = jnp.asarray((head_of_lane[:, None] == head_of_lane[None, :]) / HEAD_DIM, dtype=BF16)

    l = 0
    w_in_b = w_in[l].astype(BF16)
    w_pool_b = w_pool[l].astype(BF16)
    w_out_b = w_out[l].astype(BF16)
    w_up_b = w_up[l].astype(BF16)
    w_down_b = w_down[l].astype(BF16)
    g_attn = g_attn_norm[l].reshape(1, D_MODEL)
    g_ffn = g_ffn_norm[l].reshape(1, D_MODEL)
    gq = g_q[l].reshape(1, ATTN_WIDTH)
    gk = g_k[l].reshape(1, ATTN_WIDTH)
    ps = pool_scale[l].reshape(1, POOL_WIDTH)
    cb = conv_b[l].reshape(1, 2 * D_FF)
    cw = conv_w[l]

    xs = x_sample.reshape(nseq, N_NEW * D_MODEL)
    qs, ks, vs, us = _proj(xs, nseq, N_NEW, lambda n: (0, n), lambda n: (0, n), (nseq, N_NEW * ATTN_WIDTH),
                           g_attn, w_in_b, gq, gk, head_mean)
    per_seq = lambda a: a.reshape(nseq, N_NEW, ATTN_WIDTH)
    cache_kt = jnp.transpose(cache_k[l], (0, 2, 3, 1)).reshape(nseq, ATTN_WIDTH, -1)
    cache_vt = jnp.transpose(cache_v[l], (0, 2, 3, 1)).reshape(nseq, ATTN_WIDTH, -1)
    slope_rows = jnp.tile(slopes, N_NEW).reshape(QROWS, 1)

    n_tok = batch * seq
    xp = x_prompt.reshape(n_tok, D_MODEL)
    qp, kp, vp, up, kpt, vpt = _proj(xp, TM, n_tok // TM, lambda i: (i, 0), lambda i: (i, 0),
                                     (n_tok, ATTN_WIDTH), g_attn, w_in_b, gq, gk, head_mean, seq=seq)
    attn_p = _attn_prompt(slopes, qp, kp, vp, batch, seq)
    yp, hist_p, attn_s = _tail_prompt(xp, attn_p, up, w_pool_b, ps, w_out_b, g_ffn, w_up_b, cw, cb, w_down_b,
                                      seq, slope_rows, per_seq(qs), per_seq(ks), per_seq(vs), cache_kt, cache_vt)

    attn_s = attn_s.reshape(nseq, N_NEW * ATTN_WIDTH)
    st_pool = jnp.transpose(state_pool[l], (1, 0, 2))
    hs = _mix_sample(xs, attn_s, us, st_pool, w_pool_b, ps, w_out_b)
    ys, hist_s = _ffn_sample(hs, state_ffn_conv[l], g_ffn, w_up_b, cw, cb, w_down_b)

    n_keep = min(16 * N_STRIDED, seq)
    window = lambda t: jnp.transpose(
        t.reshape(batch, N_HEADS, HEAD_DIM, seq), (0, 3, 1, 2))[None, :, seq - n_keep:]
    kp5, vp5 = window(kpt), window(vpt)
    new_pool_p = up.reshape(batch, seq, POOL_WIDTH)[None, :, seq - POOL_HIST:]
    new_ffn_p = hist_p[None, :, :CONV_WIDTH - 1]
    u_time = jnp.transpose(us.reshape(nseq, N_NEW, POOL_WIDTH), (1, 0, 2))
    new_pool_s = jnp.transpose(jnp.concatenate([st_pool[N_NEW:], u_time], axis=0), (1, 0, 2))[None]
    return (
        yp.reshape(batch, seq, D_MODEL),
        ys.reshape(nseq, N_NEW, D_MODEL),
        kp5, vp5, new_pool_p, new_ffn_p,
        ks.reshape(1, nseq, N_NEW, N_HEADS, HEAD_DIM),
        vs.reshape(1, nseq, N_NEW, N_HEADS, HEAD_DIM),
        new_pool_s,
        hist_s.reshape(1, nseq, CONV_WIDTH - 1, 2 * D_FF),
    )
```

```python
import functools

import numpy as np
import jax
import jax.numpy as jnp
from jax import lax
from jax.experimental import pallas as pl
from jax.experimental.pallas import tpu as pltpu

D_MODEL = 1024
HEAD_DIM = 64
N_HEADS = 8
ATTN_WIDTH = N_HEADS * HEAD_DIM
POOL_WIDTH = 512
POOL_WINDOWS = (2, 4, 8, 16)
POOL_GROUP = 128
POOL_HIST = 15
PROJ_WIDTH = 3 * ATTN_WIDTH + POOL_WIDTH
DILATIONS = (1, 4, 16)
N_STRIDED = 128
BLK = 128
D_FF = 2816
CONV_WIDTH = 3
PAST_LEN = 8192
EPS = 1e-6
NEG = -1e30
LOG2E = 1.4426950408889634

F32 = jnp.float32
BF16 = jnp.bfloat16

SUBLANES = 8
LANES = 128
VMEM_LIMIT = 56 * 1024 * 1024

TM = 1024
ATTN_UNITS = 8
ATTN_MERGE_UNITS = 8
FF_CHUNK = 256
N_FF_CHUNKS = D_FF // FF_CHUNK
TAIL_FF_CHUNK = 512
POOL_HEAD = 24
CONV_HEAD = (CONV_WIDTH - 1) * SUBLANES


def _params(sem, vmem=VMEM_LIMIT):
    return pltpu.CompilerParams(dimension_semantics=sem, vmem_limit_bytes=vmem)


def _const_spec(shape):
    nd = len(shape)
    return pl.BlockSpec(shape, lambda *_: (0,) * nd, pipeline_mode=pl.Buffered(1))


def _proj_kernel(x_ref, g_ref, w_ref, gq_ref, gk_ref, hm_ref, q_ref, k_ref, v_ref, u_ref, *kvt_refs):
    x = x_ref[...]
    xn = x * lax.rsqrt(jnp.mean(x * x, axis=-1, keepdims=True) + EPS) * g_ref[...]
    proj = jnp.dot(xn.astype(BF16), w_ref[...], preferred_element_type=F32)

    def head_norm(t, g):
        ms = jnp.dot((t * t).astype(BF16), hm_ref[...], preferred_element_type=F32)
        return t * lax.rsqrt(ms + EPS) * g

    def put(ref, val):
        if len(ref.shape) == 2:
            ref[...] = val
        else:
            for hp in range(ref.shape[0]):
                ref[hp] = val[:, hp * LANES:(hp + 1) * LANES]

    q = head_norm(proj[:, :ATTN_WIDTH], gq_ref[...])
    k = head_norm(proj[:, ATTN_WIDTH:2 * ATTN_WIDTH], gk_ref[...])
    v = proj[:, 2 * ATTN_WIDTH:3 * ATTN_WIDTH]
    put(q_ref, q * (HEAD_DIM ** -0.5 * LOG2E))
    put(k_ref, k)
    put(v_ref, v)
    u_ref[...] = proj[:, 3 * ATTN_WIDTH:]
    if kvt_refs:
        kvt_refs[0][0] = k.T
        kvt_refs[1][0] = v.T


def _proj(x2d, rows, n_steps, x_map, o_map, out_rows, g_attn, w_in, gq, gk, head_mean, seq=None):
    out = jax.ShapeDtypeStruct((out_rows[0], out_rows[1]), F32)
    out_specs = [pl.BlockSpec((rows, ATTN_WIDTH), o_map)] * 4
    out_shape = [out] * 4
    if seq is not None:
        pairs = ATTN_WIDTH // LANES
        out_specs[:3] = [pl.BlockSpec((pairs, rows, LANES), lambda i: (0, i, 0))] * 3
        out_shape[:3] = [jax.ShapeDtypeStruct((pairs, out_rows[0], LANES), F32)] * 3
        tps = seq // rows
        out_specs += [pl.BlockSpec((1, ATTN_WIDTH, rows), lambda i: (i // tps, 0, i % tps))] * 2
        out_shape += [jax.ShapeDtypeStruct((out_rows[0] // seq, ATTN_WIDTH, seq), F32)] * 2
    return pl.pallas_call(
        _proj_kernel,
        grid=(n_steps,),
        in_specs=[
            pl.BlockSpec((rows, D_MODEL), x_map),
            _const_spec((1, D_MODEL)),
            _const_spec((D_MODEL, PROJ_WIDTH)),
            _const_spec((1, ATTN_WIDTH)),
            _const_spec((1, ATTN_WIDTH)),
            _const_spec((ATTN_WIDTH, ATTN_WIDTH)),
        ],
        out_specs=out_specs,
        out_shape=out_shape,
        compiler_params=_params(("arbitrary",)),
        name="proj",
    )(x2d, g_attn, w_in, gq, gk, head_mean)


def _attn_prompt_kernel(slopes_ref, q_ref, k_ref, v_ref, o_ref, bias_s, acc_s, m_s, l_s):
    hp = pl.program_id(1)
    seq = q_ref.shape[0]
    lane = lax.broadcasted_iota(jnp.int32, (BLK, 2 * HEAD_DIM), 1)
    left = lane < HEAD_DIM

    row = lax.broadcasted_iota(jnp.int32, (2 * BLK, 2 * BLK), 0)
    col = lax.broadcasted_iota(jnp.int32, (2 * BLK, 2 * BLK), 1)
    diff = (row & (BLK - 1)) + BLK - col
    valid = (diff >= 0) & (diff <= N_STRIDED)
    slope = jnp.where(row < BLK, slopes_ref[2 * hp], slopes_ref[2 * hp + 1])
    for pi, r in enumerate(DILATIONS):
        bias_s[pi] = jnp.where(valid, (-LOG2E) * slope * (r * diff).astype(F32), NEG)

    def rows_of(start, r, n=BLK):
        return pl.ds(start, n) if r == 1 else pl.ds(start, n, stride=r)

    ones_cols = jnp.ones((2 * BLK, 2 * HEAD_DIM), BF16)

    def units(specs, with_prev):
        curs = [rows_of(st, r) for _, r, st in specs]
        qqs, kks, vvs = [], [], []
        for (_, r, st), cur in zip(specs, curs):
            q2 = q_ref[cur, :]
            qqs.append(
                jnp.concatenate([jnp.where(left, q2, 0.0), jnp.where(left, 0.0, q2)], axis=0).astype(BF16))
            if not with_prev:
                kk, vv = k_ref[cur, :], v_ref[cur, :]
            elif r == 1:
                both = pl.ds(st - BLK, 2 * BLK)
                kk, vv = k_ref[both, :], v_ref[both, :]
            else:
                prev = rows_of(st - BLK * r, r)
                kk = jnp.concatenate([k_ref[prev, :], k_ref[cur, :]], axis=0)
                vv = jnp.concatenate([v_ref[prev, :], v_ref[cur, :]], axis=0)
            kks.append(kk.astype(BF16))
            vvs.append(jnp.concatenate([vv.astype(BF16), ones_cols[:vv.shape[0]]], axis=1))
        ss = [lax.dot_general(qq, kk, (((1,), (1,)), ((), ())), preferred_element_type=F32)
              + (bias_s[pi] if with_prev else bias_s[pi, :, BLK:])
              for (pi, _, _), qq, kk in zip(specs, qqs, kks)]
        ms = [jnp.max(s, axis=-1, keepdims=True) for s in ss]
        ps = [jnp.exp2(s - m).astype(BF16) for s, m in zip(ss, ms)]
        pvs = [jnp.dot(p, vv, preferred_element_type=F32) for p, vv in zip(ps, vvs)]
        W2 = 2 * HEAD_DIM
        for (pi, _, _), cur, pv, m in zip(specs, curs, pvs, ms):
            acc = jnp.where(left, pv[:BLK, :W2], pv[BLK:, :W2])
            mt = jnp.where(left, m[:BLK], m[BLK:])
            lt = jnp.where(left, pv[:BLK, W2:], pv[BLK:, W2:])
            if pi > 0:
                acc_s[pi - 1, cur, :] = acc
                m_s[pi - 1, cur, :] = mt
                l_s[pi - 1, cur, :] = lt
                continue
            parts = [(acc, mt, lt)] + [(acc_s[j, cur, :], m_s[j, cur, :], l_s[j, cur, :])
                                       for j in range(len(DILATIONS) - 1)]
            mx = parts[0][1]
            for _, mj, _ in parts[1:]:
                mx = jnp.maximum(mx, mj)
            num = jnp.zeros((BLK, W2), F32)
            den = jnp.zeros((BLK, W2), F32)
            for aj, mj, lj in parts:
                w = jnp.exp2(mj - mx)
                num = num + w * aj
                den = den + w * lj
            o_ref[cur, :] = (num / den).astype(o_ref.dtype)

    U = ATTN_UNITS
    assert DILATIONS[0] == 1
    for pats, per_group in ((range(1, len(DILATIONS)), U), ((0,), ATTN_MERGE_UNITS)):
        first = [(pi, DILATIONS[pi], c) for pi in pats for c in range(DILATIONS[pi])]
        rest = [(pi, DILATIONS[pi], c + blk * (BLK * DILATIONS[pi])) for pi in pats
                for blk in range(1, seq // (DILATIONS[pi] * BLK)) for c in range(DILATIONS[pi])]
        for group, with_prev in ((first, False), (rest, True)):
            for i0 in range(0, len(group), per_group):
                units(group[i0:i0 + per_group], with_prev)


def _attn_prompt(slopes, q, k, v, batch, seq):
    blk = pl.BlockSpec((None, seq, 2 * HEAD_DIM), lambda b, h: (h, b, 0))
    return pl.pallas_call(
        _attn_prompt_kernel,
        grid=(batch, N_HEADS // 2),
        in_specs=[pl.BlockSpec(memory_space=pltpu.SMEM), blk, blk, blk],
        out_specs=blk,
        out_shape=jax.ShapeDtypeStruct((N_HEADS // 2, batch * seq, 2 * HEAD_DIM), BF16),
        scratch_shapes=[
            pltpu.VMEM((len(DILATIONS), 2 * BLK, 2 * BLK), F32),
            pltpu.VMEM((len(DILATIONS) - 1, seq, 2 * HEAD_DIM), F32),
            pltpu.VMEM((len(DILATIONS) - 1, seq, 2 * HEAD_DIM), F32),
            pltpu.VMEM((len(DILATIONS) - 1, seq, 2 * HEAD_DIM), F32),
        ],
        compiler_params=_params(("arbitrary", "arbitrary")),
        name="attn_prompt",
    )(slopes, q, k, v)


N_NEW = 4
QROWS = N_NEW * N_HEADS


SAMPLE_TCHUNK = 512


def _attn_sample_tables(slope_ref, bias_s, mult_s):
    wb = bias_s.shape[1]
    n_of_row = lax.broadcasted_iota(jnp.int32, (QROWS, 1), 0) // N_HEADS
    t = lax.broadcasted_iota(jnp.int32, (QROWS, wb), 1)
    dist = wb + n_of_row - t
    mult = jnp.zeros((QROWS, wb), F32)
    for r in DILATIONS:
        hit = ((dist & (r - 1)) == 0) & (dist <= r * N_STRIDED)
        mult = mult + jnp.where(hit, 1.0, 0.0)
    mult_s[...] = mult
    bias_s[...] = jnp.where(mult > 0.0, (-LOG2E) * slope_ref[...] * dist.astype(F32), NEG)


def _attn_sample_step(slope_ref, q_ref, kn_ref, vn_ref, kt_ref, vt_ref, o_ref, bias_s, mult_s):
    W = ATTN_WIDTH
    wb = kt_ref.shape[2]
    chunks = [slice(c, c + SAMPLE_TCHUNK) for c in range(0, wb, SAMPLE_TCHUNK)]
    slope = slope_ref[...]
    n_of_row = lax.broadcasted_iota(jnp.int32, (QROWS, 1), 0) // N_HEADS
    q = q_ref[0]
    kn = kn_ref[0]
    vn = vn_ref[0]
    sub = lax.broadcasted_iota(jnp.int32, (N_HEADS, W), 0)
    own = sub == lax.broadcasted_iota(jnp.int32, (N_HEADS, W), 1) // HEAD_DIM
    own_rows = jnp.concatenate([own] * N_NEW, axis=0)
    qbd = jnp.concatenate(
        [jnp.where(own, jnp.broadcast_to(q[n:n + 1], (N_HEADS, W)), 0.0) for n in range(N_NEW)], axis=0)
    qbd16 = qbd.astype(BF16)

    ss = [jnp.dot(qbd16, kt_ref[0, :, c].astype(BF16), preferred_element_type=F32) + bias_s[:, c]
          for c in chunks]
    s_new, w_new = [], []
    for m in range(N_NEW):
        sm = jnp.sum(qbd * kn[m:m + 1], axis=-1, keepdims=True)
        gap = n_of_row - m
        s_new.append(jnp.where(gap >= 0, sm - LOG2E * slope * gap.astype(F32), NEG))
        w_new.append(jnp.where(gap == 0, float(len(DILATIONS)), jnp.where(gap > 0, 1.0, 0.0)))
    mx = s_new[0]
    for sm in s_new[1:] + [jnp.max(s, axis=-1, keepdims=True) for s in ss]:
        mx = jnp.maximum(mx, sm)
    den = jnp.zeros((QROWS, 1), F32)
    pv = jnp.zeros((QROWS, W), F32)
    for s, c in zip(ss, chunks):
        p = jnp.exp2(s - mx) * mult_s[:, c]
        den = den + jnp.sum(p, axis=-1, keepdims=True)
        pv = pv + lax.dot_general(p.astype(BF16), vt_ref[0, :, c].astype(BF16), (((1,), (1,)), ((), ())),
                                  preferred_element_type=F32)
    for m in range(N_NEW):
        pm = jnp.exp2(s_new[m] - mx) * w_new[m]
        den = den + pm
        pv = pv + pm * vn[m:m + 1]
    full = jnp.where(own_rows, pv / den, 0.0)
    o_ref[0] = jnp.sum(full.reshape(N_NEW, N_HEADS, W), axis=1).astype(o_ref.dtype)


def _pool_mix(d_groups, wp_ref, ps_ref):
    mixed = [jnp.dot(d.astype(BF16), wp_ref[g], preferred_element_type=F32) for g, d in enumerate(d_groups)]
    return jnp.concatenate(mixed, axis=-1) * ps_ref[...]


def _out_proj(x, attn, pool_out, wo_ref):
    h = x + jnp.dot(attn, wo_ref[:ATTN_WIDTH, :], preferred_element_type=F32)
    return h + jnp.dot(pool_out.astype(BF16), wo_ref[ATTN_WIDTH:, :], preferred_element_type=F32)


def _mix_prompt_tile(x_ref, a_ref, u_ref, wp_ref, ps_ref, wo_ref, e_s, s2_s, s4_s, s8_s, tiles_per_seq):
    i = pl.program_id(0)
    tm = u_ref.shape[0]
    H = POOL_HEAD
    G = POOL_GROUP

    @pl.when(i == 0)
    def _():
        e_s[0:8, :] = jnp.zeros((8, POOL_WIDTH), F32)
        s2_s[0:8, :] = jnp.zeros((8, POOL_WIDTH), F32)
        s4_s[0:8, :] = jnp.zeros((8, 3 * G), F32)
        s8_s[0:8, :] = jnp.zeros((8, 2 * G), F32)

    @pl.when(i % tiles_per_seq == 0)
    def _():
        e_s[8:H, :] = jnp.zeros((H - 8, POOL_WIDTH), F32)

    u = u_ref[...]
    e_s[H:, :] = u
    n = tm + H - 8
    s2_s[8:, :] = e_s[8:, :] + e_s[7:7 + n, :]
    s4_s[8:, :] = s2_s[8:, G:] + s2_s[6:6 + n, G:]
    s8_s[8:, :] = s4_s[8:, G:] + s4_s[4:4 + n, G:]
    sums = [
        s2_s[H:, :G],
        s4_s[H:, :G],
        s8_s[H:, :G],
        s8_s[H:, G:] + s8_s[H - 8:H - 8 + tm, G:],
    ]
    pos = (i % tiles_per_seq) * tm + lax.broadcasted_iota(jnp.int32, (tm, 1), 0)
    d = []
    for g, w in enumerate(POOL_WINDOWS):
        cnt = jnp.minimum(w, pos + 1).astype(F32)
        d.append(sums[g] / cnt - u[:, g * G:(g + 1) * G])
    pool_out = _pool_mix(d, wp_ref, ps_ref)
    attn = jnp.concatenate([a_ref[hp] for hp in range(a_ref.shape[0])], axis=1)
    h = _out_proj(x_ref[...], attn, pool_out, wo_ref)
    e_s[8:H, :] = e_s[tm + 8:tm + H, :]
    return h


def _mix_sample_kernel(x_ref, a_ref, u_ref, st_ref, wp_ref, ps_ref, wo_ref, h_ref):
    G = POOL_GROUP
    PW = POOL_WIDTH

    def row(t, lanes):
        if t < POOL_HIST:
            return st_ref[t, :, lanes]
        return u_ref[:, (t - POOL_HIST) * PW + lanes.start:(t - POOL_HIST) * PW + lanes.stop]

    for n in range(N_NEW):
        t = POOL_HIST + n
        d = []
        for g, w in enumerate(POOL_WINDOWS):
            lanes = slice(g * G, (g + 1) * G)
            cur = row(t, lanes)
            tot = cur
            for j in range(1, w):
                tot = tot + row(t - j, lanes)
            d.append(tot / float(min(w, PAST_LEN + n + 1)) - cur)
        pool_out = _pool_mix(d, wp_ref, ps_ref)
        cols = slice(n * D_MODEL, (n + 1) * D_MODEL)
        attn = a_ref[:, n * ATTN_WIDTH:(n + 1) * ATTN_WIDTH]
        h_ref[:, cols] = _out_proj(x_ref[:, cols], attn, pool_out, wo_ref)


def _mix_sample(x, attn, u, state_pool, w_pool, pool_scale, w_out):
    nseq = x.shape[0]
    full = lambda a: pl.BlockSpec(a.shape, lambda i: (0,) * a.ndim)
    args = (x, attn, u, state_pool, w_pool, pool_scale, w_out)
    return pl.pallas_call(
        _mix_sample_kernel,
        grid=(1,),
        in_specs=[full(a) for a in args],
        out_specs=pl.BlockSpec((nseq, N_NEW * D_MODEL), lambda i: (0, 0)),
        out_shape=jax.ShapeDtypeStruct((nseq, N_NEW * D_MODEL), F32),
        compiler_params=_params(("arbitrary",)),
        name="mix_sample",
    )(*args)


def _rms(h, g_ref):
    return (h * lax.rsqrt(jnp.mean(h * h, axis=-1, keepdims=True) + EPS) * g_ref[...]).astype(BF16)


def _silu_gate(gate, val):
    return (gate / (1.0 + jnp.exp(-gate)) * val).astype(BF16)


def _tail_prompt_kernel(x_ref, a_ref, u_ref, wp_ref, ps_ref, wo_ref, g_ref, wu_ref, cw_ref, cb_ref, wd_ref,
                        slope_ref, qs_ref, kn_ref, vn_ref, kt_ref, vt_ref,
                        y_ref, hist_ref, os_ref,
                        e_s, s2_s, s4_s, s8_s, h_s, ext_s, carry_s, act_s, bias_s, mult_s, *, tiles_per_seq):
    i = pl.program_id(0)
    tm = x_ref.shape[0]
    C = TAIL_FF_CHUNK
    K1 = CONV_WIDTH - 1
    nr = tm // SUBLANES
    n_lane_blocks = D_MODEL // LANES

    @pl.when(i == 0)
    def _():
        _attn_sample_tables(slope_ref, bias_s, mult_s)

    @pl.when(i % tiles_per_seq == 0)
    def _():
        carry_s[...] = jnp.zeros(carry_s.shape, F32)

    h = _mix_prompt_tile(x_ref, a_ref, u_ref, wp_ref, ps_ref, wo_ref, e_s, s2_s, s4_s, s8_s, tiles_per_seq)
    for s in range(SUBLANES):
        for lb in range(n_lane_blocks):
            h_s[lb, pl.ds(s, nr, stride=SUBLANES), :] = h[s * nr:(s + 1) * nr, lb * LANES:(lb + 1) * LANES]
    _attn_sample_step(slope_ref, qs_ref, kn_ref, vn_ref, kt_ref, vt_ref, os_ref, bias_s, mult_s)
    hn = _rms(jnp.concatenate([h_s[lb] for lb in range(n_lane_blocks)], axis=1), g_ref)
    for c0 in range(0, D_FF, C):
        cw = min(C, D_FF - c0)
        first_sub = lax.broadcasted_iota(jnp.int32, (SUBLANES, cw), 0) == 0
        halves = []
        for half in range(2):
            cols = slice(half * D_FF + c0, half * D_FF + c0 + cw)
            up = jnp.dot(hn, wu_ref[:, cols], preferred_element_type=F32)
            ext = ext_s.at[half, :, :cw]
            for j in range(K1):
                rows = slice(j * SUBLANES, (j + 1) * SUBLANES)
                cur = up[(nr - K1 + j) * SUBLANES:(nr - K1 + j + 1) * SUBLANES]
                ext[rows, :] = jnp.where(first_sub, pltpu.roll(carry_s[rows, cols], 1, 0),
                                         pltpu.roll(cur, 1, 0))
            ext[K1 * SUBLANES:, :] = up
            conv = cb_ref[:, cols] + sum(
                ext[j * SUBLANES:j * SUBLANES + tm, :] * cw_ref[j:j + 1, cols] for j in range(CONV_WIDTH))
            carry_s[:, cols] = up[tm - K1 * SUBLANES:]
            halves.append(conv)
        act_s[:, c0:c0 + cw] = _silu_gate(halves[0], halves[1])
    y = (jnp.concatenate([h_s[lb] for lb in range(n_lane_blocks)], axis=1)
         + jnp.dot(act_s[...], wd_ref[...], preferred_element_type=F32))
    for lb in range(n_lane_blocks):
        h_s[lb] = y[:, lb * LANES:(lb + 1) * LANES]
    for s in range(SUBLANES):
        for lb in range(n_lane_blocks):
            y_ref[s * nr:(s + 1) * nr, lb * LANES:(lb + 1) * LANES] = h_s[lb, pl.ds(s, nr, stride=SUBLANES), :]

    @pl.when(i % tiles_per_seq == tiles_per_seq - 1)
    def _():
        hist_ref[0] = jnp.zeros(hist_ref.shape[1:], F32)
        for j in range(K1):
            hist_ref[0, j:j + 1, :] = carry_s[(j + 1) * SUBLANES - 1:(j + 1) * SUBLANES, :]


def _tail_prompt(x2d, attn, u, w_pool, pool_scale, w_out, g_ffn, w_up, conv_w, conv_b, w_down, seq,
                 slope_rows, q_s, kn_s, vn_s, cache_kt, cache_vt):
    n_tok = x2d.shape[0]
    nseq, W, wb = cache_kt.shape
    tm = n_tok // nseq
    assert tm * nseq == n_tok and tm % 8 == 0 and seq % tm == 0
    tps = seq // tm
    G = POOL_GROUP
    pool_rows = tm + POOL_HEAD
    new_spec = pl.BlockSpec((1, N_NEW, W), lambda i: (i, 0, 0))
    cache_spec = pl.BlockSpec((1, W, wb), lambda i: (i, 0, 0))
    return pl.pallas_call(
        functools.partial(_tail_prompt_kernel, tiles_per_seq=tps),
        grid=(nseq,),
        in_specs=[
            pl.BlockSpec((tm, D_MODEL), lambda i: (i, 0)),
            pl.BlockSpec((ATTN_WIDTH // LANES, tm, LANES), lambda i: (0, i, 0)),
            pl.BlockSpec((tm, POOL_WIDTH), lambda i: (i, 0)),
            _const_spec((len(POOL_WINDOWS), G, G)),
            _const_spec((1, POOL_WIDTH)),
            _const_spec((D_MODEL, D_MODEL)),
            _const_spec((1, D_MODEL)),
            _const_spec((D_MODEL, 2 * D_FF)),
            _const_spec((CONV_WIDTH, 2 * D_FF)),
            _const_spec((1, 2 * D_FF)),
            _const_spec((D_FF, D_MODEL)),
            _const_spec((QROWS, 1)),
            new_spec, new_spec, new_spec, cache_spec, cache_spec,
        ],
        out_specs=[
            pl.BlockSpec((tm, D_MODEL), lambda i: (i, 0)),
            pl.BlockSpec((1, SUBLANES, 2 * D_FF), lambda i: (i // tps, 0, 0)),
            new_spec,
        ],
        out_shape=[
            jax.ShapeDtypeStruct((n_tok, D_MODEL), F32),
            jax.ShapeDtypeStruct((n_tok // seq, SUBLANES, 2 * D_FF), F32),
            jax.ShapeDtypeStruct((nseq, N_NEW, W), BF16),
        ],
        scratch_shapes=[
            pltpu.VMEM((pool_rows, POOL_WIDTH), F32),
            pltpu.VMEM((pool_rows, POOL_WIDTH), F32),
            pltpu.VMEM((pool_rows, 3 * G), F32),
            pltpu.VMEM((pool_rows, 2 * G), F32),
            pltpu.VMEM((D_MODEL // LANES, tm, LANES), F32),
            pltpu.VMEM((2, tm + CONV_HEAD, TAIL_FF_CHUNK), F32),
            pltpu.VMEM((CONV_HEAD, 2 * D_FF), F32),
            pltpu.VMEM((tm, D_FF), BF16),
            pltpu.VMEM((QROWS, wb), F32),
            pltpu.VMEM((QROWS, wb), F32),
        ],
        compiler_params=_params(("arbitrary",)),
        name="tail_prompt",
    )(x2d, attn, u, w_pool, pool_scale, w_out, g_ffn, w_up, conv_w, conv_b, w_down,
      slope_rows, q_s, kn_s, vn_s, cache_kt, cache_vt)


def _ffn_sample_kernel(h_ref, st_ref, g_ref, wu_ref, cw_ref, cb_ref, wd_ref, y_ref, hist_ref):
    nseq = h_ref.shape[0]
    C = FF_CHUNK
    K = CONV_WIDTH
    hs = [h_ref[:, n * D_MODEL:(n + 1) * D_MODEL] for n in range(N_NEW)]
    hn = jnp.concatenate([_rms(h, g_ref) for h in hs], axis=0)
    for n in range(N_NEW):
        y_ref[:, n * D_MODEL:(n + 1) * D_MODEL] = hs[n]
    for c in range(N_FF_CHUNKS):
        halves = []
        for half in range(2):
            cols = slice(half * D_FF + c * C, half * D_FF + (c + 1) * C)
            up = jnp.dot(hn, wu_ref[:, cols], preferred_element_type=F32)
            rows = [st_ref[:, t, cols] for t in range(K - 1)]
            rows += [up[n * nseq:(n + 1) * nseq] for n in range(N_NEW)]
            conv = [cb_ref[:, cols] + sum(rows[n + j] * cw_ref[j:j + 1, cols] for j in range(K))
                    for n in range(N_NEW)]
            for t in range(K - 1):
                hist_ref[:, t, cols] = rows[N_NEW + t]
            halves.append(jnp.concatenate(conv, axis=0))
        act = _silu_gate(halves[0], halves[1])
        out = jnp.dot(act, wd_ref[c * C:(c + 1) * C, :], preferred_element_type=F32)
        for n in range(N_NEW):
            y_ref[:, n * D_MODEL:(n + 1) * D_MODEL] += out[n * nseq:(n + 1) * nseq]


def _ffn_sample(h, state_ffn, g_ffn, w_up, conv_w, conv_b, w_down):
    nseq = h.shape[0]
    full = lambda a: pl.BlockSpec(a.shape, lambda i: (0,) * a.ndim, pipeline_mode=pl.Buffered(1))
    args = (h, state_ffn, g_ffn, w_up, conv_w, conv_b, w_down)
    return pl.pallas_call(
        _ffn_sample_kernel,
        grid=(1,),
        in_specs=[full(a) for a in args],
        out_specs=[
            pl.BlockSpec((nseq, N_NEW * D_MODEL), lambda i: (0, 0)),
            pl.BlockSpec((nseq, CONV_WIDTH - 1, 2 * D_FF), lambda i: (0, 0, 0)),
        ],
        out_shape=[
            jax.ShapeDtypeStruct((nseq, N_NEW * D_MODEL), F32),
            jax.ShapeDtypeStruct((nseq, CONV_WIDTH - 1, 2 * D_FF), F32),
        ],
        compiler_params=_params(("arbitrary",)),
        name="ffn_sample",
    )(*args)


def kernel(x_prompt, x_sample, cache_k, cache_v, state_pool, state_ffn_conv, g_attn_norm, w_in, g_q, g_k,
           w_pool, pool_scale, w_out, g_ffn_norm, w_up, conv_w, conv_b, w_down):
    depth = w_in.shape[0]
    assert depth == 1
    batch, seq, _ = x_prompt.shape
    nseq, n_new, _ = x_sample.shape
    assert n_new == N_NEW and seq % (16 * BLK) == 0 and seq % TM == 0
    assert cache_k.shape[2] == 16 * N_STRIDED

    slopes = jnp.asarray(2.0 ** (-8.0 * np.arange(1, N_HEADS + 1) / N_HEADS), dtype=F32)
    head_of_lane = np.arange(ATTN_WIDTH) // HEAD_DIM
    head_mean = jnp.asarray((head_of_lane[:, None] == head_of_lane[None, :]) / HEAD_DIM, dtype=BF16)

    l = 0
    w_in_b = w_in[l].astype(BF16)
    w_pool_b = w_pool[l].astype(BF16)
    w_out_b = w_out[l].astype(BF16)
    w_up_b = w_up[l].astype(BF16)
    w_down_b = w_down[l].astype(BF16)
    g_attn = g_attn_norm[l].reshape(1, D_MODEL)
    g_ffn = g_ffn_norm[l].reshape(1, D_MODEL)
    gq = g_q[l].reshape(1, ATTN_WIDTH)
    gk = g_k[l].reshape(1, ATTN_WIDTH)
    ps = pool_scale[l].reshape(1, POOL_WIDTH)
    cb = conv_b[l].reshape(1, 2 * D_FF)
    cw = conv_w[l]

    xs = x_sample.reshape(nseq, N_NEW * D_MODEL)
    qs, ks, vs, us = _proj(xs, nseq, N_NEW, lambda n: (0, n), lambda n: (0, n), (nseq, N_NEW * ATTN_WIDTH),
                           g_attn, w_in_b, gq, gk, head_mean)
    per_seq = lambda a: a.reshape(nseq, N_NEW, ATTN_WIDTH)
    cache_kt = jnp.transpose(cache_k[l], (0, 2, 3, 1)).reshape(nseq, ATTN_WIDTH, -1)
    cache_vt = jnp.transpose(cache_v[l], (0, 2, 3, 1)).reshape(nseq, ATTN_WIDTH, -1)
    slope_rows = jnp.tile(slopes, N_NEW).reshape(QROWS, 1)

    n_tok = batch * seq
    xp = x_prompt.reshape(n_tok, D_MODEL)
    qp, kp, vp, up, kpt, vpt = _proj(xp, TM, n_tok // TM, lambda i: (i, 0), lambda i: (i, 0),
                                     (n_tok, ATTN_WIDTH), g_attn, w_in_b, gq, gk, head_mean, seq=seq)
    attn_p = _attn_prompt(slopes, qp, kp, vp, batch, seq)
    yp, hist_p, attn_s = _tail_prompt(xp, attn_p, up, w_pool_b, ps, w_out_b, g_ffn, w_up_b, cw, cb, w_down_b,
                                      seq, slope_rows, per_seq(qs), per_seq(ks), per_seq(vs), cache_kt, cache_vt)

    attn_s = attn_s.reshape(nseq, N_NEW * ATTN_WIDTH)
    st_pool = jnp.transpose(state_pool[l], (1, 0, 2))
    hs = _mix_sample(xs, attn_s, us, st_pool, w_pool_b, ps, w_out_b)
    ys, hist_s = _ffn_sample(hs, state_ffn_conv[l], g_ffn, w_up_b, cw, cb, w_down_b)

    n_keep = min(16 * N_STRIDED, seq)
    window = lambda t: jnp.transpose(
        t.reshape(batch, N_HEADS, HEAD_DIM, seq), (0, 3, 1, 2))[None, :, seq - n_keep:]
    kp5, vp5 = window(kpt), window(vpt)
    new_pool_p = up.reshape(batch, seq, POOL_WIDTH)[None, :, seq - POOL_HIST:]
    new_ffn_p = hist_p[None, :, :CONV_WIDTH - 1]
    u_time = jnp.transpose(us.reshape(nseq, N_NEW, POOL_WIDTH), (1, 0, 2))
    new_pool_s = jnp.transpose(jnp.concatenate([st_pool[N_NEW:], u_time], axis=0), (1, 0, 2))[None]
    return (
        yp.reshape(batch, seq, D_MODEL),
        ys.reshape(nseq, N_NEW, D_MODEL),
        kp5, vp5, new_pool_p, new_ffn_p,
        ks.reshape(1, nseq, N_NEW, N_HEADS, HEAD_DIM),
        vs.reshape(1, nseq, N_NEW, N_HEADS, HEAD_DIM),
        new_pool_s,
        hist_s.reshape(1, nseq, CONV_WIDTH - 1, 2 * D_FF),
    )
```

```python
import functools

import numpy as np
import jax
import jax.numpy as jnp
from jax import lax
from jax.experimental import pallas as pl
from jax.experimental.pallas import tpu as pltpu

D_MODEL = 1024
HEAD_DIM = 64
N_HEADS = 8
ATTN_WIDTH = N_HEADS * HEAD_DIM
POOL_WIDTH = 512
POOL_WINDOWS = (2, 4, 8, 16)
POOL_GROUP = 128
POOL_HIST = 15
PROJ_WIDTH = 3 * ATTN_WIDTH + POOL_WIDTH
DILATIONS = (1, 4, 16)
N_STRIDED = 128
BLK = 128
D_FF = 2816
CONV_WIDTH = 3
PAST_LEN = 8192
EPS = 1e-6
NEG = -1e30
LOG2E = 1.4426950408889634

F32 = jnp.float32
BF16 = jnp.bfloat16

SUBLANES = 8
LANES = 128
VMEM_LIMIT = 56 * 1024 * 1024

TM = 1024
ATTN_UNITS = 8
ATTN_MERGE_UNITS = 8
FF_CHUNK = 256
N_FF_CHUNKS = D_FF // FF_CHUNK
TAIL_FF_CHUNK = 256
CONV_ROW_SPLIT = 2
POOL_HEAD = 24
CONV_HEAD = (CONV_WIDTH - 1) * SUBLANES


def _params(sem, vmem=VMEM_LIMIT):
    return pltpu.CompilerParams(dimension_semantics=sem, vmem_limit_bytes=vmem)


def _const_spec(shape):
    nd = len(shape)
    return pl.BlockSpec(shape, lambda *_: (0,) * nd, pipeline_mode=pl.Buffered(1))


def _proj_kernel(x_ref, g_ref, w_ref, gq_ref, gk_ref, hm_ref, q_ref, k_ref, v_ref, u_ref, *kvt_refs):
    x = x_ref[...]
    xn = x * lax.rsqrt(jnp.mean(x * x, axis=-1, keepdims=True) + EPS) * g_ref[...]
    proj = jnp.dot(xn.astype(BF16), w_ref[...], preferred_element_type=F32)

    def head_norm(t, g):
        ms = jnp.dot((t * t).astype(BF16), hm_ref[...], preferred_element_type=F32)
        return t * lax.rsqrt(ms + EPS) * g

    def put(ref, val):
        if len(ref.shape) == 2:
            ref[...] = val
        else:
            for hp in range(ref.shape[0]):
                ref[hp] = val[:, hp * LANES:(hp + 1) * LANES]

    q = head_norm(proj[:, :ATTN_WIDTH], gq_ref[...])
    k = head_norm(proj[:, ATTN_WIDTH:2 * ATTN_WIDTH], gk_ref[...])
    v = proj[:, 2 * ATTN_WIDTH:3 * ATTN_WIDTH]
    put(q_ref, q * (HEAD_DIM ** -0.5 * LOG2E))
    put(k_ref, k)
    put(v_ref, v)
    u_ref[...] = proj[:, 3 * ATTN_WIDTH:]
    if kvt_refs:
        kvt_refs[0][0] = k.T
        kvt_refs[1][0] = v.T


def _proj(x2d, rows, n_steps, x_map, o_map, out_rows, g_attn, w_in, gq, gk, head_mean, seq=None):
    out = jax.ShapeDtypeStruct((out_rows[0], out_rows[1]), F32)
    out_specs = [pl.BlockSpec((rows, ATTN_WIDTH), o_map)] * 4
    out_shape = [out] * 4
    if seq is not None:
        pairs = ATTN_WIDTH // LANES
        out_specs[:3] = [pl.BlockSpec((pairs, rows, LANES), lambda i: (0, i, 0))] * 3
        out_shape[:3] = [jax.ShapeDtypeStruct((pairs, out_rows[0], LANES), F32)] * 3
        tps = seq // rows
        out_specs += [pl.BlockSpec((1, ATTN_WIDTH, rows), lambda i: (i // tps, 0, i % tps))] * 2
        out_shape += [jax.ShapeDtypeStruct((out_rows[0] // seq, ATTN_WIDTH, seq), F32)] * 2
    return pl.pallas_call(
        _proj_kernel,
        grid=(n_steps,),
        in_specs=[
            pl.BlockSpec((rows, D_MODEL), x_map),
            _const_spec((1, D_MODEL)),
            _const_spec((D_MODEL, PROJ_WIDTH)),
            _const_spec((1, ATTN_WIDTH)),
            _const_spec((1, ATTN_WIDTH)),
            _const_spec((ATTN_WIDTH, ATTN_WIDTH)),
        ],
        out_specs=out_specs,
        out_shape=out_shape,
        compiler_params=_params(("arbitrary",)),
        name="proj",
    )(x2d, g_attn, w_in, gq, gk, head_mean)


def _attn_prompt_kernel(slopes_ref, q_ref, k_ref, v_ref, o_ref, bias_s, acc_s, m_s, l_s):
    hp = pl.program_id(1)
    seq = q_ref.shape[0]
    lane = lax.broadcasted_iota(jnp.int32, (BLK, 2 * HEAD_DIM), 1)
    left = lane < HEAD_DIM

    row = lax.broadcasted_iota(jnp.int32, (2 * BLK, 2 * BLK), 0)
    col = lax.broadcasted_iota(jnp.int32, (2 * BLK, 2 * BLK), 1)
    diff = (row & (BLK - 1)) + BLK - col
    valid = (diff >= 0) & (diff <= N_STRIDED)
    slope = jnp.where(row < BLK, slopes_ref[2 * hp], slopes_ref[2 * hp + 1])
    for pi, r in enumerate(DILATIONS):
        bias_s[pi] = jnp.where(valid, (-LOG2E) * slope * (r * diff).astype(F32), NEG)

    def rows_of(start, r, n=BLK):
        return pl.ds(start, n) if r == 1 else pl.ds(start, n, stride=r)

    ones_cols = jnp.ones((2 * BLK, 2 * HEAD_DIM), BF16)

    def units(specs, with_prev):
        curs = [rows_of(st, r) for _, r, st in specs]
        qqs, kks, vvs = [], [], []
        for (_, r, st), cur in zip(specs, curs):
            q2 = q_ref[cur, :]
            qqs.append(
                jnp.concatenate([jnp.where(left, q2, 0.0), jnp.where(left, 0.0, q2)], axis=0).astype(BF16))
            if not with_prev:
                kk, vv = k_ref[cur, :], v_ref[cur, :]
            elif r == 1:
                both = pl.ds(st - BLK, 2 * BLK)
                kk, vv = k_ref[both, :], v_ref[both, :]
            else:
                prev = rows_of(st - BLK * r, r)
                kk = jnp.concatenate([k_ref[prev, :], k_ref[cur, :]], axis=0)
                vv = jnp.concatenate([v_ref[prev, :], v_ref[cur, :]], axis=0)
            kks.append(kk.astype(BF16))
            vvs.append(jnp.concatenate([vv.astype(BF16), ones_cols[:vv.shape[0]]], axis=1))
        ss = [lax.dot_general(qq, kk, (((1,), (1,)), ((), ())), preferred_element_type=F32)
              + (bias_s[pi] if with_prev else bias_s[pi, :, BLK:])
              for (pi, _, _), qq, kk in zip(specs, qqs, kks)]
        ms = [jnp.max(s, axis=-1, keepdims=True) for s in ss]
        ps = [jnp.exp2(s - m).astype(BF16) for s, m in zip(ss, ms)]
        pvs = [jnp.dot(p, vv, preferred_element_type=F32) for p, vv in zip(ps, vvs)]
        W2 = 2 * HEAD_DIM
        for (pi, _, _), cur, pv, m in zip(specs, curs, pvs, ms):
            acc = jnp.where(left, pv[:BLK, :W2], pv[BLK:, :W2])
            mt = jnp.where(left, m[:BLK], m[BLK:])
            lt = jnp.where(left, pv[:BLK, W2:], pv[BLK:, W2:])
            if pi > 0:
                acc_s[pi - 1, cur, :] = acc
                m_s[pi - 1, cur, :] = mt
                l_s[pi - 1, cur, :] = lt
                continue
            parts = [(acc, mt, lt)] + [(acc_s[j, cur, :], m_s[j, cur, :], l_s[j, cur, :])
                                       for j in range(len(DILATIONS) - 1)]
            mx = parts[0][1]
            for _, mj, _ in parts[1:]:
                mx = jnp.maximum(mx, mj)
            num = jnp.zeros((BLK, W2), F32)
            den = jnp.zeros((BLK, W2), F32)
            for aj, mj, lj in parts:
                w = jnp.exp2(mj - mx)
                num = num + w * aj
                den = den + w * lj
            o_ref[cur, :] = (num / den).astype(o_ref.dtype)

    U = ATTN_UNITS
    assert DILATIONS[0] == 1
    for pats, per_group in ((range(1, len(DILATIONS)), U), ((0,), ATTN_MERGE_UNITS)):
        first = [(pi, DILATIONS[pi], c) for pi in pats for c in range(DILATIONS[pi])]
        rest = [(pi, DILATIONS[pi], c + blk * (BLK * DILATIONS[pi])) for pi in pats
                for blk in range(1, seq // (DILATIONS[pi] * BLK)) for c in range(DILATIONS[pi])]
        for group, with_prev in ((first, False), (rest, True)):
            for i0 in range(0, len(group), per_group):
                units(group[i0:i0 + per_group], with_prev)


def _attn_prompt(slopes, q, k, v, batch, seq):
    blk = pl.BlockSpec((None, seq, 2 * HEAD_DIM), lambda b, h: (h, b, 0))
    return pl.pallas_call(
        _attn_prompt_kernel,
        grid=(batch, N_HEADS // 2),
        in_specs=[pl.BlockSpec(memory_space=pltpu.SMEM), blk, blk, blk],
        out_specs=blk,
        out_shape=jax.ShapeDtypeStruct((N_HEADS // 2, batch * seq, 2 * HEAD_DIM), BF16),
        scratch_shapes=[
            pltpu.VMEM((len(DILATIONS), 2 * BLK, 2 * BLK), F32),
            pltpu.VMEM((len(DILATIONS) - 1, seq, 2 * HEAD_DIM), F32),
            pltpu.VMEM((len(DILATIONS) - 1, seq, 2 * HEAD_DIM), F32),
            pltpu.VMEM((len(DILATIONS) - 1, seq, 2 * HEAD_DIM), F32),
        ],
        compiler_params=_params(("arbitrary", "arbitrary")),
        name="attn_prompt",
    )(slopes, q, k, v)


N_NEW = 4
QROWS = N_NEW * N_HEADS


SAMPLE_TCHUNK = 512


def _attn_sample_tables(slope_ref, bias_s, mult_s):
    wb = bias_s.shape[1]
    n_of_row = lax.broadcasted_iota(jnp.int32, (QROWS, 1), 0) // N_HEADS
    t = lax.broadcasted_iota(jnp.int32, (QROWS, wb), 1)
    dist = wb + n_of_row - t
    mult = jnp.zeros((QROWS, wb), F32)
    for r in DILATIONS:
        hit = ((dist & (r - 1)) == 0) & (dist <= r * N_STRIDED)
        mult = mult + jnp.where(hit, 1.0, 0.0)
    mult_s[...] = mult
    bias_s[...] = jnp.where(mult > 0.0, (-LOG2E) * slope_ref[...] * dist.astype(F32), NEG)


def _attn_sample_step(slope_ref, q_ref, kn_ref, vn_ref, kt_ref, vt_ref, o_ref, bias_s, mult_s):
    W = ATTN_WIDTH
    wb = kt_ref.shape[2]
    chunks = [slice(c, c + SAMPLE_TCHUNK) for c in range(0, wb, SAMPLE_TCHUNK)]
    slope = slope_ref[...]
    n_of_row = lax.broadcasted_iota(jnp.int32, (QROWS, 1), 0) // N_HEADS
    q = q_ref[0]
    kn = kn_ref[0]
    vn = vn_ref[0]
    sub = lax.broadcasted_iota(jnp.int32, (N_HEADS, W), 0)
    own = sub == lax.broadcasted_iota(jnp.int32, (N_HEADS, W), 1) // HEAD_DIM
    own_rows = jnp.concatenate([own] * N_NEW, axis=0)
    qbd = jnp.concatenate(
        [jnp.where(own, jnp.broadcast_to(q[n:n + 1], (N_HEADS, W)), 0.0) for n in range(N_NEW)], axis=0)
    qbd16 = qbd.astype(BF16)

    ss = [jnp.dot(qbd16, kt_ref[0, :, c].astype(BF16), preferred_element_type=F32) + bias_s[:, c]
          for c in chunks]
    s_new, w_new = [], []
    for m in range(N_NEW):
        sm = jnp.sum(qbd * kn[m:m + 1], axis=-1, keepdims=True)
        gap = n_of_row - m
        s_new.append(jnp.where(gap >= 0, sm - LOG2E * slope * gap.astype(F32), NEG))
        w_new.append(jnp.where(gap == 0, float(len(DILATIONS)), jnp.where(gap > 0, 1.0, 0.0)))
    mx = s_new[0]
    for sm in s_new[1:] + [jnp.max(s, axis=-1, keepdims=True) for s in ss]:
        mx = jnp.maximum(mx, sm)
    den = jnp.zeros((QROWS, 1), F32)
    pv = jnp.zeros((QROWS, W), F32)
    for s, c in zip(ss, chunks):
        p = jnp.exp2(s - mx) * mult_s[:, c]
        den = den + jnp.sum(p, axis=-1, keepdims=True)
        pv = pv + lax.dot_general(p.astype(BF16), vt_ref[0, :, c].astype(BF16), (((1,), (1,)), ((), ())),
                                  preferred_element_type=F32)
    for m in range(N_NEW):
        pm = jnp.exp2(s_new[m] - mx) * w_new[m]
        den = den + pm
        pv = pv + pm * vn[m:m + 1]
    full = jnp.where(own_rows, pv / den, 0.0)
    o_ref[0] = jnp.sum(full.reshape(N_NEW, N_HEADS, W), axis=1).astype(o_ref.dtype)


def _pool_mix(d_groups, wp_ref, ps_ref):
    mixed = [jnp.dot(d.astype(BF16), wp_ref[g], preferred_element_type=F32) for g, d in enumerate(d_groups)]
    return jnp.concatenate(mixed, axis=-1) * ps_ref[...]


def _out_proj(x, attn, pool_out, wo_ref):
    h = x + jnp.dot(attn, wo_ref[:ATTN_WIDTH, :], preferred_element_type=F32)
    return h + jnp.dot(pool_out.astype(BF16), wo_ref[ATTN_WIDTH:, :], preferred_element_type=F32)


def _mix_prompt_tile(x_ref, a_ref, u_ref, wp_ref, ps_ref, wo_ref, e_s, s2_s, s4_s, s8_s, tiles_per_seq):
    i = pl.program_id(0)
    tm = u_ref.shape[0]
    H = POOL_HEAD
    G = POOL_GROUP

    @pl.when(i == 0)
    def _():
        e_s[0:8, :] = jnp.zeros((8, POOL_WIDTH), F32)
        s2_s[0:8, :] = jnp.zeros((8, POOL_WIDTH), F32)
        s4_s[0:8, :] = jnp.zeros((8, 3 * G), F32)
        s8_s[0:8, :] = jnp.zeros((8, 2 * G), F32)

    @pl.when(i % tiles_per_seq == 0)
    def _():
        e_s[8:H, :] = jnp.zeros((H - 8, POOL_WIDTH), F32)

    u = u_ref[...]
    e_s[H:, :] = u
    n = tm + H - 8
    s2_s[8:, :] = e_s[8:, :] + e_s[7:7 + n, :]
    s4_s[8:, :] = s2_s[8:, G:] + s2_s[6:6 + n, G:]
    s8_s[8:, :] = s4_s[8:, G:] + s4_s[4:4 + n, G:]
    sums = [
        s2_s[H:, :G],
        s4_s[H:, :G],
        s8_s[H:, :G],
        s8_s[H:, G:] + s8_s[H - 8:H - 8 + tm, G:],
    ]
    pos = (i % tiles_per_seq) * tm + lax.broadcasted_iota(jnp.int32, (tm, 1), 0)
    d = []
    for g, w in enumerate(POOL_WINDOWS):
        cnt = jnp.minimum(w, pos + 1).astype(F32)
        d.append(sums[g] / cnt - u[:, g * G:(g + 1) * G])
    pool_out = _pool_mix(d, wp_ref, ps_ref)
    attn = jnp.concatenate([a_ref[hp] for hp in range(a_ref.shape[0])], axis=1)
    h = _out_proj(x_ref[...], attn, pool_out, wo_ref)
    e_s[8:H, :] = e_s[tm + 8:tm + H, :]
    return h


def _mix_sample_kernel(x_ref, a_ref, u_ref, st_ref, wp_ref, ps_ref, wo_ref, h_ref):
    G = POOL_GROUP
    PW = POOL_WIDTH

    def row(t, lanes):
        if t < POOL_HIST:
            return st_ref[t, :, lanes]
        return u_ref[:, (t - POOL_HIST) * PW + lanes.start:(t - POOL_HIST) * PW + lanes.stop]

    for n in range(N_NEW):
        t = POOL_HIST + n
        d = []
        for g, w in enumerate(POOL_WINDOWS):
            lanes = slice(g * G, (g + 1) * G)
            cur = row(t, lanes)
            tot = cur
            for j in range(1, w):
                tot = tot + row(t - j, lanes)
            d.append(tot / float(min(w, PAST_LEN + n + 1)) - cur)
        pool_out = _pool_mix(d, wp_ref, ps_ref)
        cols = slice(n * D_MODEL, (n + 1) * D_MODEL)
        attn = a_ref[:, n * ATTN_WIDTH:(n + 1) * ATTN_WIDTH]
        h_ref[:, cols] = _out_proj(x_ref[:, cols], attn, pool_out, wo_ref)


def _mix_sample(x, attn, u, state_pool, w_pool, pool_scale, w_out):
    nseq = x.shape[0]
    full = lambda a: pl.BlockSpec(a.shape, lambda i: (0,) * a.ndim)
    args = (x, attn, u, state_pool, w_pool, pool_scale, w_out)
    return pl.pallas_call(
        _mix_sample_kernel,
        grid=(1,),
        in_specs=[full(a) for a in args],
        out_specs=pl.BlockSpec((nseq, N_NEW * D_MODEL), lambda i: (0, 0)),
        out_shape=jax.ShapeDtypeStruct((nseq, N_NEW * D_MODEL), F32),
        compiler_params=_params(("arbitrary",)),
        name="mix_sample",
    )(*args)


def _rms(h, g_ref):
    return (h * lax.rsqrt(jnp.mean(h * h, axis=-1, keepdims=True) + EPS) * g_ref[...]).astype(BF16)


def _silu_gate(gate, val):
    return (gate / (1.0 + jnp.exp(-gate)) * val).astype(BF16)


def _tail_prompt_kernel(x_ref, a_ref, u_ref, wp_ref, ps_ref, wo_ref, g_ref, wu_ref, cw_ref, cb_ref, wd_ref,
                        slope_ref, qs_ref, kn_ref, vn_ref, kt_ref, vt_ref,
                        y_ref, hist_ref, os_ref,
                        e_s, s2_s, s4_s, s8_s, h_s, ext_s, carry_s, act_s, bias_s, mult_s, *, tiles_per_seq):
    i = pl.program_id(0)
    tm = x_ref.shape[0]
    C = TAIL_FF_CHUNK
    K1 = CONV_WIDTH - 1
    nr = tm // SUBLANES
    n_lane_blocks = D_MODEL // LANES

    @pl.when(i == 0)
    def _():
        _attn_sample_tables(slope_ref, bias_s, mult_s)

    @pl.when(i % tiles_per_seq == 0)
    def _():
        carry_s[...] = jnp.zeros(carry_s.shape, F32)

    h = _mix_prompt_tile(x_ref, a_ref, u_ref, wp_ref, ps_ref, wo_ref, e_s, s2_s, s4_s, s8_s, tiles_per_seq)
    for s in range(SUBLANES):
        for lb in range(n_lane_blocks):
            h_s[lb, pl.ds(s, nr, stride=SUBLANES), :] = h[s * nr:(s + 1) * nr, lb * LANES:(lb + 1) * LANES]
    _attn_sample_step(slope_ref, qs_ref, kn_ref, vn_ref, kt_ref, vt_ref, os_ref, bias_s, mult_s)
    hn = _rms(jnp.concatenate([h_s[lb] for lb in range(n_lane_blocks)], axis=1), g_ref)
    for c0 in range(0, D_FF, C):
        cw = min(C, D_FF - c0)
        first_sub = lax.broadcasted_iota(jnp.int32, (SUBLANES, cw), 0) == 0
        col_sets = [slice(half * D_FF + c0, half * D_FF + c0 + cw) for half in range(2)]
        for half, cols in enumerate(col_sets):
            up = jnp.dot(hn, wu_ref[:, cols], preferred_element_type=F32)
            ext = ext_s.at[half, :, :cw]
            for j in range(K1):
                rows = slice(j * SUBLANES, (j + 1) * SUBLANES)
                cur = up[(nr - K1 + j) * SUBLANES:(nr - K1 + j + 1) * SUBLANES]
                ext[rows, :] = jnp.where(first_sub, pltpu.roll(carry_s[rows, cols], 1, 0),
                                         pltpu.roll(cur, 1, 0))
            ext[K1 * SUBLANES:, :] = up
            carry_s[:, cols] = up[tm - K1 * SUBLANES:]
        for r0 in range(0, tm, tm // CONV_ROW_SPLIT):
            rn = tm // CONV_ROW_SPLIT
            conv = [cb_ref[:, cols] + sum(
                ext_s[half, r0 + j * SUBLANES:r0 + j * SUBLANES + rn, :cw] * cw_ref[j:j + 1, cols]
                for j in range(CONV_WIDTH)) for half, cols in enumerate(col_sets)]
            act_s[r0:r0 + rn, c0:c0 + cw] = _silu_gate(conv[0], conv[1])
    y = (jnp.concatenate([h_s[lb] for lb in range(n_lane_blocks)], axis=1)
         + jnp.dot(act_s[...], wd_ref[...], preferred_element_type=F32))
    for lb in range(n_lane_blocks):
        h_s[lb] = y[:, lb * LANES:(lb + 1) * LANES]
    for s in range(SUBLANES):
        for lb in range(n_lane_blocks):
            y_ref[s * nr:(s + 1) * nr, lb * LANES:(lb + 1) * LANES] = h_s[lb, pl.ds(s, nr, stride=SUBLANES), :]

    @pl.when(i % tiles_per_seq == tiles_per_seq - 1)
    def _():
        hist_ref[0] = jnp.zeros(hist_ref.shape[1:], F32)
        for j in range(K1):
            hist_ref[0, j:j + 1, :] = carry_s[(j + 1) * SUBLANES - 1:(j + 1) * SUBLANES, :]


def _tail_prompt(x2d, attn, u, w_pool, pool_scale, w_out, g_ffn, w_up, conv_w, conv_b, w_down, seq,
                 slope_rows, q_s, kn_s, vn_s, cache_kt, cache_vt):
    n_tok = x2d.shape[0]
    nseq, W, wb = cache_kt.shape
    tm = n_tok // nseq
    assert tm * nseq == n_tok and tm % 8 == 0 and seq % tm == 0
    tps = seq // tm
    G = POOL_GROUP
    pool_rows = tm + POOL_HEAD
    new_spec = pl.BlockSpec((1, N_NEW, W), lambda i: (i, 0, 0))
    cache_spec = pl.BlockSpec((1, W, wb), lambda i: (i, 0, 0))
    return pl.pallas_call(
        functools.partial(_tail_prompt_kernel, tiles_per_seq=tps),
        grid=(nseq,),
        in_specs=[
            pl.BlockSpec((tm, D_MODEL), lambda i: (i, 0)),
            pl.BlockSpec((ATTN_WIDTH // LANES, tm, LANES), lambda i: (0, i, 0)),
            pl.BlockSpec((tm, POOL_WIDTH), lambda i: (i, 0)),
            _const_spec((len(POOL_WINDOWS), G, G)),
            _const_spec((1, POOL_WIDTH)),
            _const_spec((D_MODEL, D_MODEL)),
            _const_spec((1, D_MODEL)),
            _const_spec((D_MODEL, 2 * D_FF)),
            _const_spec((CONV_WIDTH, 2 * D_FF)),
            _const_spec((1, 2 * D_FF)),
            _const_spec((D_FF, D_MODEL)),
            _const_spec((QROWS, 1)),
            new_spec, new_spec, new_spec, cache_spec, cache_spec,
        ],
        out_specs=[
            pl.BlockSpec((tm, D_MODEL), lambda i: (i, 0)),
            pl.BlockSpec((1, SUBLANES, 2 * D_FF), lambda i: (i // tps, 0, 0)),
            new_spec,
        ],
        out_shape=[
            jax.ShapeDtypeStruct((n_tok, D_MODEL), F32),
            jax.ShapeDtypeStruct((n_tok // seq, SUBLANES, 2 * D_FF), F32),
            jax.ShapeDtypeStruct((nseq, N_NEW, W), BF16),
        ],
        scratch_shapes=[
            pltpu.VMEM((pool_rows, POOL_WIDTH), F32),
            pltpu.VMEM((pool_rows, POOL_WIDTH), F32),
            pltpu.VMEM((pool_rows, 3 * G), F32),
            pltpu.VMEM((pool_rows, 2 * G), F32),
            pltpu.VMEM((D_MODEL // LANES, tm, LANES), F32),
            pltpu.VMEM((2, tm + CONV_HEAD, TAIL_FF_CHUNK), F32),
            pltpu.VMEM((CONV_HEAD, 2 * D_FF), F32),
            pltpu.VMEM((tm, D_FF), BF16),
            pltpu.VMEM((QROWS, wb), F32),
            pltpu.VMEM((QROWS, wb), F32),
        ],
        compiler_params=_params(("arbitrary",)),
        name="tail_prompt",
    )(x2d, attn, u, w_pool, pool_scale, w_out, g_ffn, w_up, conv_w, conv_b, w_down,
      slope_rows, q_s, kn_s, vn_s, cache_kt, cache_vt)


def _ffn_sample_kernel(h_ref, st_ref, g_ref, wu_ref, cw_ref, cb_ref, wd_ref, y_ref, hist_ref):
    nseq = h_ref.shape[0]
    C = FF_CHUNK
    K = CONV_WIDTH
    hs = [h_ref[:, n * D_MODEL:(n + 1) * D_MODEL] for n in range(N_NEW)]
    hn = jnp.concatenate([_rms(h, g_ref) for h in hs], axis=0)
    for n in range(N_NEW):
        y_ref[:, n * D_MODEL:(n + 1) * D_MODEL] = hs[n]
    for c in range(N_FF_CHUNKS):
        halves = []
        for half in range(2):
            cols = slice(half * D_FF + c * C, half * D_FF + (c + 1) * C)
            up = jnp.dot(hn, wu_ref[:, cols], preferred_element_type=F32)
            rows = [st_ref[:, t, cols] for t in range(K - 1)]
            rows += [up[n * nseq:(n + 1) * nseq] for n in range(N_NEW)]
            conv = [cb_ref[:, cols] + sum(rows[n + j] * cw_ref[j:j + 1, cols] for j in range(K))
                    for n in range(N_NEW)]
            for t in range(K - 1):
                hist_ref[:, t, cols] = rows[N_NEW + t]
            halves.append(jnp.concatenate(conv, axis=0))
        act = _silu_gate(halves[0], halves[1])
        out = jnp.dot(act, wd_ref[c * C:(c + 1) * C, :], preferred_element_type=F32)
        for n in range(N_NEW):
            y_ref[:, n * D_MODEL:(n + 1) * D_MODEL] += out[n * nseq:(n + 1) * nseq]


def _ffn_sample(h, state_ffn, g_ffn, w_up, conv_w, conv_b, w_down):
    nseq = h.shape[0]
    full = lambda a: pl.BlockSpec(a.shape, lambda i: (0,) * a.ndim, pipeline_mode=pl.Buffered(1))
    args = (h, state_ffn, g_ffn, w_up, conv_w, conv_b, w_down)
    return pl.pallas_call(
        _ffn_sample_kernel,
        grid=(1,),
        in_specs=[full(a) for a in args],
        out_specs=[
            pl.BlockSpec((nseq, N_NEW * D_MODEL), lambda i: (0, 0)),
            pl.BlockSpec((nseq, CONV_WIDTH - 1, 2 * D_FF), lambda i: (0, 0, 0)),
        ],
        out_shape=[
            jax.ShapeDtypeStruct((nseq, N_NEW * D_MODEL), F32),
            jax.ShapeDtypeStruct((nseq, CONV_WIDTH - 1, 2 * D_FF), F32),
        ],
        compiler_params=_params(("arbitrary",)),
        name="ffn_sample",
    )(*args)


def kernel(x_prompt, x_sample, cache_k, cache_v, state_pool, state_ffn_conv, g_attn_norm, w_in, g_q, g_k,
           w_pool, pool_scale, w_out, g_ffn_norm, w_up, conv_w, conv_b, w_down):
    depth = w_in.shape[0]
    assert depth == 1
    batch, seq, _ = x_prompt.shape
    nseq, n_new, _ = x_sample.shape
    assert n_new == N_NEW and seq % (16 * BLK) == 0 and seq % TM == 0
    assert cache_k.shape[2] == 16 * N_STRIDED

    slopes = jnp.asarray(2.0 ** (-8.0 * np.arange(1, N_HEADS + 1) / N_HEADS), dtype=F32)
    head_of_lane = np.arange(ATTN_WIDTH) // HEAD_DIM
    head_mean = jnp.asarray((head_of_lane[:, None] == head_of_lane[None, :]) / HEAD_DIM, dtype=BF16)

    l = 0
    w_in_b = w_in[l].astype(BF16)
    w_pool_b = w_pool[l].astype(BF16)
    w_out_b = w_out[l].astype(BF16)
    w_up_b = w_up[l].astype(BF16)
    w_down_b = w_down[l].astype(BF16)
    g_attn = g_attn_norm[l].reshape(1, D_MODEL)
    g_ffn = g_ffn_norm[l].reshape(1, D_MODEL)
    gq = g_q[l].reshape(1, ATTN_WIDTH)
    gk = g_k[l].reshape(1, ATTN_WIDTH)
    ps = pool_scale[l].reshape(1, POOL_WIDTH)
    cb = conv_b[l].reshape(1, 2 * D_FF)
    cw = conv_w[l]

    xs = x_sample.reshape(nseq, N_NEW * D_MODEL)
    qs, ks, vs, us = _proj(xs, nseq, N_NEW, lambda n: (0, n), lambda n: (0, n), (nseq, N_NEW * ATTN_WIDTH),
                           g_attn, w_in_b, gq, gk, head_mean)
    per_seq = lambda a: a.reshape(nseq, N_NEW, ATTN_WIDTH)
    cache_kt = jnp.transpose(cache_k[l], (0, 2, 3, 1)).reshape(nseq, ATTN_WIDTH, -1)
    cache_vt = jnp.transpose(cache_v[l], (0, 2, 3, 1)).reshape(nseq, ATTN_WIDTH, -1)
    slope_rows = jnp.tile(slopes, N_NEW).reshape(QROWS, 1)

    n_tok = batch * seq
    xp = x_prompt.reshape(n_tok, D_MODEL)
    qp, kp, vp, up, kpt, vpt = _proj(xp, TM, n_tok // TM, lambda i: (i, 0), lambda i: (i, 0),
                                     (n_tok, ATTN_WIDTH), g_attn, w_in_b, gq, gk, head_mean, seq=seq)
    attn_p = _attn_prompt(slopes, qp, kp, vp, batch, seq)
    yp, hist_p, attn_s = _tail_prompt(xp, attn_p, up, w_pool_b, ps, w_out_b, g_ffn, w_up_b, cw, cb, w_down_b,
                                      seq, slope_rows, per_seq(qs), per_seq(ks), per_seq(vs), cache_kt, cache_vt)

    attn_s = attn_s.reshape(nseq, N_NEW * ATTN_WIDTH)
    st_pool = jnp.transpose(state_pool[l], (1, 0, 2))
    hs = _mix_sample(xs, attn_s, us, st_pool, w_pool_b, ps, w_out_b)
    ys, hist_s = _ffn_sample(hs, state_ffn_conv[l], g_ffn, w_up_b, cw, cb, w_down_b)

    n_keep = min(16 * N_STRIDED, seq)
    window = lambda t: jnp.transpose(
        t.reshape(batch, N_HEADS, HEAD_DIM, seq), (0, 3, 1, 2))[None, :, seq - n_keep:]
    kp5, vp5 = window(kpt), window(vpt)
    new_pool_p = up.reshape(batch, seq, POOL_WIDTH)[None, :, seq - POOL_HIST:]
    new_ffn_p = hist_p[None, :, :CONV_WIDTH - 1]
    u_time = jnp.transpose(us.reshape(nseq, N_NEW, POOL_WIDTH), (1, 0, 2))
    new_pool_s = jnp.transpose(jnp.concatenate([st_pool[N_NEW:], u_time], axis=0), (1, 0, 2))[None]
    return (
        yp.reshape(batch, seq, D_MODEL),
        ys.reshape(nseq, N_NEW, D_MODEL),
        kp5, vp5, new_pool_p, new_ffn_p,
        ks.reshape(1, nseq, N_NEW, N_HEADS, HEAD_DIM),
        vs.reshape(1, nseq, N_NEW, N_HEADS, HEAD_DIM),
        new_pool_s,
        hist_s.reshape(1, nseq, CONV_WIDTH - 1, 2 * D_FF),
    )
```
